```python
import math
import jax, jax.numpy as jnp
from jax import lax
import numpy as np

D_MODEL = 1024
BATCH = 32
SEQ = 2048
DEPTH = 2
DEC_BATCH = 2
DEC_SEQ = 8192
PAST_LEN = 128

N_MIXERS = 2
HEAD_DIM = 64
HEADS_PER_GROUP = D_MODEL // HEAD_DIM
DILATED_GROUPS = ((128, 1), (512, 4), (2048, 16))
N_GROUPS = len(DILATED_GROUPS)
N_ATTN_HEADS = N_GROUPS * HEADS_PER_GROUP
NUM_BUCKETS = 32
MAX_DISTANCE = 1024
HYENA_ORDER = 2
HYENA_DIRS = 2
HYENA_EMB_DIM = 33
HYENA_FILTER_WIDTH = 64
SHORT_CONV = 3
FAST_DECAY_PCT = 0.3
SLOW_DECAY_PCT = 1.5
DECAY_TARGET = 1e-2
D_FF = 2816
RMS_EPS = 1e-6
N_ATTN_LAYERS = (DEPTH + 1) // 2
N_HYENA_LAYERS = DEPTH // 2

kernel_name = 'hybrid_dilated_attn_hyena_encoder'


def _rms_norm(x, g):
    xf = x.astype(jnp.float32)
    y = xf * lax.rsqrt(jnp.mean(xf * xf, axis=-1, keepdims=True) + RMS_EPS)
    return (y * g.astype(jnp.float32)).astype(x.dtype)


def _swiglu(xn, w_gate_up, w_down):
    gate, up = jnp.split(xn @ w_gate_up, 2, axis=-1)
    return (jax.nn.silu(gate) * up) @ w_down


def _t5_bucket(rel):
    nb = NUM_BUCKETS // 2
    max_exact = nb // 2
    ret = (rel > 0).astype(np.int32) * nb
    n = np.abs(rel)
    large = max_exact + (np.log(np.maximum(n, 1) / max_exact)
                         / math.log(MAX_DISTANCE / max_exact) * (nb - max_exact)).astype(np.int32)
    large = np.minimum(large, nb - 1)
    return ret + np.where(n < max_exact, n, large)


def _dilated_group_attention(q, k, v, bias_g, dilation, half_window):
    B, S, H, E = q.shape
    d = dilation
    w = half_window // d
    lc = S // d
    nb = -(-lc // w)
    lp = nb * w

    def classes(t):
        return t.reshape(B, lc, d, H, E).transpose(0, 2, 1, 3, 4).reshape(B * d, lc, H, E)

    def banded(t):
        tp = jnp.pad(classes(t), ((0, 0), (w, lp - lc + w), (0, 0), (0, 0)))
        tp = tp.reshape(B * d, nb + 2, w, H, E)
        return jnp.concatenate([tp[:, :-2], tp[:, 1:-1], tp[:, 2:]], axis=2)

    qc = jnp.pad(classes(q), ((0, 0), (0, lp - lc), (0, 0), (0, 0))).reshape(B * d, nb, w, H, E)
    kb = banded(k)
    vb = banded(v)
    s = jnp.einsum('bnqhe,bnkhe->bnhqk', qc, kb,
                   preferred_element_type=jnp.float32) * (E ** -0.5)
    delta = np.arange(3 * w)[None, :] - w - np.arange(w)[:, None]
    bias = bias_g.astype(jnp.float32)[_t5_bucket(delta * d)].transpose(2, 0, 1)
    key_pos = np.arange(nb)[:, None] * w - w + np.arange(3 * w)[None, :]
    mask = (np.abs(delta) <= w)[None] & ((key_pos >= 0) & (key_pos < lc))[:, None, :]
    s = jnp.where(mask[None, :, None], s + bias, -jnp.inf)
    m = jnp.max(s, axis=-1, keepdims=True)
    p = jnp.exp(s - m)
    l = jnp.sum(p, axis=-1, keepdims=True)
    o = jnp.einsum('bnhqk,bnkhe->bnqhe', p, vb) / jnp.moveaxis(l, 2, 3)
    lse = jnp.moveaxis((m + jnp.log(l))[..., 0], 2, 3)
    o = o.reshape(B, d, lp, H, E)[:, :, :lc].transpose(0, 2, 1, 3, 4).reshape(B, S, H, E)
    lse = lse.reshape(B, d, lp, H)[:, :, :lc].transpose(0, 2, 1, 3).reshape(B, S, H)
    return o, lse


def _dilated_attention(xn, w_qkv, w_o, rel_bias):
    B, S, _ = xn.shape
    qkv = (xn @ w_qkv).reshape(B, S, N_GROUPS, 3, HEADS_PER_GROUP, HEAD_DIM)
    outs, lses = [], []
    for g, (window, dil) in enumerate(DILATED_GROUPS):
        o, lse = _dilated_group_attention(
            qkv[:, :, g, 0], qkv[:, :, g, 1], qkv[:, :, g, 2],
            rel_bias[:, g * HEADS_PER_GROUP:(g + 1) * HEADS_PER_GROUP], dil, window // 2)
        outs.append(o)
        lses.append(lse)
    alpha = jax.nn.softmax(jnp.stack(lses), axis=0)
    o = jnp.einsum('gbsh,gbshe->bshe', alpha, jnp.stack(outs))
    return o.reshape(B, S, D_MODEL).astype(xn.dtype) @ w_o


def _hyena_filters(L, w1, b1, w2, b2, w3, b3, freq, w4):
    f32 = jnp.float32
    t = jnp.linspace(0.0, 1.0, L, dtype=f32)[:, None]
    bands = (HYENA_EMB_DIM - 1) // 2
    wpos = 2.0 * math.pi * jnp.arange(L, dtype=f32)[:, None] / L
    fr = jnp.linspace(1e-4, bands - 1, bands, dtype=f32)[None, :]
    z = jnp.concatenate([t, jnp.cos(fr * wpos), -jnp.sin(fr * wpos)], axis=-1)
    fq = freq.astype(f32)
    h = jnp.sin(fq * (z @ w1.astype(f32) + b1.astype(f32)))
    h = jnp.sin(fq * (h @ w2.astype(f32) + b2.astype(f32)))
    h = jnp.sin(fq * (h @ w3.astype(f32) + b3.astype(f32)))
    h = h @ w4.astype(f32)
    max_decay = math.log(DECAY_TARGET) / FAST_DECAY_PCT
    min_decay = math.log(DECAY_TARGET) / SLOW_DECAY_PCT
    deltas = jnp.linspace(min_decay, max_decay, D_MODEL, dtype=f32)
    decay = jnp.exp(-t * jnp.abs(deltas))
    return h.reshape(L, HYENA_ORDER, HYENA_DIRS, D_MODEL) * decay[:, None, None, :]


def _bidirectional_spectrum(h):
    fwd = h[:, :, 0]
    bwd = h[:, :, 1]
    k = jnp.concatenate([fwd, jnp.zeros_like(fwd[:1]), bwd[1:][::-1]], axis=0)
    return jnp.fft.rfft(k, axis=0)


def _hyena(xn, w_in, conv_w, conv_b, w1, b1, w2, b2, w3, b3, freq, w4, skip, w_out):
    B, L, D = xn.shape
    u = xn @ w_in
    pad = (SHORT_CONV - 1) // 2
    up = jnp.pad(u, ((0, 0), (pad, pad), (0, 0)))
    u = sum(up[:, j:j + L] * conv_w[j] for j in range(SHORT_CONV)) + conv_b
    u = u.astype(jnp.float32)
    x1, x2, z = u[..., :D], u[..., D:2 * D], u[..., 2 * D:]
    kf = _bidirectional_spectrum(_hyena_filters(L, w1, b1, w2, b2, w3, b3, freq, w4))
    skipf = skip.astype(jnp.float32)
    for o, gate in enumerate((x1, x2)):
        zf = jnp.fft.rfft(z, n=2 * L, axis=1)
        conv = jnp.fft.irfft(zf * kf[None, :, o], n=2 * L, axis=1)[:, :L]
        z = gate * (conv + z * skipf[o])
    return z.astype(xn.dtype) @ w_out


def _trunk(x, rel_bias, ffn1_norm, ffn1_w_gate_up, ffn1_w_down, mixer_norm, attn_w_qkv, attn_w_o,
           hyena_w_in, hyena_conv_w, hyena_conv_b, hyena_filt_w1, hyena_filt_b1, hyena_filt_w2,
           hyena_filt_b2, hyena_filt_w3, hyena_filt_b3, hyena_filt_freq, hyena_filt_w4, hyena_skip,
           hyena_w_out, ffn2_norm, ffn2_w_gate_up, ffn2_w_down, final_norm):
    for i in range(DEPTH):
        x = x + 0.5 * _swiglu(_rms_norm(x, ffn1_norm[i]), ffn1_w_gate_up[i], ffn1_w_down[i])
        h = _rms_norm(x, mixer_norm[i])
        j = i // N_MIXERS
        if i % N_MIXERS == 0:
            x = x + _dilated_attention(h, attn_w_qkv[j], attn_w_o[j], rel_bias)
        else:
            x = x + _hyena(h, hyena_w_in[j], hyena_conv_w[j], hyena_conv_b[j],
                           hyena_filt_w1[j], hyena_filt_b1[j], hyena_filt_w2[j], hyena_filt_b2[j],
                           hyena_filt_w3[j], hyena_filt_b3[j], hyena_filt_freq[j], hyena_filt_w4[j],
                           hyena_skip[j], hyena_w_out[j])
        x = x + 0.5 * _swiglu(_rms_norm(x, ffn2_norm[i]), ffn2_w_gate_up[i], ffn2_w_down[i])
    return _rms_norm(x, final_norm)


def setup_inputs(seed: int = 0) -> dict:
    key = jax.random.key(seed)
    ks = jax.random.split(key, 32)
    f32 = jnp.float32

    def nrm(k, shape, scale):
        return jax.random.normal(k, shape, f32) * scale

    D, F, NA, NH, FW = D_MODEL, D_FF, N_ATTN_LAYERS, N_HYENA_LAYERS, HYENA_FILTER_WIDTH
    return {
        'x_prompt': nrm(ks[0], (BATCH, SEQ, D), 1.0),
        'x_sample': nrm(ks[1], (DEC_BATCH, DEC_SEQ, D), 1.0),
        'rel_bias': nrm(ks[2], (NUM_BUCKETS, N_ATTN_HEADS), 0.5),
        'ffn1_norm': 1.0 + nrm(ks[3], (DEPTH, D), 0.05),
        'ffn1_w_gate_up': nrm(ks[4], (DEPTH, D, 2 * F), D ** -0.5),
        'ffn1_w_down': nrm(ks[5], (DEPTH, F, D), F ** -0.5),
        'mixer_norm': 1.0 + nrm(ks[6], (DEPTH, D), 0.05),
        'attn_w_qkv': nrm(ks[7], (NA, D, N_GROUPS * 3 * D), D ** -0.5),
        'attn_w_o': nrm(ks[8], (NA, D, D), D ** -0.5),
        'hyena_w_in': nrm(ks[9], (NH, D, 3 * D), D ** -0.5),
        'hyena_conv_w': nrm(ks[10], (NH, SHORT_CONV, 3 * D), SHORT_CONV ** -0.5),
        'hyena_conv_b': nrm(ks[11], (NH, 3 * D), 0.02),
        'hyena_filt_w1': nrm(ks[12], (NH, HYENA_EMB_DIM, FW), HYENA_EMB_DIM ** -0.5),
        'hyena_filt_b1': nrm(ks[13], (NH, FW), 0.1),
        'hyena_filt_w2': nrm(ks[14], (NH, FW, FW), FW ** -0.5),
        'hyena_filt_b2': nrm(ks[15], (NH, FW), 0.1),
        'hyena_filt_w3': nrm(ks[16], (NH, FW, FW), FW ** -0.5),
        'hyena_filt_b3': nrm(ks[17], (NH, FW), 0.1),
        'hyena_filt_freq': 1.0 + nrm(ks[18], (NH, FW), 0.1),
        'hyena_filt_w4': nrm(ks[19], (NH, FW, HYENA_ORDER * HYENA_DIRS * D), 0.03 * FW ** -0.5),
        'hyena_skip': nrm(ks[20], (NH, HYENA_ORDER, D), 1.0),
        'hyena_w_out': nrm(ks[21], (NH, D, D), D ** -0.5),
        'ffn2_norm': 1.0 + nrm(ks[22], (DEPTH, D), 0.05),
        'ffn2_w_gate_up': nrm(ks[23], (DEPTH, D, 2 * F), D ** -0.5),
        'ffn2_w_down': nrm(ks[24], (DEPTH, F, D), F ** -0.5),
        'final_norm': 1.0 + nrm(ks[25], (D,), 0.05),
    }


def reference(x_prompt, x_sample, rel_bias, ffn1_norm, ffn1_w_gate_up, ffn1_w_down, mixer_norm,
              attn_w_qkv, attn_w_o, hyena_w_in, hyena_conv_w, hyena_conv_b, hyena_filt_w1,
              hyena_filt_b1, hyena_filt_w2, hyena_filt_b2, hyena_filt_w3, hyena_filt_b3,
              hyena_filt_freq, hyena_filt_w4, hyena_skip, hyena_w_out, ffn2_norm, ffn2_w_gate_up,
              ffn2_w_down, final_norm):
    weights = (rel_bias, ffn1_norm, ffn1_w_gate_up, ffn1_w_down, mixer_norm, attn_w_qkv, attn_w_o,
               hyena_w_in, hyena_conv_w, hyena_conv_b, hyena_filt_w1, hyena_filt_b1, hyena_filt_w2,
               hyena_filt_b2, hyena_filt_w3, hyena_filt_b3, hyena_filt_freq, hyena_filt_w4,
               hyena_skip, hyena_w_out, ffn2_norm, ffn2_w_gate_up, ffn2_w_down, final_norm)
    y_prompt = _trunk(x_prompt, *weights)
    y_sample = _trunk(x_sample, *weights)
    return (y_prompt, y_sample)
```

```python
import functools
import math

import numpy as np
import jax
import jax.numpy as jnp
from jax import lax
from jax.experimental import pallas as pl
from jax.experimental.pallas import tpu as pltpu

D_MODEL = 1024
HEAD_DIM = 64
HEADS_PER_GROUP = D_MODEL // HEAD_DIM
DILATED_GROUPS = ((128, 1), (512, 4), (2048, 16))
N_GROUPS = len(DILATED_GROUPS)
NUM_BUCKETS = 32
MAX_DISTANCE = 1024
HYENA_EMB_DIM = 33
HYENA_FILTER_WIDTH = 64
FAST_DECAY_PCT = 0.3
SLOW_DECAY_PCT = 1.5
DECAY_TARGET = 1e-2
D_FF = 2816
RMS_EPS = 1e-6

_F32 = jnp.float32
_BF16 = jnp.bfloat16
_V7X_VMEM_LIMIT_BYTES = 56 * 1024 * 1024
_LANES = 128
_NEG = -1e30

_TQ = 128
_HALF_W = 64
_TK = _TQ + 2 * _HALF_W
_FFN_CHUNKS = 2
_CONV_P = 512
_CONV_ROWS = 32


def _params(*sem):
    return pltpu.CompilerParams(dimension_semantics=sem, vmem_limit_bytes=_V7X_VMEM_LIMIT_BYTES)


def _const_spec(shape):
    nd = len(shape)
    return pl.BlockSpec(shape, lambda *_: (0,) * nd, pipeline_mode=pl.Buffered(1))


def _rms(x, g):
    y = x * lax.rsqrt(jnp.mean(x * x, axis=-1, keepdims=True) + RMS_EPS)
    return y * g


def _ffn_kernel(*refs, n_chunks, final):
    if final:
        x_ref, g_ref, wg_ref, wu_ref, wd_ref, gf_ref, o_ref = refs
    else:
        x_ref, g_ref, wg_ref, wu_ref, wd_ref, o_ref = refs
    x = x_ref[...]
    xb = _rms(x, g_ref[...]).astype(_BF16)
    acc = jnp.zeros(x.shape, _F32)
    for c in range(n_chunks):
        gate = jnp.dot(xb, wg_ref[c], preferred_element_type=_F32)
        up = jnp.dot(xb, wu_ref[c], preferred_element_type=_F32)
        act = (gate * jax.nn.sigmoid(gate)) * up
        acc = acc + jnp.dot(act.astype(_BF16), wd_ref[c], preferred_element_type=_F32)
    y = x + 0.5 * acc
    if final:
        y = _rms(y, gf_ref[...])
    o_ref[...] = y


def _ffn(x2d, norm_g, wg, wu, wd, final_g=None, tm=512):
    t, d = x2d.shape
    final = final_g is not None
    in_specs = [pl.BlockSpec((tm, d), lambda i: (i, 0)), _const_spec((1, d)),
                _const_spec(wg.shape), _const_spec(wu.shape), _const_spec(wd.shape)]
    args = [x2d, norm_g.reshape(1, d), wg, wu, wd]
    if final:
        in_specs.append(_const_spec((1, d)))
        args.append(final_g.reshape(1, d))
    return pl.pallas_call(
        functools.partial(_ffn_kernel, n_chunks=wg.shape[0], final=final),
        grid=(t // tm,), in_specs=in_specs,
        out_specs=pl.BlockSpec((tm, d), lambda i: (i, 0)),
        out_shape=jax.ShapeDtypeStruct((t, d), _F32),
        compiler_params=_params("parallel"), name="ffn")(*args)


def _ffn_weights(w_gate_up, w_down):
    d, f2 = w_gate_up.shape
    f = f2 // 2
    fc = f // _FFN_CHUNKS
    wg = w_gate_up[:, :f].reshape(d, _FFN_CHUNKS, fc).transpose(1, 0, 2).astype(_BF16)
    wu = w_gate_up[:, f:].reshape(d, _FFN_CHUNKS, fc).transpose(1, 0, 2).astype(_BF16)
    wd = w_down.reshape(_FFN_CHUNKS, fc, d).astype(_BF16)
    return wg, wu, wd


def _matmul_residual_kernel(a_ref, w_ref, x_ref, o_ref):
    o_ref[...] = x_ref[...] + jnp.dot(a_ref[...], w_ref[...], preferred_element_type=_F32)


def _matmul_residual(a2d, w_bf16, x2d, tm=1024):
    t, d = x2d.shape
    k = a2d.shape[1]
    return pl.pallas_call(
        _matmul_residual_kernel, grid=(t // tm,),
        in_specs=[pl.BlockSpec((tm, k), lambda i: (i, 0)), _const_spec(w_bf16.shape),
                  pl.BlockSpec((tm, d), lambda i: (i, 0))],
        out_specs=pl.BlockSpec((tm, d), lambda i: (i, 0)),
        out_shape=jax.ShapeDtypeStruct((t, d), _F32),
        compiler_params=_params("parallel"), name="matmul_residual")(a2d, w_bf16, x2d)


def _t5_bucket(rel):
    nb = NUM_BUCKETS // 2
    max_exact = nb // 2
    ret = (rel > 0).astype(np.int32) * nb
    n = np.abs(rel)
    large = max_exact + (np.log(np.maximum(n, 1) / max_exact)
                         / math.log(MAX_DISTANCE / max_exact) * (nb - max_exact)).astype(np.int32)
    large = np.minimum(large, nb - 1)
    return ret + np.where(n < max_exact, n, large)


def _attn_bias_tiles(rel_bias_g, dilation):
    delta = np.arange(_TK)[None, :] - _HALF_W - np.arange(_TQ)[:, None]
    bucket = _t5_bucket(delta * dilation)
    tiles = rel_bias_g.astype(_F32)[bucket]
    tiles = jnp.where((np.abs(delta) <= _HALF_W)[:, :, None], tiles, _NEG)
    tiles = tiles.transpose(2, 0, 1)
    return tiles.reshape(HEADS_PER_GROUP // 2, 2 * _TQ, _TK)


def _attn_kernel(q_ref, kp_ref, kc_ref, kn_ref, vp_ref, vc_ref, vn_ref, bias_ref,
                 o_ref, lse_ref, *, lc):
    i = pl.program_id(2)
    kwin = jnp.concatenate([kp_ref[_TQ - _HALF_W:, :], kc_ref[...], kn_ref[:_HALF_W, :]], axis=0)
    vwin = jnp.concatenate([vp_ref[_TQ - _HALF_W:, :], vc_ref[...], vn_ref[:_HALF_W, :]], axis=0)
    kpos = i * _TQ - _HALF_W + lax.broadcasted_iota(jnp.int32, (1, _TK), 1)
    valid = (kpos >= 0) & (kpos < lc)
    lane = lax.broadcasted_iota(jnp.int32, (1, _LANES), 1)
    low = lane < HEAD_DIM
    zero = jnp.zeros((), _BF16)
    lse_tile = jnp.zeros((_TQ, _LANES), _F32)
    for hp in range(HEADS_PER_GROUP // 2):
        sl = slice(hp * _LANES, (hp + 1) * _LANES)
        q2 = q_ref[:, sl] * jnp.asarray(HEAD_DIM ** -0.5, _BF16)
        qq = jnp.concatenate([jnp.where(low, q2, zero), jnp.where(low, zero, q2)], axis=0)
        s = lax.dot_general(qq, kwin[:, sl], (((1,), (1,)), ((), ())),
                            preferred_element_type=_F32)
        s = jnp.where(valid, s + bias_ref[hp], _NEG)
        m = jnp.max(s, axis=-1, keepdims=True)
        p = jnp.exp(s - m)
        l = jnp.sum(p, axis=-1, keepdims=True)
        pv = jnp.dot(p.astype(_BF16), vwin[:, sl], preferred_element_type=_F32)
        pv = pv / l
        o_ref[:, sl] = jnp.where(low, pv[:_TQ], pv[_TQ:]).astype(o_ref.dtype)
        lse = m + jnp.log(l)
        lse_tile = jnp.where(lane == 2 * hp, lse[:_TQ], lse_tile)
        lse_tile = jnp.where(lane == 2 * hp + 1, lse[_TQ:], lse_tile)
    lse_ref[...] = lse_tile


def _qkv_group_kernel(x_ref, g_ref, w_ref, o_ref, xn_ref, xp_ref, *, d, tm):
    rows = tm // d
    xn = _rms(x_ref[...], g_ref[...])
    if d == 1:
        xp_ref[...] = xn.astype(_BF16)
    else:
        for c in range(D_MODEL // _LANES):
            xn_ref[c] = xn[:, c * _LANES:(c + 1) * _LANES]
        for r in range(d):
            for c in range(D_MODEL // _LANES):
                xp_ref[r * rows:(r + 1) * rows, c * _LANES:(c + 1) * _LANES] = (
                    xn_ref[c, pl.ds(r, rows, stride=d), :].astype(_BF16))
    for j in range(3):
        cs = slice(j * D_MODEL, (j + 1) * D_MODEL)
        y = jnp.dot(xp_ref[...], w_ref[:, cs], preferred_element_type=_F32).astype(o_ref.dtype)
        for r in range(d):
            o_ref[r, :, cs] = y[r * rows:(r + 1) * rows]


def _qkv_group(x, norm_g, w_g_bf16, dilation, tm=512):
    b, s, dm = x.shape
    d = dilation
    n3 = w_g_bf16.shape[1]
    return pl.pallas_call(
        functools.partial(_qkv_group_kernel, d=d, tm=tm), grid=(b, s // tm),
        in_specs=[pl.BlockSpec((None, tm, dm), lambda bi, i: (bi, i, 0)), _const_spec((1, dm)),
                  _const_spec(w_g_bf16.shape)],
        out_specs=pl.BlockSpec((None, d, tm // d, n3), lambda bi, i: (bi, 0, i, 0)),
        out_shape=jax.ShapeDtypeStruct((b, d, s // d, n3), _BF16),
        scratch_shapes=[pltpu.VMEM((dm // _LANES, tm, _LANES), _F32), pltpu.VMEM((tm, dm), _BF16)],
        compiler_params=_params("parallel", "parallel"), name=f"qkv_d{d}")(
            x, norm_g.reshape(1, dm), w_g_bf16)


def _group_attention(qkv, bias_tiles):
    b, d, lc, _ = qkv.shape
    nt = lc // _TQ

    def spec(which, shift):
        def imap(bi, r, i):
            blk = i + shift
            if shift < 0:
                blk = jnp.maximum(blk, 0)
            elif shift > 0:
                blk = jnp.minimum(blk, nt - 1)
            return (bi, r, blk, which)
        return pl.BlockSpec((None, None, _TQ, D_MODEL), imap)

    return pl.pallas_call(
        functools.partial(_attn_kernel, lc=lc), grid=(b, d, nt),
        in_specs=[spec(0, 0), spec(1, -1), spec(1, 0), spec(1, 1), spec(2, -1), spec(2, 0),
                  spec(2, 1), _const_spec(bias_tiles.shape)],
        out_specs=[pl.BlockSpec((None, None, _TQ, D_MODEL), lambda bi, r, i: (bi, r, i, 0)),
                   pl.BlockSpec((None, None, _TQ, _LANES), lambda bi, r, i: (bi, r, i, 0))],
        out_shape=[jax.ShapeDtypeStruct((b, d, lc, D_MODEL), _BF16),
                   jax.ShapeDtypeStruct((b, d, lc, _LANES), _F32)],
        compiler_params=_params("parallel", "parallel", "arbitrary"), name=f"attn_d{d}")(
            qkv, qkv, qkv, qkv, qkv, qkv, qkv, bias_tiles)


def _merge_kernel(o0_ref, o1_ref, o2_ref, l0_ref, l1_ref, l2_ref, x_ref, e_ref, wo_ref, out_ref,
                  of_ref, lf_ref, *, tm):
    def natural(o_ref, l_ref, slot):
        d = o_ref.shape[0]
        if d == 1:
            return o_ref[0].astype(_F32), l_ref[0]
        rows = tm // d
        nblk = D_MODEL // _LANES
        for r in range(d):
            o_r = o_ref[r].astype(_F32)
            for c in range(nblk):
                of_ref[slot, c, pl.ds(r, rows, stride=d), :] = o_r[:, c * _LANES:(c + 1) * _LANES]
            lf_ref[slot, pl.ds(r, rows, stride=d), :] = l_ref[r]
        return jnp.concatenate([of_ref[slot, c] for c in range(nblk)], axis=1), lf_ref[slot]

    o0, l0 = natural(o0_ref, l0_ref, 0)
    o1, l1 = natural(o1_ref, l1_ref, 0)
    o2, l2 = natural(o2_ref, l2_ref, 1)
    m = jnp.maximum(jnp.maximum(l0, l1), l2)
    a0, a1, a2 = jnp.exp(l0 - m), jnp.exp(l1 - m), jnp.exp(l2 - m)
    inv = 1.0 / (a0 + a1 + a2)

    def expand(a):
        al = a * inv
        hi = al.astype(_BF16)
        lo = (al - hi.astype(_F32)).astype(_BF16)
        return jnp.dot(jnp.concatenate([hi, lo], axis=1), e_ref[...], preferred_element_type=_F32)

    merged = expand(a0) * o0 + expand(a1) * o1 + expand(a2) * o2
    out_ref[...] = x_ref[...] + jnp.dot(merged.astype(_BF16), wo_ref[...],
                                        preferred_element_type=_F32)


def _head_expand_matrix():
    e = np.zeros((2 * _LANES, D_MODEL), np.float32)
    for h in range(HEADS_PER_GROUP):
        e[h, h * HEAD_DIM:(h + 1) * HEAD_DIM] = 1.0
        e[_LANES + h, h * HEAD_DIM:(h + 1) * HEAD_DIM] = 1.0
    return jnp.asarray(e, _BF16)


def _merge(os_, lses, x, wo_bf16, tm=512):
    b, s, dm = x.shape
    e = _head_expand_matrix()

    def cls(a):
        d, w = a.shape[1], a.shape[3]
        return pl.BlockSpec((None, d, tm // d, w), lambda bi, i: (bi, 0, i, 0))

    row = pl.BlockSpec((None, tm, dm), lambda bi, i: (bi, i, 0))
    return pl.pallas_call(
        functools.partial(_merge_kernel, tm=tm), grid=(b, s // tm),
        in_specs=[cls(a) for a in os_] + [cls(a) for a in lses]
        + [row, _const_spec(e.shape), _const_spec(wo_bf16.shape)],
        out_specs=row, out_shape=jax.ShapeDtypeStruct((b, s, dm), _F32),
        scratch_shapes=[pltpu.VMEM((2, dm // _LANES, tm, _LANES), _F32),
                        pltpu.VMEM((2, tm, _LANES), _F32)],
        compiler_params=_params("parallel", "parallel"), name="attn_merge")(
            *os_, *lses, x, e, wo_bf16)


def _dilated_attention_layer(x, norm_g, w_qkv, w_o, rel_bias):
    n3 = 3 * D_MODEL
    os_, lses = [], []
    for g, (window, dil) in enumerate(DILATED_GROUPS):
        assert (window // 2) // dil == _HALF_W
        w_g = w_qkv[:, g * n3:(g + 1) * n3].astype(_BF16)
        qkv = _qkv_group(x, norm_g, w_g, dil)
        bias = _attn_bias_tiles(rel_bias[:, g * HEADS_PER_GROUP:(g + 1) * HEADS_PER_GROUP], dil)
        o, lse = _group_attention(qkv, bias)
        os_.append(o)
        lses.append(lse)
    return _merge(os_, lses, x, w_o.astype(_BF16))


def _hyena_in_kernel(xp_ref, x_ref, xn_ref, g_ref, w_ref, cw_ref, cb_ref,
                     x1_ref, x2_ref, v_ref, *, tm):
    i = pl.program_id(1)
    last = pl.num_programs(1) - 1
    xe = jnp.concatenate([xp_ref[...], x_ref[...], xn_ref[...]], axis=0)
    xb = _rms(xe, g_ref[...]).astype(_BF16)
    row = lax.broadcasted_iota(jnp.int32, (tm + 16, 1), 0)
    inside = ((row >= 8) | (i > 0)) & ((row < tm + 8) | (i < last))
    for j, o_ref in enumerate((x1_ref, x2_ref, v_ref)):
        cs = slice(j * D_MODEL, (j + 1) * D_MODEL)
        u = jnp.dot(xb, w_ref[:, cs], preferred_element_type=_F32)
        u = jnp.where(inside, u, 0.0)
        cw = cw_ref[:, cs]
        y = (u[7:tm + 7] * cw[0:1] + u[8:tm + 8] * cw[1:2] + u[9:tm + 9] * cw[2:3]) + cb_ref[:, cs]
        o_ref[...] = y.astype(o_ref.dtype)


def _hyena_in(x, norm_g, w_in_bf16, conv_w, conv_b, tm=512):
    b, s, d = x.shape
    n3 = w_in_bf16.shape[1]
    nblk8 = s // 8
    r8 = tm // 8
    main = pl.BlockSpec((None, tm, d), lambda bi, i: (bi, i, 0))
    prev = pl.BlockSpec((None, 8, d), lambda bi, i: (bi, jnp.maximum(i * r8 - 1, 0), 0))
    nxt = pl.BlockSpec((None, 8, d), lambda bi, i: (bi, jnp.minimum((i + 1) * r8, nblk8 - 1), 0))
    out = jax.ShapeDtypeStruct((b, s, d), _BF16)
    return pl.pallas_call(
        functools.partial(_hyena_in_kernel, tm=tm), grid=(b, s // tm),
        in_specs=[prev, main, nxt, _const_spec((1, d)), _const_spec(w_in_bf16.shape),
                  _const_spec((3, n3)), _const_spec((1, n3))],
        out_specs=[main, main, main], out_shape=[out, out, out],
        compiler_params=_params("parallel", "arbitrary"), name="hyena_in")(
            x, x, x, norm_g.reshape(1, d), w_in_bf16, conv_w.astype(_F32),
            conv_b.astype(_F32).reshape(1, n3))


def _filter_kernel(fr_ref, w1_ref, b1_ref, w2_ref, b2_ref, w3_ref, b3_ref, fq_ref, w4_ref,
                   dl_ref, k0_ref, k1_ref, *, seq_len, tr):
    hi = lax.Precision.HIGHEST
    r = pl.program_id(0) * tr + lax.broadcasted_iota(jnp.int32, (tr, 1), 0)
    fwd = r >= seq_len
    pos = jnp.abs(r - seq_len).astype(_F32)
    t = pos / (seq_len - 1.0)
    wpos = (2.0 * math.pi) * pos / seq_len
    lane = lax.broadcasted_iota(jnp.int32, (1, _LANES), 1)
    bands = (HYENA_EMB_DIM - 1) // 2
    ang = fr_ref[...] * wpos
    z = jnp.where(lane == 0, t,
                  jnp.where(lane <= bands, jnp.cos(ang),
                            jnp.where(lane <= 2 * bands, -jnp.sin(ang), 0.0)))
    fq = fq_ref[...]
    h = jnp.sin(fq * (jnp.dot(z, w1_ref[...], precision=hi, preferred_element_type=_F32) + b1_ref[...]))
    h = jnp.sin(fq * (jnp.dot(h, w2_ref[...], precision=hi, preferred_element_type=_F32) + b2_ref[...]))
    h = jnp.sin(fq * (jnp.dot(h, w3_ref[...], precision=hi, preferred_element_type=_F32) + b3_ref[...]))
    decay = jnp.exp(-t * jnp.abs(dl_ref[...]))
    live = r > 0
    for o, k_ref in enumerate((k0_ref, k1_ref)):
        base = o * 2 * D_MODEL
        hf = jnp.dot(h, w4_ref[:, base:base + D_MODEL], precision=hi, preferred_element_type=_F32)
        hb = jnp.dot(h, w4_ref[:, base + D_MODEL:base + 2 * D_MODEL], precision=hi,
                     preferred_element_type=_F32)
        k_ref[...] = jnp.where(live, jnp.where(fwd, hf, hb) * decay, 0.0)


def _hyena_two_sided_filters(seq_len, w1, b1, w2, b2, w3, b3, freq, w4, tr=512):
    fw = HYENA_FILTER_WIDTH
    bands = (HYENA_EMB_DIM - 1) // 2
    fr = jnp.linspace(1e-4, bands - 1, bands, dtype=_F32)
    fr_lanes = jnp.zeros((1, _LANES), _F32).at[0, 1:1 + bands].set(fr).at[0, 1 + bands:1 + 2 * bands].set(fr)
    w1p = jnp.zeros((_LANES, fw), _F32).at[:HYENA_EMB_DIM].set(w1.astype(_F32))
    max_decay = math.log(DECAY_TARGET) / FAST_DECAY_PCT
    min_decay = math.log(DECAY_TARGET) / SLOW_DECAY_PCT
    deltas = jnp.linspace(min_decay, max_decay, D_MODEL, dtype=_F32).reshape(1, D_MODEL)
    vec = lambda a: a.astype(_F32).reshape(1, fw)
    out = jax.ShapeDtypeStruct((2 * seq_len, D_MODEL), _F32)
    cs = _const_spec
    return pl.pallas_call(
        functools.partial(_filter_kernel, seq_len=seq_len, tr=tr), grid=(2 * seq_len // tr,),
        in_specs=[cs((1, _LANES)), cs((_LANES, fw)), cs((1, fw)), cs((fw, fw)), cs((1, fw)),
                  cs((fw, fw)), cs((1, fw)), cs((1, fw)), cs(w4.shape), cs((1, D_MODEL))],
        out_specs=[pl.BlockSpec((tr, D_MODEL), lambda i: (i, 0))] * 2, out_shape=[out, out],
        compiler_params=_params("parallel"), name="hyena_filter")(
            fr_lanes, w1p, vec(b1), w2.astype(_F32), vec(b2), w3.astype(_F32), vec(b3), vec(freq),
            w4.astype(_F32), deltas)


def _dft_matrices(p):
    n = 2 * p
    k = np.arange(p)[:, None].astype(np.float64)
    s = np.arange(n)[None, :].astype(np.float64)
    ang = np.pi * k * s / p
    fre = np.cos(ang)
    fim = -np.sin(ang)
    fim[0, :] = np.cos(np.pi * s[0])
    fwd_full = np.concatenate([fre, fim], axis=0)
    tt = (p + np.arange(p))[:, None].astype(np.float64)
    kk = np.arange(p)[None, :].astype(np.float64)
    ang2 = np.pi * kk * tt / p
    are = 2.0 * np.cos(ang2) / n
    are[:, 0] = 1.0 / n
    aim = -2.0 * np.sin(ang2) / n
    aim[:, 0] = np.cos(np.pi * tt[:, 0]) / n
    inv = np.concatenate([are, aim], axis=1)
    return (jnp.asarray(fwd_full[:, :p], _BF16), jnp.asarray(fwd_full, _BF16), jnp.asarray(inv, _BF16))


def _filter_spectrum_kernel(ga_ref, gb_ref, f_ref, o_ref, *, p):
    ga = ga_ref[...].astype(_BF16)
    gb = gb_ref[...].astype(_BF16)
    acc = jnp.dot(f_ref[:, :p], ga, preferred_element_type=_F32)
    acc = acc + jnp.dot(f_ref[:, p:], gb, preferred_element_type=_F32)
    o_ref[...] = acc.astype(o_ref.dtype)


def _filter_spectrum(kt, fwd_full, p, tc=256):
    rows, d = kt.shape
    nseg = rows // p - 1
    return pl.pallas_call(
        functools.partial(_filter_spectrum_kernel, p=p), grid=(nseg, d // tc),
        in_specs=[pl.BlockSpec((p, tc), lambda q, c: (q, c)),
                  pl.BlockSpec((p, tc), lambda q, c: (q + 1, c)), _const_spec(fwd_full.shape)],
        out_specs=pl.BlockSpec((None, 2 * p, tc), lambda q, c: (q, 0, c)),
        out_shape=jax.ShapeDtypeStruct((nseg, 2 * p, d), _BF16),
        compiler_params=_params("parallel", "parallel"), name="hyena_filter_spectrum")(kt, kt, fwd_full)


def _long_conv_kernel(gate_ref, z_ref, kf_ref, skip_ref, ff_ref, ai_ref, o_ref, zf_ref, yf_ref,
                      *, p, nb, tc):
    ch = _CONV_ROWS

    def fwd_body(i, carry):
        zb = z_ref[pl.ds(pl.multiple_of(i * p, p), p), :]
        zf_ref[i] = jnp.dot(ff_ref[...], zb, preferred_element_type=_F32)
        return carry

    lax.fori_loop(0, nb, fwd_body, 0)
    skip = skip_ref[...]
    row0 = lax.broadcasted_iota(jnp.int32, (ch, 1), 0)

    def out_body(j, carry):
        def chunk_body(rc, c2):
            r_re = pl.multiple_of(rc * ch, ch)
            r_im = pl.multiple_of(p + rc * ch, ch)
            a = jnp.zeros((ch, tc), _F32)
            bm = jnp.zeros((ch, tc), _F32)
            cc = jnp.zeros((ch, tc), _F32)
            for i in range(nb):
                q = j - i + (nb - 1)
                zre = zf_ref[i, pl.ds(r_re, ch), :]
                zim = zf_ref[i, pl.ds(r_im, ch), :]
                kre = kf_ref[q, pl.ds(r_re, ch), :].astype(_F32)
                kim = kf_ref[q, pl.ds(r_im, ch), :].astype(_F32)
                a = a + zre * kre
                bm = bm + zim * kim
                cc = cc + (zre * kim + zim * kre)
            packed = (row0 + rc * ch) == 0
            yf_ref[pl.ds(r_re, ch), :] = jnp.where(packed, a, a - bm).astype(_BF16)
            yf_ref[pl.ds(r_im, ch), :] = jnp.where(packed, bm, cc).astype(_BF16)
            return c2

        lax.fori_loop(0, p // ch, chunk_body, 0)
        y = jnp.dot(ai_ref[...], yf_ref[...], preferred_element_type=_F32)
        rows = pl.ds(pl.multiple_of(j * p, p), p)
        zt = z_ref[rows, :].astype(_F32)
        gt = gate_ref[rows, :].astype(_F32)
        o_ref[rows, :] = (gt * (y + zt * skip)).astype(o_ref.dtype)
        return carry

    lax.fori_loop(0, nb, out_body, 0)


def _long_conv(gate, z, kf, skip_o, fwd_half, inv, p):
    b, l, d = z.shape
    nb = l // p
    tc = 256 if nb <= 8 else 128
    seq = pl.BlockSpec((None, l, tc), lambda c, bi: (bi, 0, c))
    return pl.pallas_call(
        functools.partial(_long_conv_kernel, p=p, nb=nb, tc=tc), grid=(d // tc, b),
        in_specs=[seq, seq, pl.BlockSpec((2 * nb - 1, 2 * p, tc), lambda c, bi: (0, 0, c)),
                  pl.BlockSpec((1, tc), lambda c, bi: (0, c)), _const_spec(fwd_half.shape),
                  _const_spec(inv.shape)],
        out_specs=seq, out_shape=jax.ShapeDtypeStruct((b, l, d), _BF16),
        scratch_shapes=[pltpu.VMEM((nb, 2 * p, tc), _F32), pltpu.VMEM((2 * p, tc), _BF16)],
        compiler_params=_params("parallel", "arbitrary"), name="hyena_long_conv")(
            gate, z, kf, skip_o.astype(_F32).reshape(1, d), fwd_half, inv)


def _hyena_layer(x, norm_g, w_in, conv_w, conv_b, w1, b1, w2, b2, w3, b3, freq, w4, skip, w_out):
    b, l, d = x.shape
    p = min(_CONV_P, l)
    x1, x2, v = _hyena_in(x, norm_g, w_in.astype(_BF16), conv_w, conv_b)
    k0, k1 = _hyena_two_sided_filters(l, w1, b1, w2, b2, w3, b3, freq, w4)
    fwd_half, fwd_full, inv = _dft_matrices(p)
    z = v
    for o, (gate, kt) in enumerate(((x1, k0), (x2, k1))):
        kf = _filter_spectrum(kt, fwd_full, p)
        z = _long_conv(gate, z, kf, skip[o], fwd_half, inv, p)
    y = _matmul_residual(z.reshape(b * l, d), w_out.astype(_BF16), x.reshape(b * l, d))
    return y.reshape(b, l, d)


def _trunk(x, rel_bias, ffn1_norm, ffn1_w, mixer_norm, attn_w_qkv, attn_w_o, hy, ffn2_norm, ffn2_w,
           final_norm):
    b, s, d = x.shape
    depth = ffn1_norm.shape[0]
    flat = lambda a: a.reshape(b * s, d)
    for i in range(depth):
        x = _ffn(flat(x), ffn1_norm[i], *ffn1_w[i]).reshape(b, s, d)
        j = i // 2
        if i % 2 == 0:
            x = _dilated_attention_layer(x, mixer_norm[i], attn_w_qkv[j], attn_w_o[j], rel_bias)
        else:
            x = _hyena_layer(x, mixer_norm[i], *[a[j] for a in hy])
        fin = final_norm if i == depth - 1 else None
        x = _ffn(flat(x), ffn2_norm[i], *ffn2_w[i], final_g=fin).reshape(b, s, d)
    return x


def kernel(x_prompt, x_sample, rel_bias, ffn1_norm, ffn1_w_gate_up, ffn1_w_down, mixer_norm, attn_w_qkv, attn_w_o, hyena_w_in, hyena_conv_w, hyena_conv_b, hyena_filt_w1, hyena_filt_b1, hyena_filt_w2, hyena_filt_b2, hyena_filt_w3, hyena_filt_b3, hyena_filt_freq, hyena_filt_w4, hyena_skip, hyena_w_out, ffn2_norm, ffn2_w_gate_up, ffn2_w_down, final_norm):
    depth = ffn1_norm.shape[0]
    ffn1_w = [_ffn_weights(ffn1_w_gate_up[i], ffn1_w_down[i]) for i in range(depth)]
    ffn2_w = [_ffn_weights(ffn2_w_gate_up[i], ffn2_w_down[i]) for i in range(depth)]
    hy = (hyena_w_in, hyena_conv_w, hyena_conv_b, hyena_filt_w1, hyena_filt_b1, hyena_filt_w2,
          hyena_filt_b2, hyena_filt_w3, hyena_filt_b3, hyena_filt_freq, hyena_filt_w4, hyena_skip,
          hyena_w_out)
    args = (rel_bias, ffn1_norm, ffn1_w, mixer_norm, attn_w_qkv, attn_w_o, hy, ffn2_norm, ffn2_w,
            final_norm)
    return (_trunk(x_prompt, *args), _trunk(x_sample, *args))
```

```python
import functools
import math

import numpy as np
import jax
import jax.numpy as jnp
from jax import lax
from jax.experimental import pallas as pl
from jax.experimental.pallas import tpu as pltpu

D_MODEL = 1024
HEAD_DIM = 64
HEADS_PER_GROUP = D_MODEL // HEAD_DIM
DILATED_GROUPS = ((128, 1), (512, 4), (2048, 16))
N_GROUPS = len(DILATED_GROUPS)
NUM_BUCKETS = 32
MAX_DISTANCE = 1024
HYENA_EMB_DIM = 33
HYENA_FILTER_WIDTH = 64
FAST_DECAY_PCT = 0.3
SLOW_DECAY_PCT = 1.5
DECAY_TARGET = 1e-2
D_FF = 2816
RMS_EPS = 1e-6

_F32 = jnp.float32
_BF16 = jnp.bfloat16
_V7X_VMEM_LIMIT_BYTES = 56 * 1024 * 1024
_LANES = 128
_NEG = -1e30

_TQ = 128
_HALF_W = 64
_TK = _TQ + 2 * _HALF_W
_FFN_CHUNKS = 2
_CONV_P = 512
_CONV_ROWS = 32


def _params(*sem):
    return pltpu.CompilerParams(dimension_semantics=sem, vmem_limit_bytes=_V7X_VMEM_LIMIT_BYTES)


def _const_spec(shape):
    nd = len(shape)
    return pl.BlockSpec(shape, lambda *_: (0,) * nd, pipeline_mode=pl.Buffered(1))


def _rms(x, g):
    y = x * lax.rsqrt(jnp.mean(x * x, axis=-1, keepdims=True) + RMS_EPS)
    return y * g


def _ffn_kernel(*refs, final):
    if final:
        x_ref, g_ref, wgu_ref, wd_ref, gf_ref, o_ref = refs
    else:
        x_ref, g_ref, wgu_ref, wd_ref, o_ref = refs
    f = wd_ref.shape[0]
    fc = f // _FFN_CHUNKS
    x = x_ref[...]
    xb = _rms(x, g_ref[...]).astype(_BF16)
    acc = jnp.zeros(x.shape, _F32)
    for c in range(_FFN_CHUNKS):
        gate = jnp.dot(xb, wgu_ref[:, c * fc:(c + 1) * fc], preferred_element_type=_F32)
        up = jnp.dot(xb, wgu_ref[:, f + c * fc:f + (c + 1) * fc], preferred_element_type=_F32)
        act = (gate * jax.nn.sigmoid(gate)) * up
        acc = acc + jnp.dot(act.astype(_BF16), wd_ref[c * fc:(c + 1) * fc, :],
                            preferred_element_type=_F32)
    y = x + 0.5 * acc
    if final:
        y = _rms(y, gf_ref[...])
    o_ref[...] = y


def _ffn(x2d, norm_g, wgu, wd, final_g=None, tm=512):
    t, d = x2d.shape
    final = final_g is not None
    in_specs = [pl.BlockSpec((tm, d), lambda i: (i, 0)), _const_spec((1, d)),
                _const_spec(wgu.shape), _const_spec(wd.shape)]
    args = [x2d, norm_g.reshape(1, d), wgu, wd]
    if final:
        in_specs.append(_const_spec((1, d)))
        args.append(final_g.reshape(1, d))
    return pl.pallas_call(
        functools.partial(_ffn_kernel, final=final),
        grid=(t // tm,), in_specs=in_specs,
        out_specs=pl.BlockSpec((tm, d), lambda i: (i, 0)),
        out_shape=jax.ShapeDtypeStruct((t, d), _F32),
        compiler_params=_params("parallel"), name="ffn")(*args)


def _ffn_weights(w_gate_up, w_down):
    return w_gate_up.astype(_BF16), w_down.astype(_BF16)


def _matmul_residual_kernel(a_ref, w_ref, x_ref, o_ref):
    o_ref[...] = x_ref[...] + jnp.dot(a_ref[...], w_ref[...], preferred_element_type=_F32)


def _matmul_residual(a2d, w_bf16, x2d, tm=1024):
    t, d = x2d.shape
    k = a2d.shape[1]
    return pl.pallas_call(
        _matmul_residual_kernel, grid=(t // tm,),
        in_specs=[pl.BlockSpec((tm, k), lambda i: (i, 0)), _const_spec(w_bf16.shape),
                  pl.BlockSpec((tm, d), lambda i: (i, 0))],
        out_specs=pl.BlockSpec((tm, d), lambda i: (i, 0)),
        out_shape=jax.ShapeDtypeStruct((t, d), _F32),
        compiler_params=_params("parallel"), name="matmul_residual")(a2d, w_bf16, x2d)


def _t5_bucket(rel):
    nb = NUM_BUCKETS // 2
    max_exact = nb // 2
    ret = (rel > 0).astype(np.int32) * nb
    n = np.abs(rel)
    large = max_exact + (np.log(np.maximum(n, 1) / max_exact)
                         / math.log(MAX_DISTANCE / max_exact) * (nb - max_exact)).astype(np.int32)
    large = np.minimum(large, nb - 1)
    return ret + np.where(n < max_exact, n, large)


def _attn_bucket_tables(dilation):
    delta = np.arange(_TK)[None, :] - _HALF_W - np.arange(_TQ)[:, None]
    bucket = _t5_bucket(delta * dilation)
    band = np.abs(delta) <= _HALF_W
    kk = np.arange(_TK)[None, :]
    tables = []
    for variant in range(4):
        ok = band
        if variant & 1:
            ok = ok & (kk >= _HALF_W)
        if variant & 2:
            ok = ok & (kk < _TQ + _HALF_W)
        tables.append(np.where(ok, bucket, -1))
    return jnp.asarray(np.stack(tables), jnp.int32)


def _bias_tiles_kernel(rb_ref, bucket_ref, o_ref):
    hp = pl.program_id(1)
    bucket = bucket_ref[...]
    for hh in range(2):
        acc = jnp.full((_TQ, _TK), _NEG, _F32)
        for bkt in range(NUM_BUCKETS):
            acc = jnp.where(bucket == bkt, rb_ref[2 * hp + hh, bkt], acc)
        o_ref[hh * _TQ:(hh + 1) * _TQ, :] = acc


def _attn_bias_tiles(rel_bias_g, dilation):
    tables = _attn_bucket_tables(dilation)
    npair = HEADS_PER_GROUP // 2
    return pl.pallas_call(
        _bias_tiles_kernel, grid=(4, npair),
        in_specs=[pl.BlockSpec(memory_space=pltpu.SMEM),
                  pl.BlockSpec((None, _TQ, _TK), lambda v, hp: (v, 0, 0))],
        out_specs=pl.BlockSpec((None, None, 2 * _TQ, _TK), lambda v, hp: (v, hp, 0, 0)),
        out_shape=jax.ShapeDtypeStruct((4, npair, 2 * _TQ, _TK), _F32),
        compiler_params=_params("parallel", "parallel"), name="attn_bias_tiles")(
            rel_bias_g.astype(_F32).T, tables)


def _attn_kernel(q_ref, kp_ref, kc_ref, kn_ref, vp_ref, vc_ref, vn_ref, bias_ref,
                 o_ref, lse_ref):
    i = pl.program_id(2)
    variant = jnp.where(i == 0, 1, 0) + jnp.where(i == pl.num_programs(2) - 1, 2, 0)
    kwin = jnp.concatenate([kp_ref[_TQ - _HALF_W:, :], kc_ref[...], kn_ref[:_HALF_W, :]], axis=0)
    vwin = jnp.concatenate([vp_ref[_TQ - _HALF_W:, :], vc_ref[...], vn_ref[:_HALF_W, :]], axis=0)
    lane = lax.broadcasted_iota(jnp.int32, (1, _LANES), 1)
    low = lane < HEAD_DIM
    zero = jnp.zeros((), _BF16)
    lse_tile = jnp.zeros((_TQ, _LANES), _F32)
    for hp in range(HEADS_PER_GROUP // 2):
        sl = slice(hp * _LANES, (hp + 1) * _LANES)
        q2 = q_ref[:, sl] * jnp.asarray(HEAD_DIM ** -0.5, _BF16)
        qq = jnp.concatenate([jnp.where(low, q2, zero), jnp.where(low, zero, q2)], axis=0)
        s = lax.dot_general(qq, kwin[:, sl], (((1,), (1,)), ((), ())),
                            preferred_element_type=_F32)
        s = s + bias_ref[variant, hp]
        m = jnp.max(s, axis=-1, keepdims=True)
        p = jnp.exp(s - m)
        l = jnp.sum(p, axis=-1, keepdims=True)
        pv = jnp.dot(p.astype(_BF16), vwin[:, sl], preferred_element_type=_F32)
        pv = pv / l
        o_ref[:, sl] = jnp.where(low, pv[:_TQ], pv[_TQ:]).astype(o_ref.dtype)
        lse = m + jnp.log(l)
        lse_tile = jnp.where(lane == 2 * hp, lse[:_TQ], lse_tile)
        lse_tile = jnp.where(lane == 2 * hp + 1, lse[_TQ:], lse_tile)
    lse_ref[...] = lse_tile


def _qkv_group_kernel(x_ref, g_ref, w_ref, o_ref, xn_ref, xp_ref, *, d, tm):
    rows = tm // d
    xn = _rms(x_ref[...], g_ref[...])
    if d == 1:
        xp_ref[...] = xn.astype(_BF16)
    else:
        for c in range(D_MODEL // _LANES):
            xn_ref[c] = xn[:, c * _LANES:(c + 1) * _LANES]
        for r in range(d):
            for c in range(D_MODEL // _LANES):
                xp_ref[r * rows:(r + 1) * rows, c * _LANES:(c + 1) * _LANES] = (
                    xn_ref[c, pl.ds(r, rows, stride=d), :].astype(_BF16))
    for j in range(3):
        cs = slice(j * D_MODEL, (j + 1) * D_MODEL)
        y = jnp.dot(xp_ref[...], w_ref[:, cs], preferred_element_type=_F32).astype(o_ref.dtype)
        for r in range(d):
            o_ref[r, :, cs] = y[r * rows:(r + 1) * rows]


def _qkv_group(x, norm_g, w_g_bf16, dilation, tm=512):
    b, s, dm = x.shape
    d = dilation
    n3 = w_g_bf16.shape[1]
    return pl.pallas_call(
        functools.partial(_qkv_group_kernel, d=d, tm=tm), grid=(b, s // tm),
        in_specs=[pl.BlockSpec((None, tm, dm), lambda bi, i: (bi, i, 0)), _const_spec((1, dm)),
                  _const_spec(w_g_bf16.shape)],
        out_specs=pl.BlockSpec((None, d, tm // d, n3), lambda bi, i: (bi, 0, i, 0)),
        out_shape=jax.ShapeDtypeStruct((b, d, s // d, n3), _BF16),
        scratch_shapes=[pltpu.VMEM((dm // _LANES, tm, _LANES), _F32), pltpu.VMEM((tm, dm), _BF16)],
        compiler_params=_params("parallel", "parallel"), name=f"qkv_d{d}")(
            x, norm_g.reshape(1, dm), w_g_bf16)


def _group_attention(qkv, bias_tiles):
    b, d, lc, _ = qkv.shape
    nt = lc // _TQ

    def spec(which, shift):
        def imap(bi, r, i):
            blk = i + shift
            if shift < 0:
                blk = jnp.maximum(blk, 0)
            elif shift > 0:
                blk = jnp.minimum(blk, nt - 1)
            return (bi, r, blk, which)
        return pl.BlockSpec((None, None, _TQ, D_MODEL), imap)

    return pl.pallas_call(
        _attn_kernel, grid=(b, d, nt),
        in_specs=[spec(0, 0), spec(1, -1), spec(1, 0), spec(1, 1), spec(2, -1), spec(2, 0),
                  spec(2, 1), _const_spec(bias_tiles.shape)],
        out_specs=[pl.BlockSpec((None, None, _TQ, D_MODEL), lambda bi, r, i: (bi, r, i, 0)),
                   pl.BlockSpec((None, None, _TQ, _LANES), lambda bi, r, i: (bi, r, i, 0))],
        out_shape=[jax.ShapeDtypeStruct((b, d, lc, D_MODEL), _BF16),
                   jax.ShapeDtypeStruct((b, d, lc, _LANES), _F32)],
        compiler_params=_params("parallel", "parallel", "arbitrary"), name=f"attn_d{d}")(
            qkv, qkv, qkv, qkv, qkv, qkv, qkv, bias_tiles)


def _merge_kernel(o0_ref, o1_ref, o2_ref, l0_ref, l1_ref, l2_ref, x_ref, e_ref, wo_ref, out_ref,
                  of_ref, lf_ref, *, tm):
    def natural(o_ref, l_ref, slot):
        d = o_ref.shape[0]
        if d == 1:
            return o_ref[0].astype(_F32), l_ref[0]
        rows = tm // d
        nblk = D_MODEL // _LANES
        for r in range(d):
            o_r = o_ref[r].astype(_F32)
            for c in range(nblk):
                of_ref[slot, c, pl.ds(r, rows, stride=d), :] = o_r[:, c * _LANES:(c + 1) * _LANES]
            lf_ref[slot, pl.ds(r, rows, stride=d), :] = l_ref[r]
        return jnp.concatenate([of_ref[slot, c] for c in range(nblk)], axis=1), lf_ref[slot]

    o0, l0 = natural(o0_ref, l0_ref, 0)
    o1, l1 = natural(o1_ref, l1_ref, 0)
    o2, l2 = natural(o2_ref, l2_ref, 1)
    m = jnp.maximum(jnp.maximum(l0, l1), l2)
    a0, a1, a2 = jnp.exp(l0 - m), jnp.exp(l1 - m), jnp.exp(l2 - m)
    inv = 1.0 / (a0 + a1 + a2)

    def expand(a):
        al = a * inv
        hi = al.astype(_BF16)
        lo = (al - hi.astype(_F32)).astype(_BF16)
        return jnp.dot(jnp.concatenate([hi, lo], axis=1), e_ref[...], preferred_element_type=_F32)

    merged = expand(a0) * o0 + expand(a1) * o1 + expand(a2) * o2
    out_ref[...] = x_ref[...] + jnp.dot(merged.astype(_BF16), wo_ref[...],
                                        preferred_element_type=_F32)


def _head_expand_matrix():
    e = np.zeros((2 * _LANES, D_MODEL), np.float32)
    for h in range(HEADS_PER_GROUP):
        e[h, h * HEAD_DIM:(h + 1) * HEAD_DIM] = 1.0
        e[_LANES + h, h * HEAD_DIM:(h + 1) * HEAD_DIM] = 1.0
    return jnp.asarray(e, _BF16)


def _merge(os_, lses, x, wo_bf16, tm=512):
    b, s, dm = x.shape
    e = _head_expand_matrix()

    def cls(a):
        d, w = a.shape[1], a.shape[3]
        return pl.BlockSpec((None, d, tm // d, w), lambda bi, i: (bi, 0, i, 0))

    row = pl.BlockSpec((None, tm, dm), lambda bi, i: (bi, i, 0))
    return pl.pallas_call(
        functools.partial(_merge_kernel, tm=tm), grid=(b, s // tm),
        in_specs=[cls(a) for a in os_] + [cls(a) for a in lses]
        + [row, _const_spec(e.shape), _const_spec(wo_bf16.shape)],
        out_specs=row, out_shape=jax.ShapeDtypeStruct((b, s, dm), _F32),
        scratch_shapes=[pltpu.VMEM((2, dm // _LANES, tm, _LANES), _F32),
                        pltpu.VMEM((2, tm, _LANES), _F32)],
        compiler_params=_params("parallel", "parallel"), name="attn_merge")(
            *os_, *lses, x, e, wo_bf16)


def _dilated_attention_layer(x, norm_g, w_qkv, w_o, rel_bias):
    n3 = 3 * D_MODEL
    os_, lses = [], []
    for g, (window, dil) in enumerate(DILATED_GROUPS):
        assert (window // 2) // dil == _HALF_W
        w_g = w_qkv[:, g * n3:(g + 1) * n3].astype(_BF16)
        qkv = _qkv_group(x, norm_g, w_g, dil)
        bias = _attn_bias_tiles(rel_bias[:, g * HEADS_PER_GROUP:(g + 1) * HEADS_PER_GROUP], dil)
        o, lse = _group_attention(qkv, bias)
        os_.append(o)
        lses.append(lse)
    return _merge(os_, lses, x, w_o.astype(_BF16))


def _hyena_in_kernel(xp_ref, x_ref, xn_ref, g_ref, w_ref, cw_ref, cb_ref,
                     x1_ref, x2_ref, v_ref, *, tm):
    i = pl.program_id(1)
    last = pl.num_programs(1) - 1
    keep_prev = jnp.where(i > 0, 1.0, 0.0)
    keep_next = jnp.where(i < last, 1.0, 0.0)
    xe = jnp.concatenate([xp_ref[...] * keep_prev, x_ref[...], xn_ref[...] * keep_next], axis=0)
    xb = _rms(xe, g_ref[...]).astype(_BF16)
    n = tm + 16
    for j, o_ref in enumerate((x1_ref, x2_ref, v_ref)):
        cs = slice(j * D_MODEL, (j + 1) * D_MODEL)
        u = jnp.dot(xb, w_ref[:, cs], preferred_element_type=_F32)
        u_prev = pltpu.roll(u, 1, 0)[8:tm + 8]
        u_next = pltpu.roll(u, n - 1, 0)[8:tm + 8]
        cw = cw_ref[:, cs]
        y = (u_prev * cw[0:1] + u[8:tm + 8] * cw[1:2] + u_next * cw[2:3]) + cb_ref[:, cs]
        o_ref[...] = y.astype(o_ref.dtype)


def _hyena_in(x, norm_g, w_in_bf16, conv_w, conv_b, tm=512):
    b, s, d = x.shape
    n3 = w_in_bf16.shape[1]
    nblk8 = s // 8
    r8 = tm // 8
    main = pl.BlockSpec((None, tm, d), lambda bi, i: (bi, i, 0))
    prev = pl.BlockSpec((None, 8, d), lambda bi, i: (bi, jnp.maximum(i * r8 - 1, 0), 0))
    nxt = pl.BlockSpec((None, 8, d), lambda bi, i: (bi, jnp.minimum((i + 1) * r8, nblk8 - 1), 0))
    out = jax.ShapeDtypeStruct((b, s, d), _BF16)
    return pl.pallas_call(
        functools.partial(_hyena_in_kernel, tm=tm), grid=(b, s // tm),
        in_specs=[prev, main, nxt, _const_spec((1, d)), _const_spec(w_in_bf16.shape),
                  _const_spec((3, n3)), _const_spec((1, n3))],
        out_specs=[main, main, main], out_shape=[out, out, out],
        compiler_params=_params("parallel", "arbitrary"), name="hyena_in")(
            x, x, x, norm_g.reshape(1, d), w_in_bf16, conv_w.astype(_F32),
            conv_b.astype(_F32).reshape(1, n3))


def _filter_kernel(fr_ref, w1_ref, b1_ref, w2_ref, b2_ref, w3_ref, b3_ref, fq_ref, w4h_ref, w4l_ref,
                   dl_ref, k0_ref, k1_ref, *, seq_len, tr):
    hi = lax.Precision.HIGHEST
    r = pl.program_id(0) * tr + lax.broadcasted_iota(jnp.int32, (tr, 1), 0)
    pos = jnp.abs(r - seq_len).astype(_F32)
    t = pos / (seq_len - 1.0)
    wpos = (2.0 * math.pi) * pos / seq_len
    lane = lax.broadcasted_iota(jnp.int32, (1, _LANES), 1)
    bands = (HYENA_EMB_DIM - 1) // 2
    ang = fr_ref[...] * wpos
    z = jnp.where(lane == 0, t,
                  jnp.where(lane <= bands, jnp.cos(ang),
                            jnp.where(lane <= 2 * bands, -jnp.sin(ang), 0.0)))
    fq = fq_ref[...]
    h = jnp.sin(fq * (jnp.dot(z, w1_ref[...], precision=hi, preferred_element_type=_F32) + b1_ref[...]))
    h = jnp.sin(fq * (jnp.dot(h, w2_ref[...], precision=hi, preferred_element_type=_F32) + b2_ref[...]))
    h = jnp.sin(fq * (jnp.dot(h, w3_ref[...], precision=hi, preferred_element_type=_F32) + b3_ref[...]))
    decay = jnp.exp(-t * jnp.abs(dl_ref[...]))
    live = r > 0
    h_hi = h.astype(_BF16)
    h_lo = (h - h_hi.astype(_F32)).astype(_BF16)
    for o, k_ref in enumerate((k0_ref, k1_ref)):
        cs = slice(o * D_MODEL, (o + 1) * D_MODEL)
        w_hi = w4h_ref[:, cs]
        y = jnp.dot(h_hi, w_hi, preferred_element_type=_F32)
        y = y + jnp.dot(h_hi, w4l_ref[:, cs], preferred_element_type=_F32)
        y = y + jnp.dot(h_lo, w_hi, preferred_element_type=_F32)
        k_ref[...] = jnp.where(live, y * decay, 0.0)


def _hyena_two_sided_filters(seq_len, w1, b1, w2, b2, w3, b3, freq, w4, tr=512):
    fw = HYENA_FILTER_WIDTH
    n_tiles = 2 * seq_len // tr
    w4d = w4.astype(_F32).reshape(fw, 2, 2, D_MODEL).transpose(2, 0, 1, 3).reshape(2, fw, 2 * D_MODEL)
    w4_hi = w4d.astype(_BF16)
    w4_lo = (w4d - w4_hi.astype(_F32)).astype(_BF16)
    w4_spec = pl.BlockSpec((None, fw, 2 * D_MODEL),
                           lambda i: (jnp.where(i >= n_tiles // 2, 0, 1), 0, 0))
    bands = (HYENA_EMB_DIM - 1) // 2
    fr = jnp.linspace(1e-4, bands - 1, bands, dtype=_F32)
    fr_lanes = jnp.zeros((1, _LANES), _F32).at[0, 1:1 + bands].set(fr).at[0, 1 + bands:1 + 2 * bands].set(fr)
    w1p = jnp.zeros((_LANES, fw), _F32).at[:HYENA_EMB_DIM].set(w1.astype(_F32))
    max_decay = math.log(DECAY_TARGET) / FAST_DECAY_PCT
    min_decay = math.log(DECAY_TARGET) / SLOW_DECAY_PCT
    deltas = jnp.linspace(min_decay, max_decay, D_MODEL, dtype=_F32).reshape(1, D_MODEL)
    vec = lambda a: a.astype(_F32).reshape(1, fw)
    out = jax.ShapeDtypeStruct((2 * seq_len, D_MODEL), _F32)
    cs = _const_spec
    return pl.pallas_call(
        functools.partial(_filter_kernel, seq_len=seq_len, tr=tr), grid=(n_tiles,),
        in_specs=[cs((1, _LANES)), cs((_LANES, fw)), cs((1, fw)), cs((fw, fw)), cs((1, fw)),
                  cs((fw, fw)), cs((1, fw)), cs((1, fw)), w4_spec, w4_spec, cs((1, D_MODEL))],
        out_specs=[pl.BlockSpec((tr, D_MODEL), lambda i: (i, 0))] * 2, out_shape=[out, out],
        compiler_params=_params("parallel"), name="hyena_filter")(
            fr_lanes, w1p, vec(b1), w2.astype(_F32), vec(b2), w3.astype(_F32), vec(b3), vec(freq),
            w4_hi, w4_lo, deltas)


def _dft_matrices(p):
    n = 2 * p
    k = np.arange(p)[:, None].astype(np.float64)
    s = np.arange(n)[None, :].astype(np.float64)
    ang = np.pi * k * s / p
    fre = np.cos(ang)
    fim = -np.sin(ang)
    fim[0, :] = np.cos(np.pi * s[0])
    fwd_full = np.concatenate([fre, fim], axis=0)
    tt = (p + np.arange(p))[:, None].astype(np.float64)
    kk = np.arange(p)[None, :].astype(np.float64)
    ang2 = np.pi * kk * tt / p
    are = 2.0 * np.cos(ang2) / n
    are[:, 0] = 1.0 / n
    aim = -2.0 * np.sin(ang2) / n
    aim[:, 0] = np.cos(np.pi * tt[:, 0]) / n
    inv = np.concatenate([are, aim], axis=1)
    return (jnp.asarray(fwd_full[:, :p], _BF16), jnp.asarray(fwd_full, _BF16), jnp.asarray(inv, _BF16))


def _filter_spectrum_kernel(ga_ref, gb_ref, f_ref, o_ref, *, p):
    ga = ga_ref[...].astype(_BF16)
    gb = gb_ref[...].astype(_BF16)
    acc = jnp.dot(f_ref[:, :p], ga, preferred_element_type=_F32)
    acc = acc + jnp.dot(f_ref[:, p:], gb, preferred_element_type=_F32)
    o_ref[...] = acc.astype(o_ref.dtype)


def _filter_spectrum(kt, fwd_full, p, tc=256):
    rows, d = kt.shape
    nseg = rows // p - 1
    return pl.pallas_call(
        functools.partial(_filter_spectrum_kernel, p=p), grid=(nseg, d // tc),
        in_specs=[pl.BlockSpec((p, tc), lambda q, c: (q, c)),
                  pl.BlockSpec((p, tc), lambda q, c: (q + 1, c)), _const_spec(fwd_full.shape)],
        out_specs=pl.BlockSpec((None, 2 * p, tc), lambda q, c: (q, 0, c)),
        out_shape=jax.ShapeDtypeStruct((nseg, 2 * p, d), _F32),
        compiler_params=_params("parallel", "parallel"), name="hyena_filter_spectrum")(kt, kt, fwd_full)


def _long_conv_kernel(gate_ref, z_ref, kf_ref, skip_ref, ff_ref, ai_ref, o_ref, zf_ref, yfa_ref,
                      yfb_ref, *, p, nb, tc):
    ch = _CONV_ROWS

    def fwd_body(i, carry):
        zb = z_ref[pl.ds(pl.multiple_of(i * p, p), p), :]
        zf_ref[i] = jnp.dot(ff_ref[...], zb, preferred_element_type=_F32)
        return carry

    lax.fori_loop(0, nb, fwd_body, 0)
    skip = skip_ref[...]
    first_row = lax.broadcasted_iota(jnp.int32, (ch, 1), 0) == 0

    def spectrum(j, yf_ref):
        for rc in range(p // ch):
            re = slice(rc * ch, (rc + 1) * ch)
            im = slice(p + rc * ch, p + (rc + 1) * ch)
            a = jnp.zeros((ch, tc), _F32)
            bm = jnp.zeros((ch, tc), _F32)
            cc = jnp.zeros((ch, tc), _F32)
            for i in range(nb):
                q = j - i + (nb - 1)
                zre, zim = zf_ref[i, re, :], zf_ref[i, im, :]
                kre, kim = kf_ref[q, re, :], kf_ref[q, im, :]
                a = a + zre * kre
                bm = bm + zim * kim
                cc = cc + (zre * kim + zim * kre)
            if rc == 0:
                yf_ref[re, :] = jnp.where(first_row, a, a - bm).astype(_BF16)
                yf_ref[im, :] = jnp.where(first_row, bm, cc).astype(_BF16)
            else:
                yf_ref[re, :] = (a - bm).astype(_BF16)
                yf_ref[im, :] = cc.astype(_BF16)

    def finish(j, yf_ref):
        y = jnp.dot(ai_ref[...], yf_ref[...], preferred_element_type=_F32)
        rows = pl.ds(pl.multiple_of(j * p, p), p)
        zt = z_ref[rows, :].astype(_F32)
        gt = gate_ref[rows, :].astype(_F32)
        o_ref[rows, :] = (gt * (y + zt * skip)).astype(o_ref.dtype)

    spectrum(0, yfa_ref)
    if nb > 1:
        def pair_body(k, carry):
            finish(2 * k, yfa_ref)
            spectrum(2 * k + 1, yfb_ref)
            finish(2 * k + 1, yfb_ref)
            spectrum(2 * k + 2, yfa_ref)
            return carry

        lax.fori_loop(0, nb // 2 - 1, pair_body, 0)
        finish(nb - 2, yfa_ref)
        spectrum(nb - 1, yfb_ref)
        finish(nb - 1, yfb_ref)
    else:
        finish(0, yfa_ref)


def _long_conv(gate, z, kf, skip_o, fwd_half, inv, p):
    b, l, d = z.shape
    nb = l // p
    assert nb == 1 or nb % 2 == 0
    spectra_bytes_per_lane = (3 * nb - 1) * 2 * p * 4
    tc = 256 if 256 * spectra_bytes_per_lane <= _V7X_VMEM_LIMIT_BYTES // 2 else 128
    seq = pl.BlockSpec((None, l, tc), lambda c, bi: (bi, 0, c))
    kf_spec = pl.BlockSpec((2 * nb - 1, 2 * p, tc), lambda c, bi: (0, 0, c),
                           pipeline_mode=pl.Buffered(1))
    return pl.pallas_call(
        functools.partial(_long_conv_kernel, p=p, nb=nb, tc=tc), grid=(d // tc, b),
        in_specs=[seq, seq, kf_spec, pl.BlockSpec((1, tc), lambda c, bi: (0, c)),
                  _const_spec(fwd_half.shape), _const_spec(inv.shape)],
        out_specs=seq, out_shape=jax.ShapeDtypeStruct((b, l, d), _BF16),
        scratch_shapes=[pltpu.VMEM((nb, 2 * p, tc), _F32), pltpu.VMEM((2 * p, tc), _BF16),
                        pltpu.VMEM((2 * p, tc), _BF16)],
        compiler_params=_params("parallel", "arbitrary"), name="hyena_long_conv")(
            gate, z, kf, skip_o.astype(_F32).reshape(1, d), fwd_half, inv)


def _hyena_layer(x, norm_g, w_in, conv_w, conv_b, w1, b1, w2, b2, w3, b3, freq, w4, skip, w_out):
    b, l, d = x.shape
    p = min(l, max(_CONV_P, l // 8))
    x1, x2, v = _hyena_in(x, norm_g, w_in.astype(_BF16), conv_w, conv_b)
    k0, k1 = _hyena_two_sided_filters(l, w1, b1, w2, b2, w3, b3, freq, w4)
    fwd_half, fwd_full, inv = _dft_matrices(p)
    z = v
    for o, (gate, kt) in enumerate(((x1, k0), (x2, k1))):
        kf = _filter_spectrum(kt, fwd_full, p)
        z = _long_conv(gate, z, kf, skip[o], fwd_half, inv, p)
    y = _matmul_residual(z.reshape(b * l, d), w_out.astype(_BF16), x.reshape(b * l, d))
    return y.reshape(b, l, d)


def _trunk(x, rel_bias, ffn1_norm, ffn1_w, mixer_norm, attn_w_qkv, attn_w_o, hy, ffn2_norm, ffn2_w,
           final_norm):
    b, s, d = x.shape
    depth = ffn1_norm.shape[0]
    flat = lambda a: a.reshape(b * s, d)
    for i in range(depth):
        x = _ffn(flat(x), ffn1_norm[i], *ffn1_w[i]).reshape(b, s, d)
        j = i // 2
        if i % 2 == 0:
            x = _dilated_attention_layer(x, mixer_norm[i], attn_w_qkv[j], attn_w_o[j], rel_bias)
        else:
            x = _hyena_layer(x, mixer_norm[i], *[a[j] for a in hy])
        fin = final_norm if i == depth - 1 else None
        x = _ffn(flat(x), ffn2_norm[i], *ffn2_w[i], final_g=fin).reshape(b, s, d)
    return x


def kernel(x_prompt, x_sample, rel_bias, ffn1_norm, ffn1_w_gate_up, ffn1_w_down, mixer_norm, attn_w_qkv, attn_w_o, hyena_w_in, hyena_conv_w, hyena_conv_b, hyena_filt_w1, hyena_filt_b1, hyena_filt_w2, hyena_filt_b2, hyena_filt_w3, hyena_filt_b3, hyena_filt_freq, hyena_filt_w4, hyena_skip, hyena_w_out, ffn2_norm, ffn2_w_gate_up, ffn2_w_down, final_norm):
    depth = ffn1_norm.shape[0]
    ffn1_w = [_ffn_weights(ffn1_w_gate_up[i], ffn1_w_down[i]) for i in range(depth)]
    ffn2_w = [_ffn_weights(ffn2_w_gate_up[i], ffn2_w_down[i]) for i in range(depth)]
    hy = (hyena_w_in, hyena_conv_w, hyena_conv_b, hyena_filt_w1, hyena_filt_b1, hyena_filt_w2,
          hyena_filt_b2, hyena_filt_w3, hyena_filt_b3, hyena_filt_freq, hyena_filt_w4, hyena_skip,
          hyena_w_out)
    args = (rel_bias, ffn1_norm, ffn1_w, mixer_norm, attn_w_qkv, attn_w_o, hy, ffn2_norm, ffn2_w,
            final_norm)
    return (_trunk(x_prompt, *args), _trunk(x_sample, *args))
```

```python
import functools
import math

import numpy as np
import jax
import jax.numpy as jnp
from jax import lax
from jax.experimental import pallas as pl
from jax.experimental.pallas import tpu as pltpu

D_MODEL = 1024
HEAD_DIM = 64
HEADS_PER_GROUP = D_MODEL // HEAD_DIM
DILATED_GROUPS = ((128, 1), (512, 4), (2048, 16))
N_GROUPS = len(DILATED_GROUPS)
NUM_BUCKETS = 32
MAX_DISTANCE = 1024
HYENA_EMB_DIM = 33
HYENA_FILTER_WIDTH = 64
FAST_DECAY_PCT = 0.3
SLOW_DECAY_PCT = 1.5
DECAY_TARGET = 1e-2
D_FF = 2816
RMS_EPS = 1e-6

_F32 = jnp.float32
_BF16 = jnp.bfloat16
_V7X_VMEM_LIMIT_BYTES = 56 * 1024 * 1024
_LANES = 128
_NEG = -1e30
_LOG2E = math.log2(math.e)

_TQ = 128
_HALF_W = 64
_TK = _TQ + 2 * _HALF_W
_ATTN_MAX_SUBTILES = 4
_MXU_DIM = 256
_CONV_P = 512
_CONV_ROWS = 32


def _params(*sem):
    return pltpu.CompilerParams(dimension_semantics=sem, vmem_limit_bytes=_V7X_VMEM_LIMIT_BYTES)


def _const_spec(shape):
    nd = len(shape)
    return pl.BlockSpec(shape, lambda *_: (0,) * nd, pipeline_mode=pl.Buffered(1))


def _rms(x, g):
    y = x * lax.rsqrt(jnp.mean(x * x, axis=-1, keepdims=True) + RMS_EPS)
    return y * g


def _ffn_kernel(*refs, final):
    if final:
        x_ref, g_ref, wgu_ref, wd_ref, gf_ref, o_ref = refs
    else:
        x_ref, g_ref, wgu_ref, wd_ref, o_ref = refs
    f = wd_ref.shape[0]
    n_tiles = f // _MXU_DIM
    bounds = [0, (n_tiles + 1) // 2 * _MXU_DIM, f]
    x = x_ref[...]
    xb = _rms(x, g_ref[...]).astype(_BF16)
    acc = jnp.zeros(x.shape, _F32)
    for lo, hi in zip(bounds[:-1], bounds[1:]):
        gate = jnp.dot(xb, wgu_ref[:, lo:hi], preferred_element_type=_F32)
        up = jnp.dot(xb, wgu_ref[:, f + lo:f + hi], preferred_element_type=_F32)
        act = (gate * jax.nn.sigmoid(gate)) * up
        acc = acc + jnp.dot(act.astype(_BF16), wd_ref[lo:hi, :], preferred_element_type=_F32)
    y = x + 0.5 * acc
    if final:
        y = _rms(y, gf_ref[...])
    o_ref[...] = y


def _ffn(x2d, norm_g, wgu, wd, final_g=None, tm=512):
    t, d = x2d.shape
    final = final_g is not None
    in_specs = [pl.BlockSpec((tm, d), lambda i: (i, 0)), _const_spec((1, d)),
                _const_spec(wgu.shape), _const_spec(wd.shape)]
    args = [x2d, norm_g.reshape(1, d), wgu, wd]
    if final:
        in_specs.append(_const_spec((1, d)))
        args.append(final_g.reshape(1, d))
    return pl.pallas_call(
        functools.partial(_ffn_kernel, final=final),
        grid=(t // tm,), in_specs=in_specs,
        out_specs=pl.BlockSpec((tm, d), lambda i: (i, 0)),
        out_shape=jax.ShapeDtypeStruct((t, d), _F32),
        compiler_params=_params("parallel"), name="ffn")(*args)


def _ffn_weights(w_gate_up, w_down):
    return w_gate_up.astype(_BF16), w_down.astype(_BF16)


def _matmul_residual_kernel(a_ref, w_ref, x_ref, o_ref):
    o_ref[...] = x_ref[...] + jnp.dot(a_ref[...], w_ref[...], preferred_element_type=_F32)


def _matmul_residual(a2d, w_bf16, x2d, tm=1024):
    t, d = x2d.shape
    k = a2d.shape[1]
    return pl.pallas_call(
        _matmul_residual_kernel, grid=(t // tm,),
        in_specs=[pl.BlockSpec((tm, k), lambda i: (i, 0)), _const_spec(w_bf16.shape),
                  pl.BlockSpec((tm, d), lambda i: (i, 0))],
        out_specs=pl.BlockSpec((tm, d), lambda i: (i, 0)),
        out_shape=jax.ShapeDtypeStruct((t, d), _F32),
        compiler_params=_params("parallel"), name="matmul_residual")(a2d, w_bf16, x2d)


def _t5_bucket(rel):
    nb = NUM_BUCKETS // 2
    max_exact = nb // 2
    ret = (rel > 0).astype(np.int32) * nb
    n = np.abs(rel)
    large = max_exact + (np.log(np.maximum(n, 1) / max_exact)
                         / math.log(MAX_DISTANCE / max_exact) * (nb - max_exact)).astype(np.int32)
    large = np.minimum(large, nb - 1)
    return ret + np.where(n < max_exact, n, large)


def _attn_bucket_tables(dilation):
    delta = np.arange(_TK)[None, :] - _HALF_W - np.arange(_TQ)[:, None]
    bucket = _t5_bucket(delta * dilation)
    band = np.abs(delta) <= _HALF_W
    kk = np.arange(_TK)[None, :]
    tables = []
    for variant in range(4):
        ok = band
        if variant & 1:
            ok = ok & (kk >= _HALF_W)
        if variant & 2:
            ok = ok & (kk < _TQ + _HALF_W)
        tables.append(np.where(ok, bucket, -1))
    return jnp.asarray(np.stack(tables), jnp.int32)


def _bias_tiles_kernel(rb_ref, bucket_ref, o_ref):
    hp = pl.program_id(1)
    bucket = bucket_ref[...]
    for hh in range(2):
        acc = jnp.full((_TQ, _TK), _NEG, _F32)
        for bkt in range(NUM_BUCKETS):
            acc = jnp.where(bucket == bkt, rb_ref[2 * hp + hh, bkt] * _LOG2E, acc)
        o_ref[hh * _TQ:(hh + 1) * _TQ, :] = acc


def _attn_bias_tiles(rel_bias_g, dilation):
    tables = _attn_bucket_tables(dilation)
    npair = HEADS_PER_GROUP // 2
    return pl.pallas_call(
        _bias_tiles_kernel, grid=(4, npair),
        in_specs=[pl.BlockSpec(memory_space=pltpu.SMEM),
                  pl.BlockSpec((None, _TQ, _TK), lambda v, hp: (v, 0, 0))],
        out_specs=pl.BlockSpec((None, None, 2 * _TQ, _TK), lambda v, hp: (v, hp, 0, 0)),
        out_shape=jax.ShapeDtypeStruct((4, npair, 2 * _TQ, _TK), _F32),
        compiler_params=_params("parallel", "parallel"), name="attn_bias_tiles")(
            rel_bias_g.astype(_F32).T, tables)


def _attn_kernel(q_ref, kp_ref, kc_ref, kn_ref, vp_ref, vc_ref, vn_ref, bias_ref,
                 o_ref, lse_ref, *, nr, nsub):
    i = pl.program_id(2)
    is_first = jnp.where(i == 0, 1, 0)
    is_last = jnp.where(i == pl.num_programs(2) - 1, 2, 0)
    lane = lax.broadcasted_iota(jnp.int32, (1, _LANES), 1)
    low = lane < HEAD_DIM
    zero = jnp.zeros((), _BF16)
    cur_rows = nsub * _TQ

    def window(prev_ref, cur_ref, next_ref, ri, sub, sl):
        start, end = sub * _TQ - _HALF_W, (sub + 1) * _TQ + _HALF_W
        pieces = []
        if start < 0:
            pieces.append(prev_ref[ri, _TQ - _HALF_W:, sl])
        pieces.append(cur_ref[ri, max(start, 0):min(end, cur_rows), sl])
        if end > cur_rows:
            pieces.append(next_ref[ri, :_HALF_W, sl])
        return jnp.concatenate(pieces, axis=0)

    for ri in range(nr):
        for sub in range(nsub):
            rows = slice(sub * _TQ, (sub + 1) * _TQ)
            variant = (is_first if sub == 0 else 0) + (is_last if sub == nsub - 1 else 0)
            m_tile = jnp.zeros((_TQ, _LANES), _F32)
            l_tile = jnp.ones((_TQ, _LANES), _F32)
            for hp in range(HEADS_PER_GROUP // 2):
                sl = slice(hp * _LANES, (hp + 1) * _LANES)
                q2 = q_ref[ri, rows, sl]
                qq = jnp.concatenate([jnp.where(low, q2, zero), jnp.where(low, zero, q2)], axis=0)
                k2 = window(kp_ref, kc_ref, kn_ref, ri, sub, sl)
                v2 = window(vp_ref, vc_ref, vn_ref, ri, sub, sl)
                s = lax.dot_general(qq, k2, (((1,), (1,)), ((), ())),
                                    preferred_element_type=_F32)
                s = s + bias_ref[variant, hp]
                m = jnp.max(s, axis=-1, keepdims=True)
                p = jnp.exp2(s - m)
                l = jnp.sum(p, axis=-1, keepdims=True)
                pv = jnp.dot(p.astype(_BF16), v2, preferred_element_type=_F32)
                pv = pv / l
                o_ref[ri, rows, sl] = jnp.where(low, pv[:_TQ], pv[_TQ:]).astype(o_ref.dtype)
                m_tile = jnp.where(lane == 2 * hp, m[:_TQ],
                                   jnp.where(lane == 2 * hp + 1, m[_TQ:], m_tile))
                l_tile = jnp.where(lane == 2 * hp, l[:_TQ],
                                   jnp.where(lane == 2 * hp + 1, l[_TQ:], l_tile))
            lse_ref[ri, rows, :] = (m_tile + jnp.log2(l_tile)) * math.log(2.0)


def _qkv_group_kernel(x_ref, g_ref, w_ref, o_ref, xn_ref, xp_ref, *, d, tm):
    rows = tm // d
    xn = _rms(x_ref[...], g_ref[...])
    if d == 1:
        xp_ref[...] = xn.astype(_BF16)
    else:
        for c in range(D_MODEL // _LANES):
            xn_ref[c] = xn[:, c * _LANES:(c + 1) * _LANES]
        for r in range(d):
            for c in range(D_MODEL // _LANES):
                xp_ref[r * rows:(r + 1) * rows, c * _LANES:(c + 1) * _LANES] = (
                    xn_ref[c, pl.ds(r, rows, stride=d), :].astype(_BF16))
    for j in range(3):
        cs = slice(j * D_MODEL, (j + 1) * D_MODEL)
        y = jnp.dot(xp_ref[...], w_ref[:, cs], preferred_element_type=_F32).astype(o_ref.dtype)
        for r in range(d):
            o_ref[r, :, cs] = y[r * rows:(r + 1) * rows]


def _qkv_group(x, norm_g, w_g_bf16, dilation, tm=512):
    b, s, dm = x.shape
    d = dilation
    n3 = w_g_bf16.shape[1]
    return pl.pallas_call(
        functools.partial(_qkv_group_kernel, d=d, tm=tm), grid=(b, s // tm),
        in_specs=[pl.BlockSpec((None, tm, dm), lambda bi, i: (bi, i, 0)), _const_spec((1, dm)),
                  _const_spec(w_g_bf16.shape)],
        out_specs=pl.BlockSpec((None, d, tm // d, n3), lambda bi, i: (bi, 0, i, 0)),
        out_shape=jax.ShapeDtypeStruct((b, d, s // d, n3), _BF16),
        scratch_shapes=[pltpu.VMEM((dm // _LANES, tm, _LANES), _F32), pltpu.VMEM((tm, dm), _BF16)],
        compiler_params=_params("parallel", "parallel"), name=f"qkv_d{d}")(
            x, norm_g.reshape(1, dm), w_g_bf16)


def _group_attention(qkv, bias_tiles):
    b, d, lc, _ = qkv.shape
    n_tq = lc // _TQ
    nsub = min(_ATTN_MAX_SUBTILES, n_tq)
    nr = min(d, _ATTN_MAX_SUBTILES // nsub)
    rows = nsub * _TQ

    def cur(which):
        return pl.BlockSpec((None, nr, rows, D_MODEL), lambda bi, r, i: (bi, r, i, which))

    def prev(which):
        return pl.BlockSpec((None, nr, _TQ, D_MODEL),
                            lambda bi, r, i: (bi, r, jnp.maximum(i * nsub - 1, 0), which))

    def nxt(which):
        return pl.BlockSpec((None, nr, _TQ, D_MODEL),
                            lambda bi, r, i: (bi, r, jnp.minimum((i + 1) * nsub, n_tq - 1), which))

    return pl.pallas_call(
        functools.partial(_attn_kernel, nr=nr, nsub=nsub), grid=(b, d // nr, n_tq // nsub),
        in_specs=[cur(0), prev(1), cur(1), nxt(1), prev(2), cur(2), nxt(2),
                  _const_spec(bias_tiles.shape)],
        out_specs=[pl.BlockSpec((None, nr, rows, D_MODEL), lambda bi, r, i: (bi, r, i, 0)),
                   pl.BlockSpec((None, nr, rows, _LANES), lambda bi, r, i: (bi, r, i, 0))],
        out_shape=[jax.ShapeDtypeStruct((b, d, lc, D_MODEL), _BF16),
                   jax.ShapeDtypeStruct((b, d, lc, _LANES), _F32)],
        compiler_params=_params("parallel", "parallel", "arbitrary"), name=f"attn_d{d}")(
            qkv, qkv, qkv, qkv, qkv, qkv, qkv, bias_tiles)


def _merge_kernel(o0_ref, o1_ref, o2_ref, l0_ref, l1_ref, l2_ref, x_ref, e_ref, wo_ref, out_ref,
                  of_ref, lf_ref, *, tm):
    def natural(o_ref, l_ref, slot):
        d = o_ref.shape[0]
        if d == 1:
            return o_ref[0].astype(_F32), l_ref[0]
        rows = tm // d
        nblk = D_MODEL // _LANES
        for r in range(d):
            o_r = o_ref[r].astype(_F32)
            for c in range(nblk):
                of_ref[slot, c, pl.ds(r, rows, stride=d), :] = o_r[:, c * _LANES:(c + 1) * _LANES]
            lf_ref[slot, pl.ds(r, rows, stride=d), :] = l_ref[r]
        return jnp.concatenate([of_ref[slot, c] for c in range(nblk)], axis=1), lf_ref[slot]

    o0, l0 = natural(o0_ref, l0_ref, 0)
    o1, l1 = natural(o1_ref, l1_ref, 0)
    o2, l2 = natural(o2_ref, l2_ref, 1)
    m = jnp.maximum(jnp.maximum(l0, l1), l2)
    a0, a1, a2 = jnp.exp(l0 - m), jnp.exp(l1 - m), jnp.exp(l2 - m)
    inv = 1.0 / (a0 + a1 + a2)

    def expand(a):
        al = a * inv
        hi = al.astype(_BF16)
        lo = (al - hi.astype(_F32)).astype(_BF16)
        return jnp.dot(jnp.concatenate([hi, lo], axis=1), e_ref[...], preferred_element_type=_F32)

    merged = expand(a0) * o0 + expand(a1) * o1 + expand(a2) * o2
    out_ref[...] = x_ref[...] + jnp.dot(merged.astype(_BF16), wo_ref[...],
                                        preferred_element_type=_F32)


def _head_expand_matrix():
    e = np.zeros((2 * _LANES, D_MODEL), np.float32)
    for h in range(HEADS_PER_GROUP):
        e[h, h * HEAD_DIM:(h + 1) * HEAD_DIM] = 1.0
        e[_LANES + h, h * HEAD_DIM:(h + 1) * HEAD_DIM] = 1.0
    return jnp.asarray(e, _BF16)


def _merge(os_, lses, x, wo_bf16, tm=512):
    b, s, dm = x.shape
    e = _head_expand_matrix()

    def cls(a):
        d, w = a.shape[1], a.shape[3]
        return pl.BlockSpec((None, d, tm // d, w), lambda bi, i: (bi, 0, i, 0))

    row = pl.BlockSpec((None, tm, dm), lambda bi, i: (bi, i, 0))
    return pl.pallas_call(
        functools.partial(_merge_kernel, tm=tm), grid=(b, s // tm),
        in_specs=[cls(a) for a in os_] + [cls(a) for a in lses]
        + [row, _const_spec(e.shape), _const_spec(wo_bf16.shape)],
        out_specs=row, out_shape=jax.ShapeDtypeStruct((b, s, dm), _F32),
        scratch_shapes=[pltpu.VMEM((2, dm // _LANES, tm, _LANES), _F32),
                        pltpu.VMEM((2, tm, _LANES), _F32)],
        compiler_params=_params("parallel", "parallel"), name="attn_merge")(
            *os_, *lses, x, e, wo_bf16)


def _dilated_attention_layer(x, norm_g, w_qkv, w_o, rel_bias):
    n3 = 3 * D_MODEL
    os_, lses = [], []
    for g, (window, dil) in enumerate(DILATED_GROUPS):
        assert (window // 2) // dil == _HALF_W
        col_scale = jnp.where(jnp.arange(n3) < D_MODEL, _LOG2E * HEAD_DIM ** -0.5, 1.0).astype(_F32)
        w_g = (w_qkv[:, g * n3:(g + 1) * n3].astype(_F32) * col_scale).astype(_BF16)
        qkv = _qkv_group(x, norm_g, w_g, dil)
        bias = _attn_bias_tiles(rel_bias[:, g * HEADS_PER_GROUP:(g + 1) * HEADS_PER_GROUP], dil)
        o, lse = _group_attention(qkv, bias)
        os_.append(o)
        lses.append(lse)
    return _merge(os_, lses, x, w_o.astype(_BF16))


def _hyena_in_kernel(xp_ref, x_ref, xn_ref, g_ref, w_ref, cw_ref, cb_ref,
                     x1_ref, x2_ref, v_ref, *, tm):
    i = pl.program_id(1)
    last = pl.num_programs(1) - 1
    keep_prev = jnp.where(i > 0, 1.0, 0.0)
    keep_next = jnp.where(i < last, 1.0, 0.0)
    xe = jnp.concatenate([xp_ref[...] * keep_prev, x_ref[...], xn_ref[...] * keep_next], axis=0)
    xb = _rms(xe, g_ref[...]).astype(_BF16)
    n = tm + 16
    for j, o_ref in enumerate((x1_ref, x2_ref, v_ref)):
        cs = slice(j * D_MODEL, (j + 1) * D_MODEL)
        u = jnp.dot(xb, w_ref[:, cs], preferred_element_type=_F32)
        u_prev = pltpu.roll(u, 1, 0)[8:tm + 8]
        u_next = pltpu.roll(u, n - 1, 0)[8:tm + 8]
        cw = cw_ref[:, cs]
        y = (u_prev * cw[0:1] + u[8:tm + 8] * cw[1:2] + u_next * cw[2:3]) + cb_ref[:, cs]
        o_ref[...] = y.astype(o_ref.dtype)


def _hyena_in(x, norm_g, w_in_bf16, conv_w, conv_b, tm=512):
    b, s, d = x.shape
    n3 = w_in_bf16.shape[1]
    nblk8 = s // 8
    r8 = tm // 8
    main = pl.BlockSpec((None, tm, d), lambda bi, i: (bi, i, 0))
    prev = pl.BlockSpec((None, 8, d), lambda bi, i: (bi, jnp.maximum(i * r8 - 1, 0), 0))
    nxt = pl.BlockSpec((None, 8, d), lambda bi, i: (bi, jnp.minimum((i + 1) * r8, nblk8 - 1), 0))
    out = jax.ShapeDtypeStruct((b, s, d), _BF16)
    return pl.pallas_call(
        functools.partial(_hyena_in_kernel, tm=tm), grid=(b, s // tm),
        in_specs=[prev, main, nxt, _const_spec((1, d)), _const_spec(w_in_bf16.shape),
                  _const_spec((3, n3)), _const_spec((1, n3))],
        out_specs=[main, main, main], out_shape=[out, out, out],
        compiler_params=_params("parallel", "arbitrary"), name="hyena_in")(
            x, x, x, norm_g.reshape(1, d), w_in_bf16, conv_w.astype(_F32),
            conv_b.astype(_F32).reshape(1, n3))


def _filter_kernel(fr_ref, w1_ref, b1_ref, w2_ref, b2_ref, w3_ref, b3_ref, fq_ref, w4h_ref, w4l_ref,
                   dl_ref, k0_ref, k1_ref, *, seq_len, tr):
    hi = lax.Precision.HIGHEST
    r = pl.program_id(0) * tr + lax.broadcasted_iota(jnp.int32, (tr, 1), 0)
    pos = jnp.abs(r - seq_len).astype(_F32)
    t = pos / (seq_len - 1.0)
    wpos = (2.0 * math.pi) * pos / seq_len
    lane = lax.broadcasted_iota(jnp.int32, (1, _LANES), 1)
    bands = (HYENA_EMB_DIM - 1) // 2
    ang = fr_ref[...] * wpos
    z = jnp.where(lane == 0, t,
                  jnp.where(lane <= bands, jnp.cos(ang),
                            jnp.where(lane <= 2 * bands, -jnp.sin(ang), 0.0)))
    fq = fq_ref[...]
    h = jnp.sin(fq * (jnp.dot(z, w1_ref[...], precision=hi, preferred_element_type=_F32) + b1_ref[...]))
    h = jnp.sin(fq * (jnp.dot(h, w2_ref[...], precision=hi, preferred_element_type=_F32) + b2_ref[...]))
    h = jnp.sin(fq * (jnp.dot(h, w3_ref[...], precision=hi, preferred_element_type=_F32) + b3_ref[...]))
    decay = jnp.exp(-t * jnp.abs(dl_ref[...]))
    live = r > 0
    h_hi = h.astype(_BF16)
    h_lo = (h - h_hi.astype(_F32)).astype(_BF16)
    for o, k_ref in enumerate((k0_ref, k1_ref)):
        cs = slice(o * D_MODEL, (o + 1) * D_MODEL)
        w_hi = w4h_ref[:, cs]
        y = jnp.dot(h_hi, w_hi, preferred_element_type=_F32)
        y = y + jnp.dot(h_hi, w4l_ref[:, cs], preferred_element_type=_F32)
        y = y + jnp.dot(h_lo, w_hi, preferred_element_type=_F32)
        k_ref[...] = jnp.where(live, y * decay, 0.0)


def _hyena_two_sided_filters(seq_len, w1, b1, w2, b2, w3, b3, freq, w4, tr=512):
    fw = HYENA_FILTER_WIDTH
    n_tiles = 2 * seq_len // tr
    w4d = w4.astype(_F32).reshape(fw, 2, 2, D_MODEL).transpose(2, 0, 1, 3).reshape(2, fw, 2 * D_MODEL)
    w4_hi = w4d.astype(_BF16)
    w4_lo = (w4d - w4_hi.astype(_F32)).astype(_BF16)
    w4_spec = pl.BlockSpec((None, fw, 2 * D_MODEL),
                           lambda i: (jnp.where(i >= n_tiles // 2, 0, 1), 0, 0))
    bands = (HYENA_EMB_DIM - 1) // 2
    fr = jnp.linspace(1e-4, bands - 1, bands, dtype=_F32)
    fr_lanes = jnp.zeros((1, _LANES), _F32).at[0, 1:1 + bands].set(fr).at[0, 1 + bands:1 + 2 * bands].set(fr)
    w1p = jnp.zeros((_LANES, fw), _F32).at[:HYENA_EMB_DIM].set(w1.astype(_F32))
    max_decay = math.log(DECAY_TARGET) / FAST_DECAY_PCT
    min_decay = math.log(DECAY_TARGET) / SLOW_DECAY_PCT
    deltas = jnp.linspace(min_decay, max_decay, D_MODEL, dtype=_F32).reshape(1, D_MODEL)
    vec = lambda a: a.astype(_F32).reshape(1, fw)
    out = jax.ShapeDtypeStruct((2 * seq_len, D_MODEL), _F32)
    cs = _const_spec
    return pl.pallas_call(
        functools.partial(_filter_kernel, seq_len=seq_len, tr=tr), grid=(n_tiles,),
        in_specs=[cs((1, _LANES)), cs((_LANES, fw)), cs((1, fw)), cs((fw, fw)), cs((1, fw)),
                  cs((fw, fw)), cs((1, fw)), cs((1, fw)), w4_spec, w4_spec, cs((1, D_MODEL))],
        out_specs=[pl.BlockSpec((tr, D_MODEL), lambda i: (i, 0))] * 2, out_shape=[out, out],
        compiler_params=_params("parallel"), name="hyena_filter")(
            fr_lanes, w1p, vec(b1), w2.astype(_F32), vec(b2), w3.astype(_F32), vec(b3), vec(freq),
            w4_hi, w4_lo, deltas)


def _dft_matrices(p):
    n = 2 * p
    k = np.arange(p)[:, None].astype(np.float64)
    s = np.arange(n)[None, :].astype(np.float64)
    ang = np.pi * k * s / p
    fre = np.cos(ang)
    fim = -np.sin(ang)
    fim[0, :] = np.cos(np.pi * s[0])
    fwd_full = np.concatenate([fre, fim], axis=0)
    tt = (p + np.arange(p))[:, None].astype(np.float64)
    kk = np.arange(p)[None, :].astype(np.float64)
    ang2 = np.pi * kk * tt / p
    are = 2.0 * np.cos(ang2) / n
    are[:, 0] = 1.0 / n
    aim = -2.0 * np.sin(ang2) / n
    aim[:, 0] = np.cos(np.pi * tt[:, 0]) / n
    inv = np.concatenate([are, aim], axis=1)
    return (jnp.asarray(fwd_full[:, :p], _BF16), jnp.asarray(fwd_full, _BF16), jnp.asarray(inv, _BF16))


def _filter_spectrum_kernel(ga_ref, gb_ref, f_ref, o_ref, *, p):
    ga = ga_ref[...].astype(_BF16)
    gb = gb_ref[...].astype(_BF16)
    acc = jnp.dot(f_ref[:, :p], ga, preferred_element_type=_F32)
    acc = acc + jnp.dot(f_ref[:, p:], gb, preferred_element_type=_F32)
    o_ref[...] = acc.astype(o_ref.dtype)


def _filter_spectrum(kt, fwd_full, p, tc=256):
    rows, d = kt.shape
    nseg = rows // p - 1
    return pl.pallas_call(
        functools.partial(_filter_spectrum_kernel, p=p), grid=(nseg, d // tc),
        in_specs=[pl.BlockSpec((p, tc), lambda q, c: (q, c)),
                  pl.BlockSpec((p, tc), lambda q, c: (q + 1, c)), _const_spec(fwd_full.shape)],
        out_specs=pl.BlockSpec((None, 2 * p, tc), lambda q, c: (q, 0, c)),
        out_shape=jax.ShapeDtypeStruct((nseg, 2 * p, d), _F32),
        compiler_params=_params("parallel", "parallel"), name="hyena_filter_spectrum")(kt, kt, fwd_full)


def _long_conv_kernel(gate_ref, z_ref, kf_ref, skip_ref, ff_ref, ai_ref, o_ref, zf_ref, yfa_ref,
                      yfb_ref, *, p, nb, tc):
    ch = _CONV_ROWS

    def fwd_body(i, carry):
        zb = z_ref[pl.ds(pl.multiple_of(i * p, p), p), :]
        zf_ref[i] = jnp.dot(ff_ref[...], zb, preferred_element_type=_F32)
        return carry

    lax.fori_loop(0, nb, fwd_body, 0)
    skip = skip_ref[...]
    first_row = lax.broadcasted_iota(jnp.int32, (ch, 1), 0) == 0

    def spectrum(j, yf_ref):
        for rc in range(p // ch):
            re = slice(rc * ch, (rc + 1) * ch)
            im = slice(p + rc * ch, p + (rc + 1) * ch)
            a = jnp.zeros((ch, tc), _F32)
            bm = jnp.zeros((ch, tc), _F32)
            cc = jnp.zeros((ch, tc), _F32)
            for i in range(nb):
                q = j - i + (nb - 1)
                zre, zim = zf_ref[i, re, :], zf_ref[i, im, :]
                kre, kim = kf_ref[q, re, :], kf_ref[q, im, :]
                a = a + zre * kre
                bm = bm + zim * kim
                cc = cc + (zre * kim + zim * kre)
            if rc == 0:
                yf_ref[re, :] = jnp.where(first_row, a, a - bm).astype(_BF16)
                yf_ref[im, :] = jnp.where(first_row, bm, cc).astype(_BF16)
            else:
                yf_ref[re, :] = (a - bm).astype(_BF16)
                yf_ref[im, :] = cc.astype(_BF16)

    def finish(j, yf_ref):
        y = jnp.dot(ai_ref[...], yf_ref[...], preferred_element_type=_F32)
        rows = pl.ds(pl.multiple_of(j * p, p), p)
        zt = z_ref[rows, :].astype(_F32)
        gt = gate_ref[rows, :].astype(_F32)
        o_ref[rows, :] = (gt * (y + zt * skip)).astype(o_ref.dtype)

    spectrum(0, yfa_ref)
    if nb > 1:
        def pair_body(k, carry):
            finish(2 * k, yfa_ref)
            spectrum(2 * k + 1, yfb_ref)
            finish(2 * k + 1, yfb_ref)
            spectrum(2 * k + 2, yfa_ref)
            return carry

        lax.fori_loop(0, nb // 2 - 1, pair_body, 0)
        finish(nb - 2, yfa_ref)
        spectrum(nb - 1, yfb_ref)
        finish(nb - 1, yfb_ref)
    else:
        finish(0, yfa_ref)


def _long_conv(gate, z, kf, skip_o, fwd_half, inv, p):
    b, l, d = z.shape
    nb = l // p
    assert nb == 1 or nb % 2 == 0
    spectra_bytes_per_lane = (3 * nb - 1) * 2 * p * 4
    tc = 256 if 256 * spectra_bytes_per_lane <= _V7X_VMEM_LIMIT_BYTES // 2 else 128
    seq = pl.BlockSpec((None, l, tc), lambda c, bi: (bi, 0, c))
    kf_spec = pl.BlockSpec((2 * nb - 1, 2 * p, tc), lambda c, bi: (0, 0, c),
                           pipeline_mode=pl.Buffered(1))
    return pl.pallas_call(
        functools.partial(_long_conv_kernel, p=p, nb=nb, tc=tc), grid=(d // tc, b),
        in_specs=[seq, seq, kf_spec, pl.BlockSpec((1, tc), lambda c, bi: (0, c)),
                  _const_spec(fwd_half.shape), _const_spec(inv.shape)],
        out_specs=seq, out_shape=jax.ShapeDtypeStruct((b, l, d), _BF16),
        scratch_shapes=[pltpu.VMEM((nb, 2 * p, tc), _F32), pltpu.VMEM((2 * p, tc), _BF16),
                        pltpu.VMEM((2 * p, tc), _BF16)],
        compiler_params=_params("parallel", "arbitrary"), name="hyena_long_conv")(
            gate, z, kf, skip_o.astype(_F32).reshape(1, d), fwd_half, inv)


def _hyena_layer(x, norm_g, w_in, conv_w, conv_b, w1, b1, w2, b2, w3, b3, freq, w4, skip, w_out):
    b, l, d = x.shape
    p = min(l, max(_CONV_P, l // 8))
    x1, x2, v = _hyena_in(x, norm_g, w_in.astype(_BF16), conv_w, conv_b)
    k0, k1 = _hyena_two_sided_filters(l, w1, b1, w2, b2, w3, b3, freq, w4)
    fwd_half, fwd_full, inv = _dft_matrices(p)
    z = v
    for o, (gate, kt) in enumerate(((x1, k0), (x2, k1))):
        kf = _filter_spectrum(kt, fwd_full, p)
        z = _long_conv(gate, z, kf, skip[o], fwd_half, inv, p)
    y = _matmul_residual(z.reshape(b * l, d), w_out.astype(_BF16), x.reshape(b * l, d))
    return y.reshape(b, l, d)


def _trunk(x, rel_bias, ffn1_norm, ffn1_w, mixer_norm, attn_w_qkv, attn_w_o, hy, ffn2_norm, ffn2_w,
           final_norm):
    b, s, d = x.shape
    depth = ffn1_norm.shape[0]
    flat = lambda a: a.reshape(b * s, d)
    for i in range(depth):
        x = _ffn(flat(x), ffn1_norm[i], *ffn1_w[i]).reshape(b, s, d)
        j = i // 2
        if i % 2 == 0:
            x = _dilated_attention_layer(x, mixer_norm[i], attn_w_qkv[j], attn_w_o[j], rel_bias)
        else:
            x = _hyena_layer(x, mixer_norm[i], *[a[j] for a in hy])
        fin = final_norm if i == depth - 1 else None
        x = _ffn(flat(x), ffn2_norm[i], *ffn2_w[i], final_g=fin).reshape(b, s, d)
    return x


def kernel(x_prompt, x_sample, rel_bias, ffn1_norm, ffn1_w_gate_up, ffn1_w_down, mixer_norm, attn_w_qkv, attn_w_o, hyena_w_in, hyena_conv_w, hyena_conv_b, hyena_filt_w1, hyena_filt_b1, hyena_filt_w2, hyena_filt_b2, hyena_filt_w3, hyena_filt_b3, hyena_filt_freq, hyena_filt_w4, hyena_skip, hyena_w_out, ffn2_norm, ffn2_w_gate_up, ffn2_w_down, final_norm):
    depth = ffn1_norm.shape[0]
    ffn1_w = [_ffn_weights(ffn1_w_gate_up[i], ffn1_w_down[i]) for i in range(depth)]
    ffn2_w = [_ffn_weights(ffn2_w_gate_up[i], ffn2_w_down[i]) for i in range(depth)]
    hy = (hyena_w_in, hyena_conv_w, hyena_conv_b, hyena_filt_w1, hyena_filt_b1, hyena_filt_w2,
          hyena_filt_b2, hyena_filt_w3, hyena_filt_b3, hyena_filt_freq, hyena_filt_w4, hyena_skip,
          hyena_w_out)
    args = (rel_bias, ffn1_norm, ffn1_w, mixer_norm, attn_w_qkv, attn_w_o, hy, ffn2_norm, ffn2_w,
            final_norm)
    return (_trunk(x_prompt, *args), _trunk(x_sample, *args))
```

```python
import functools
import math

import numpy as np
import jax
import jax.numpy as jnp
from jax import lax
from jax.experimental import pallas as pl
from jax.experimental.pallas import tpu as pltpu

D_MODEL = 1024
HEAD_DIM = 64
HEADS_PER_GROUP = D_MODEL // HEAD_DIM
DILATED_GROUPS = ((128, 1), (512, 4), (2048, 16))
N_GROUPS = len(DILATED_GROUPS)
NUM_BUCKETS = 32
MAX_DISTANCE = 1024
HYENA_EMB_DIM = 33
HYENA_FILTER_WIDTH = 64
FAST_DECAY_PCT = 0.3
SLOW_DECAY_PCT = 1.5
DECAY_TARGET = 1e-2
D_FF = 2816
RMS_EPS = 1e-6

_F32 = jnp.float32
_BF16 = jnp.bfloat16
_V7X_VMEM_LIMIT_BYTES = 56 * 1024 * 1024
_LANES = 128
_NEG = -1e30
_LOG2E = math.log2(math.e)

_TQ = 128
_HALF_W = 64
_TK = _TQ + 2 * _HALF_W
_ATTN_MAX_SUBTILES = 4
_MXU_DIM = 256
_CONV_P = 512
_CONV_ROWS = 32
_CONV_STATIC_BLOCKS = 4


def _params(*sem):
    return pltpu.CompilerParams(dimension_semantics=sem, vmem_limit_bytes=_V7X_VMEM_LIMIT_BYTES)


def _const_spec(shape):
    nd = len(shape)
    return pl.BlockSpec(shape, lambda *_: (0,) * nd, pipeline_mode=pl.Buffered(1))


def _rms(x, g):
    y = x * lax.rsqrt(jnp.mean(x * x, axis=-1, keepdims=True) + RMS_EPS)
    return y * g


def _ffn_kernel(*refs, final):
    if final:
        x_ref, g_ref, wgu_ref, wd_ref, gf_ref, o_ref = refs
    else:
        x_ref, g_ref, wgu_ref, wd_ref, o_ref = refs
    f = wd_ref.shape[0]
    n_tiles = f // _MXU_DIM
    bounds = [0, (n_tiles + 1) // 2 * _MXU_DIM, f]
    x = x_ref[...]
    xb = _rms(x, g_ref[...]).astype(_BF16)
    acc = jnp.zeros(x.shape, _F32)
    for lo, hi in zip(bounds[:-1], bounds[1:]):
        gate = jnp.dot(xb, wgu_ref[:, lo:hi], preferred_element_type=_F32)
        up = jnp.dot(xb, wgu_ref[:, f + lo:f + hi], preferred_element_type=_F32)
        act = (gate * jax.nn.sigmoid(gate)) * up
        acc = acc + jnp.dot(act.astype(_BF16), wd_ref[lo:hi, :], preferred_element_type=_F32)
    y = x + 0.5 * acc
    if final:
        y = _rms(y, gf_ref[...])
    o_ref[...] = y


def _ffn(x2d, norm_g, wgu, wd, final_g=None, tm=512):
    t, d = x2d.shape
    final = final_g is not None
    in_specs = [pl.BlockSpec((tm, d), lambda i: (i, 0)), _const_spec((1, d)),
                _const_spec(wgu.shape), _const_spec(wd.shape)]
    args = [x2d, norm_g.reshape(1, d), wgu, wd]
    if final:
        in_specs.append(_const_spec((1, d)))
        args.append(final_g.reshape(1, d))
    return pl.pallas_call(
        functools.partial(_ffn_kernel, final=final),
        grid=(t // tm,), in_specs=in_specs,
        out_specs=pl.BlockSpec((tm, d), lambda i: (i, 0)),
        out_shape=jax.ShapeDtypeStruct((t, d), _F32),
        compiler_params=_params("parallel"), name="ffn")(*args)


def _ffn_weights(w_gate_up, w_down):
    return w_gate_up.astype(_BF16), w_down.astype(_BF16)


def _matmul_residual_kernel(a_ref, w_ref, x_ref, o_ref):
    o_ref[...] = x_ref[...] + jnp.dot(a_ref[...], w_ref[...], preferred_element_type=_F32)


def _matmul_residual(a2d, w_bf16, x2d, tm=1024):
    t, d = x2d.shape
    k = a2d.shape[1]
    return pl.pallas_call(
        _matmul_residual_kernel, grid=(t // tm,),
        in_specs=[pl.BlockSpec((tm, k), lambda i: (i, 0)), _const_spec(w_bf16.shape),
                  pl.BlockSpec((tm, d), lambda i: (i, 0))],
        out_specs=pl.BlockSpec((tm, d), lambda i: (i, 0)),
        out_shape=jax.ShapeDtypeStruct((t, d), _F32),
        compiler_params=_params("parallel"), name="matmul_residual")(a2d, w_bf16, x2d)


def _t5_bucket(rel):
    nb = NUM_BUCKETS // 2
    max_exact = nb // 2
    ret = (rel > 0).astype(np.int32) * nb
    n = np.abs(rel)
    large = max_exact + (np.log(np.maximum(n, 1) / max_exact)
                         / math.log(MAX_DISTANCE / max_exact) * (nb - max_exact)).astype(np.int32)
    large = np.minimum(large, nb - 1)
    return ret + np.where(n < max_exact, n, large)


def _attn_bucket_tables(dilation):
    delta = np.arange(_TK)[None, :] - _HALF_W - np.arange(_TQ)[:, None]
    bucket = _t5_bucket(delta * dilation)
    band = np.abs(delta) <= _HALF_W
    kk = np.arange(_TK)[None, :]
    tables = []
    for variant in range(4):
        ok = band
        if variant & 1:
            ok = ok & (kk >= _HALF_W)
        if variant & 2:
            ok = ok & (kk < _TQ + _HALF_W)
        tables.append(np.where(ok, bucket, -1))
    return jnp.asarray(np.stack(tables), jnp.int32)


def _bias_tiles_kernel(rb_ref, bucket_ref, o_ref):
    hp = pl.program_id(1)
    bucket = bucket_ref[...]
    for hh in range(2):
        acc = jnp.full((_TQ, _TK), _NEG, _F32)
        for bkt in range(NUM_BUCKETS):
            acc = jnp.where(bucket == bkt, rb_ref[2 * hp + hh, bkt] * _LOG2E, acc)
        o_ref[hh * _TQ:(hh + 1) * _TQ, :] = acc


def _attn_bias_tiles(rel_bias_g, dilation):
    tables = _attn_bucket_tables(dilation)
    npair = HEADS_PER_GROUP // 2
    return pl.pallas_call(
        _bias_tiles_kernel, grid=(4, npair),
        in_specs=[pl.BlockSpec(memory_space=pltpu.SMEM),
                  pl.BlockSpec((None, _TQ, _TK), lambda v, hp: (v, 0, 0))],
        out_specs=pl.BlockSpec((None, None, 2 * _TQ, _TK), lambda v, hp: (v, hp, 0, 0)),
        out_shape=jax.ShapeDtypeStruct((4, npair, 2 * _TQ, _TK), _F32),
        compiler_params=_params("parallel", "parallel"), name="attn_bias_tiles")(
            rel_bias_g.astype(_F32).T, tables)


def _attn_kernel(q_ref, kp_ref, kc_ref, kn_ref, vp_ref, vc_ref, vn_ref, bias_ref,
                 o_ref, stat_ref, *, nr, nsub):
    i = pl.program_id(2)
    is_first = jnp.where(i == 0, 1, 0)
    is_last = jnp.where(i == pl.num_programs(2) - 1, 2, 0)
    lane = lax.broadcasted_iota(jnp.int32, (1, _LANES), 1)
    low = lane < HEAD_DIM
    zero = jnp.zeros((), _BF16)
    cur_rows = nsub * _TQ

    def window(prev_ref, cur_ref, next_ref, ri, sub, sl):
        start, end = sub * _TQ - _HALF_W, (sub + 1) * _TQ + _HALF_W
        pieces = []
        if start < 0:
            pieces.append(prev_ref[ri, _TQ - _HALF_W:, sl])
        pieces.append(cur_ref[ri, max(start, 0):min(end, cur_rows), sl])
        if end > cur_rows:
            pieces.append(next_ref[ri, :_HALF_W, sl])
        return jnp.concatenate(pieces, axis=0)

    for ri in range(nr):
        for sub in range(nsub):
            rows = slice(sub * _TQ, (sub + 1) * _TQ)
            variant = (is_first if sub == 0 else 0) + (is_last if sub == nsub - 1 else 0)
            stat = jnp.zeros((_TQ, _LANES), _F32)
            for hp in range(HEADS_PER_GROUP // 2):
                sl = slice(hp * _LANES, (hp + 1) * _LANES)
                q2 = q_ref[ri, rows, sl]
                qq = jnp.concatenate([jnp.where(low, q2, zero), jnp.where(low, zero, q2)], axis=0)
                k2 = window(kp_ref, kc_ref, kn_ref, ri, sub, sl)
                v2 = window(vp_ref, vc_ref, vn_ref, ri, sub, sl)
                s = lax.dot_general(qq, k2, (((1,), (1,)), ((), ())),
                                    preferred_element_type=_F32)
                s = s + bias_ref[variant, hp]
                m = jnp.max(s, axis=-1, keepdims=True)
                p = jnp.exp2(s - m)
                l = jnp.sum(p, axis=-1, keepdims=True)
                pv = jnp.dot(p.astype(_BF16), v2, preferred_element_type=_F32)
                o_ref[ri, rows, sl] = jnp.where(low, pv[:_TQ], pv[_TQ:]).astype(o_ref.dtype)
                h0, h1 = 2 * hp, 2 * hp + 1
                stat = jnp.where(lane == h0, m[:_TQ], jnp.where(lane == h1, m[_TQ:], stat))
                stat = jnp.where(lane == HEADS_PER_GROUP + h0, l[:_TQ],
                                 jnp.where(lane == HEADS_PER_GROUP + h1, l[_TQ:], stat))
            stat_ref[ri, rows, :] = stat


def _qkv_group_kernel(x_ref, g_ref, w_ref, o_ref, xn_ref, xp_ref, *, d, tm):
    rows = tm // d
    xn = _rms(x_ref[...], g_ref[...])
    if d == 1:
        xp_ref[...] = xn.astype(_BF16)
    else:
        for c in range(D_MODEL // _LANES):
            xn_ref[c] = xn[:, c * _LANES:(c + 1) * _LANES]
        for r in range(d):
            for c in range(D_MODEL // _LANES):
                xp_ref[r * rows:(r + 1) * rows, c * _LANES:(c + 1) * _LANES] = (
                    xn_ref[c, pl.ds(r, rows, stride=d), :].astype(_BF16))
    for j in range(3):
        cs = slice(j * D_MODEL, (j + 1) * D_MODEL)
        y = jnp.dot(xp_ref[...], w_ref[:, cs], preferred_element_type=_F32).astype(o_ref.dtype)
        for r in range(d):
            o_ref[r, :, cs] = y[r * rows:(r + 1) * rows]


def _qkv_group(x, norm_g, w_g_bf16, dilation, tm=512):
    b, s, dm = x.shape
    d = dilation
    n3 = w_g_bf16.shape[1]
    return pl.pallas_call(
        functools.partial(_qkv_group_kernel, d=d, tm=tm), grid=(b, s // tm),
        in_specs=[pl.BlockSpec((None, tm, dm), lambda bi, i: (bi, i, 0)), _const_spec((1, dm)),
                  _const_spec(w_g_bf16.shape)],
        out_specs=pl.BlockSpec((None, d, tm // d, n3), lambda bi, i: (bi, 0, i, 0)),
        out_shape=jax.ShapeDtypeStruct((b, d, s // d, n3), _BF16),
        scratch_shapes=[pltpu.VMEM((dm // _LANES, tm, _LANES), _F32), pltpu.VMEM((tm, dm), _BF16)],
        compiler_params=_params("parallel", "parallel"), name=f"qkv_d{d}")(
            x, norm_g.reshape(1, dm), w_g_bf16)


def _group_attention(qkv, bias_tiles):
    b, d, lc, _ = qkv.shape
    n_tq = lc // _TQ
    nsub = min(_ATTN_MAX_SUBTILES, n_tq)
    nr = min(d, _ATTN_MAX_SUBTILES // nsub)
    rows = nsub * _TQ

    def cur(which):
        return pl.BlockSpec((None, nr, rows, D_MODEL), lambda bi, r, i: (bi, r, i, which))

    def prev(which):
        return pl.BlockSpec((None, nr, _TQ, D_MODEL),
                            lambda bi, r, i: (bi, r, jnp.maximum(i * nsub - 1, 0), which))

    def nxt(which):
        return pl.BlockSpec((None, nr, _TQ, D_MODEL),
                            lambda bi, r, i: (bi, r, jnp.minimum((i + 1) * nsub, n_tq - 1), which))

    return pl.pallas_call(
        functools.partial(_attn_kernel, nr=nr, nsub=nsub), grid=(b, d // nr, n_tq // nsub),
        in_specs=[cur(0), prev(1), cur(1), nxt(1), prev(2), cur(2), nxt(2),
                  _const_spec(bias_tiles.shape)],
        out_specs=[pl.BlockSpec((None, nr, rows, D_MODEL), lambda bi, r, i: (bi, r, i, 0)),
                   pl.BlockSpec((None, nr, rows, _LANES), lambda bi, r, i: (bi, r, i, 0))],
        out_shape=[jax.ShapeDtypeStruct((b, d, lc, D_MODEL), _BF16),
                   jax.ShapeDtypeStruct((b, d, lc, _LANES), _F32)],
        compiler_params=_params("parallel", "parallel", "arbitrary"), name=f"attn_d{d}")(
            qkv, qkv, qkv, qkv, qkv, qkv, qkv, bias_tiles)


def _merge_kernel(o0_ref, o1_ref, o2_ref, l0_ref, l1_ref, l2_ref, x_ref, e_ref, wo_ref, out_ref,
                  of_ref, lf_ref, *, tm):
    def natural(o_ref, l_ref, slot):
        d = o_ref.shape[0]
        if d == 1:
            return o_ref[0].astype(_F32), l_ref[0]
        rows = tm // d
        nblk = D_MODEL // _LANES
        for r in range(d):
            o_r = o_ref[r].astype(_F32)
            for c in range(nblk):
                of_ref[slot, c, pl.ds(r, rows, stride=d), :] = o_r[:, c * _LANES:(c + 1) * _LANES]
            lf_ref[slot, pl.ds(r, rows, stride=d), :] = l_ref[r]
        return jnp.concatenate([of_ref[slot, c] for c in range(nblk)], axis=1), lf_ref[slot]

    o0, s0 = natural(o0_ref, l0_ref, 0)
    o1, s1 = natural(o1_ref, l1_ref, 0)
    o2, s2 = natural(o2_ref, l2_ref, 1)
    head = lax.broadcasted_iota(jnp.int32, (1, _LANES), 1) < HEADS_PER_GROUP
    ms = [jnp.where(head, st, 0.0) for st in (s0, s1, s2)]
    ls = [jnp.where(head, pltpu.roll(st, _LANES - HEADS_PER_GROUP, 1), 1.0) for st in (s0, s1, s2)]
    m = jnp.maximum(jnp.maximum(ms[0], ms[1]), ms[2])
    a0, a1, a2 = jnp.exp2(ms[0] - m), jnp.exp2(ms[1] - m), jnp.exp2(ms[2] - m)
    inv = 1.0 / (a0 * ls[0] + a1 * ls[1] + a2 * ls[2])

    def expand(a):
        al = a * inv
        hi = al.astype(_BF16)
        lo = (al - hi.astype(_F32)).astype(_BF16)
        return jnp.dot(jnp.concatenate([hi, lo], axis=1), e_ref[...], preferred_element_type=_F32)

    merged = expand(a0) * o0 + expand(a1) * o1 + expand(a2) * o2
    out_ref[...] = x_ref[...] + jnp.dot(merged.astype(_BF16), wo_ref[...],
                                        preferred_element_type=_F32)


def _head_expand_matrix():
    e = np.zeros((2 * _LANES, D_MODEL), np.float32)
    for h in range(HEADS_PER_GROUP):
        e[h, h * HEAD_DIM:(h + 1) * HEAD_DIM] = 1.0
        e[_LANES + h, h * HEAD_DIM:(h + 1) * HEAD_DIM] = 1.0
    return jnp.asarray(e, _BF16)


def _merge(os_, stats, x, wo_bf16, tm=512):
    b, s, dm = x.shape
    e = _head_expand_matrix()

    def cls(a):
        d, w = a.shape[1], a.shape[3]
        return pl.BlockSpec((None, d, tm // d, w), lambda bi, i: (bi, 0, i, 0))

    row = pl.BlockSpec((None, tm, dm), lambda bi, i: (bi, i, 0))
    return pl.pallas_call(
        functools.partial(_merge_kernel, tm=tm), grid=(b, s // tm),
        in_specs=[cls(a) for a in os_] + [cls(a) for a in stats]
        + [row, _const_spec(e.shape), _const_spec(wo_bf16.shape)],
        out_specs=row, out_shape=jax.ShapeDtypeStruct((b, s, dm), _F32),
        scratch_shapes=[pltpu.VMEM((2, dm // _LANES, tm, _LANES), _F32),
                        pltpu.VMEM((2, tm, _LANES), _F32)],
        compiler_params=_params("parallel", "parallel"), name="attn_merge")(
            *os_, *stats, x, e, wo_bf16)


def _dilated_attention_layer(x, norm_g, w_qkv, w_o, rel_bias):
    n3 = 3 * D_MODEL
    os_, stats = [], []
    for g, (window, dil) in enumerate(DILATED_GROUPS):
        assert (window // 2) // dil == _HALF_W
        col_scale = jnp.where(jnp.arange(n3) < D_MODEL, _LOG2E * HEAD_DIM ** -0.5, 1.0).astype(_F32)
        w_g = (w_qkv[:, g * n3:(g + 1) * n3].astype(_F32) * col_scale).astype(_BF16)
        qkv = _qkv_group(x, norm_g, w_g, dil)
        bias = _attn_bias_tiles(rel_bias[:, g * HEADS_PER_GROUP:(g + 1) * HEADS_PER_GROUP], dil)
        o, stat = _group_attention(qkv, bias)
        os_.append(o)
        stats.append(stat)
    return _merge(os_, stats, x, w_o.astype(_BF16))


def _hyena_in_kernel(xp_ref, x_ref, xn_ref, g_ref, w_ref, cw_ref, cb_ref,
                     x1_ref, x2_ref, v_ref, *, tm):
    i = pl.program_id(1)
    last = pl.num_programs(1) - 1
    keep_prev = jnp.where(i > 0, 1.0, 0.0)
    keep_next = jnp.where(i < last, 1.0, 0.0)
    xe = jnp.concatenate([xp_ref[...] * keep_prev, x_ref[...], xn_ref[...] * keep_next], axis=0)
    xb = _rms(xe, g_ref[...]).astype(_BF16)
    n = tm + 16
    for j, o_ref in enumerate((x1_ref, x2_ref, v_ref)):
        cs = slice(j * D_MODEL, (j + 1) * D_MODEL)
        u = jnp.dot(xb, w_ref[:, cs], preferred_element_type=_F32)
        u_prev = pltpu.roll(u, 1, 0)[8:tm + 8]
        u_next = pltpu.roll(u, n - 1, 0)[8:tm + 8]
        cw = cw_ref[:, cs]
        y = (u_prev * cw[0:1] + u[8:tm + 8] * cw[1:2] + u_next * cw[2:3]) + cb_ref[:, cs]
        o_ref[...] = y.astype(o_ref.dtype)


def _hyena_in(x, norm_g, w_in_bf16, conv_w, conv_b, tm=512):
    b, s, d = x.shape
    n3 = w_in_bf16.shape[1]
    nblk8 = s // 8
    r8 = tm // 8
    main = pl.BlockSpec((None, tm, d), lambda bi, i: (bi, i, 0))
    prev = pl.BlockSpec((None, 8, d), lambda bi, i: (bi, jnp.maximum(i * r8 - 1, 0), 0))
    nxt = pl.BlockSpec((None, 8, d), lambda bi, i: (bi, jnp.minimum((i + 1) * r8, nblk8 - 1), 0))
    out = jax.ShapeDtypeStruct((b, s, d), _BF16)
    return pl.pallas_call(
        functools.partial(_hyena_in_kernel, tm=tm), grid=(b, s // tm),
        in_specs=[prev, main, nxt, _const_spec((1, d)), _const_spec(w_in_bf16.shape),
                  _const_spec((3, n3)), _const_spec((1, n3))],
        out_specs=[main, main, main], out_shape=[out, out, out],
        compiler_params=_params("parallel", "arbitrary"), name="hyena_in")(
            x, x, x, norm_g.reshape(1, d), w_in_bf16, conv_w.astype(_F32),
            conv_b.astype(_F32).reshape(1, n3))


def _filter_kernel(fr_ref, w1_ref, b1_ref, w2_ref, b2_ref, w3_ref, b3_ref, fq_ref, w4h_ref, w4l_ref,
                   dl_ref, k0_ref, k1_ref, *, seq_len, tr):
    hi = lax.Precision.HIGHEST
    r = pl.program_id(0) * tr + lax.broadcasted_iota(jnp.int32, (tr, 1), 0)
    pos = jnp.abs(r - seq_len).astype(_F32)
    t = pos / (seq_len - 1.0)
    wpos = (2.0 * math.pi) * pos / seq_len
    lane = lax.broadcasted_iota(jnp.int32, (1, _LANES), 1)
    bands = (HYENA_EMB_DIM - 1) // 2
    ang = fr_ref[...] * wpos
    z = jnp.where(lane == 0, t,
                  jnp.where(lane <= bands, jnp.cos(ang),
                            jnp.where(lane <= 2 * bands, -jnp.sin(ang), 0.0)))
    fq = fq_ref[...]
    h = jnp.sin(fq * (jnp.dot(z, w1_ref[...], precision=hi, preferred_element_type=_F32) + b1_ref[...]))
    h = jnp.sin(fq * (jnp.dot(h, w2_ref[...], precision=hi, preferred_element_type=_F32) + b2_ref[...]))
    h = jnp.sin(fq * (jnp.dot(h, w3_ref[...], precision=hi, preferred_element_type=_F32) + b3_ref[...]))
    decay = jnp.exp(-t * jnp.abs(dl_ref[...]))
    live = r > 0
    h_hi = h.astype(_BF16)
    h_lo = (h - h_hi.astype(_F32)).astype(_BF16)
    for o, k_ref in enumerate((k0_ref, k1_ref)):
        cs = slice(o * D_MODEL, (o + 1) * D_MODEL)
        w_hi = w4h_ref[:, cs]
        y = jnp.dot(h_hi, w_hi, preferred_element_type=_F32)
        y = y + jnp.dot(h_hi, w4l_ref[:, cs], preferred_element_type=_F32)
        y = y + jnp.dot(h_lo, w_hi, preferred_element_type=_F32)
        k_ref[...] = jnp.where(live, y * decay, 0.0)


def _hyena_two_sided_filters(seq_len, w1, b1, w2, b2, w3, b3, freq, w4, tr=512):
    fw = HYENA_FILTER_WIDTH
    n_tiles = 2 * seq_len // tr
    w4d = w4.astype(_F32).reshape(fw, 2, 2, D_MODEL).transpose(2, 0, 1, 3).reshape(2, fw, 2 * D_MODEL)
    w4_hi = w4d.astype(_BF16)
    w4_lo = (w4d - w4_hi.astype(_F32)).astype(_BF16)
    w4_spec = pl.BlockSpec((None, fw, 2 * D_MODEL),
                           lambda i: (jnp.where(i >= n_tiles // 2, 0, 1), 0, 0))
    bands = (HYENA_EMB_DIM - 1) // 2
    fr = jnp.linspace(1e-4, bands - 1, bands, dtype=_F32)
    fr_lanes = jnp.zeros((1, _LANES), _F32).at[0, 1:1 + bands].set(fr).at[0, 1 + bands:1 + 2 * bands].set(fr)
    w1p = jnp.zeros((_LANES, fw), _F32).at[:HYENA_EMB_DIM].set(w1.astype(_F32))
    max_decay = math.log(DECAY_TARGET) / FAST_DECAY_PCT
    min_decay = math.log(DECAY_TARGET) / SLOW_DECAY_PCT
    deltas = jnp.linspace(min_decay, max_decay, D_MODEL, dtype=_F32).reshape(1, D_MODEL)
    vec = lambda a: a.astype(_F32).reshape(1, fw)
    out = jax.ShapeDtypeStruct((2 * seq_len, D_MODEL), _F32)
    cs = _const_spec
    return pl.pallas_call(
        functools.partial(_filter_kernel, seq_len=seq_len, tr=tr), grid=(n_tiles,),
        in_specs=[cs((1, _LANES)), cs((_LANES, fw)), cs((1, fw)), cs((fw, fw)), cs((1, fw)),
                  cs((fw, fw)), cs((1, fw)), cs((1, fw)), w4_spec, w4_spec, cs((1, D_MODEL))],
        out_specs=[pl.BlockSpec((tr, D_MODEL), lambda i: (i, 0))] * 2, out_shape=[out, out],
        compiler_params=_params("parallel"), name="hyena_filter")(
            fr_lanes, w1p, vec(b1), w2.astype(_F32), vec(b2), w3.astype(_F32), vec(b3), vec(freq),
            w4_hi, w4_lo, deltas)


def _dft_matrices(p):
    n = 2 * p
    k = np.arange(p)[:, None].astype(np.float64)
    s = np.arange(p)[None, :].astype(np.float64)
    ang = np.pi * k * s / p
    fre = np.cos(ang)
    fim = -np.sin(ang)
    fim[0, :] = np.cos(np.pi * s[0])
    fwd_half = np.concatenate([fre, fim], axis=0)
    tt = (p + np.arange(p))[:, None].astype(np.float64)
    kk = np.arange(p)[None, :].astype(np.float64)
    ang2 = np.pi * kk * tt / p
    are = 2.0 * np.cos(ang2) / n
    are[:, 0] = 1.0 / n
    aim = -2.0 * np.sin(ang2) / n
    aim[:, 0] = np.cos(np.pi * tt[:, 0]) / n
    inv = np.concatenate([are, aim], axis=1)
    return jnp.asarray(fwd_half, _BF16), jnp.asarray(inv, _BF16)


def _filter_spectrum_kernel(g_ref, f_ref, o_ref, prev_ref):
    q = pl.program_id(1)
    a = jnp.dot(f_ref[...], g_ref[...].astype(_BF16), preferred_element_type=_F32)

    @pl.when(q > 0)
    def _():
        odd = (lax.broadcasted_iota(jnp.int32, (a.shape[0], 1), 0) & 1) == 1
        o_ref[...] = prev_ref[...] + jnp.where(odd, -a, a)

    prev_ref[...] = a


def _filter_spectrum(kt, fwd_half, p, tc=256):
    rows, d = kt.shape
    nblk = rows // p
    return pl.pallas_call(
        _filter_spectrum_kernel, grid=(d // tc, nblk),
        in_specs=[pl.BlockSpec((p, tc), lambda c, q: (q, c)), _const_spec(fwd_half.shape)],
        out_specs=pl.BlockSpec((None, 2 * p, tc), lambda c, q: (jnp.maximum(q - 1, 0), 0, c)),
        out_shape=jax.ShapeDtypeStruct((nblk - 1, 2 * p, d), _F32),
        scratch_shapes=[pltpu.VMEM((2 * p, tc), _F32)],
        compiler_params=_params("parallel", "arbitrary"), name="hyena_filter_spectrum")(kt, fwd_half)


def _long_conv_kernel(gate_ref, z_ref, kf_ref, skip_ref, ff_ref, ai_ref, o_ref, zf_ref, yfa_ref,
                      yfb_ref, *, p, nb, tc, ne):
    ch = _CONV_ROWS
    slots = zf_ref.shape[0]
    skip = skip_ref[...]
    first_row = lax.broadcasted_iota(jnp.int32, (ch, 1), 0) == 0

    def forward(e, i):
        zb = z_ref[e, pl.ds(pl.multiple_of(i * p, p), p), :]
        zf_ref[e % slots, i] = jnp.dot(ff_ref[...], zb, preferred_element_type=_F32)

    def spectrum(e, j, yf_ref):
        for rc in range(p // ch):
            re = slice(rc * ch, (rc + 1) * ch)
            im = slice(p + rc * ch, p + (rc + 1) * ch)
            a = jnp.zeros((ch, tc), _F32)
            bm = jnp.zeros((ch, tc), _F32)
            cc = jnp.zeros((ch, tc), _F32)
            for i in range(nb):
                q = j - i + (nb - 1)
                zre, zim = zf_ref[e % slots, i, re, :], zf_ref[e % slots, i, im, :]
                kre, kim = kf_ref[q, re, :], kf_ref[q, im, :]
                a = a + zre * kre
                bm = bm + zim * kim
                cc = cc + (zre * kim + zim * kre)
            if rc == 0:
                yf_ref[re, :] = jnp.where(first_row, a, a - bm).astype(_BF16)
                yf_ref[im, :] = jnp.where(first_row, bm, cc).astype(_BF16)
            else:
                yf_ref[re, :] = (a - bm).astype(_BF16)
                yf_ref[im, :] = cc.astype(_BF16)

    def finish(e, j, yf_ref):
        y = jnp.dot(ai_ref[...], yf_ref[...], preferred_element_type=_F32)
        rows = pl.ds(pl.multiple_of(j * p, p), p)
        zt = z_ref[e, rows, :].astype(_F32)
        gt = gate_ref[e, rows, :].astype(_F32)
        o_ref[e, rows, :] = (gt * (y + zt * skip)).astype(o_ref.dtype)

    yf = (yfa_ref, yfb_ref)
    if nb <= _CONV_STATIC_BLOCKS:
        for i in range(nb):
            forward(0, i)
        pending = None
        for e in range(ne):
            for j in range(nb):
                if pending is not None:
                    finish(*pending)
                if e + 1 < ne:
                    forward(e + 1, j)
                spectrum(e, j, yf[j % 2])
                pending = (e, j, yf[j % 2])
        finish(*pending)
    else:
        assert ne == 1 and nb % 2 == 0
        for i in range(nb):
            forward(0, i)
        spectrum(0, 0, yfa_ref)

        def pair_body(k, carry):
            finish(0, 2 * k, yfa_ref)
            spectrum(0, 2 * k + 1, yfb_ref)
            finish(0, 2 * k + 1, yfb_ref)
            spectrum(0, 2 * k + 2, yfa_ref)
            return carry

        lax.fori_loop(0, nb // 2 - 1, pair_body, 0)
        finish(0, nb - 2, yfa_ref)
        spectrum(0, nb - 1, yfb_ref)
        finish(0, nb - 1, yfb_ref)


def _long_conv(gate, z, kf, skip_o, fwd_half, inv, p):
    b, l, d = z.shape
    nb = l // p
    ne = 2 if (nb <= _CONV_STATIC_BLOCKS and b % 2 == 0) else 1
    slots = min(ne, 2)
    spectra_bytes_per_lane = (2 * nb - 1 + slots * nb) * 2 * p * 4
    tc = 256 if 256 * spectra_bytes_per_lane <= _V7X_VMEM_LIMIT_BYTES // 2 else 128
    seq = pl.BlockSpec((ne, l, tc), lambda c, bi: (bi, 0, c))
    kf_spec = pl.BlockSpec((2 * nb - 1, 2 * p, tc), lambda c, bi: (0, 0, c),
                           pipeline_mode=pl.Buffered(1))
    return pl.pallas_call(
        functools.partial(_long_conv_kernel, p=p, nb=nb, tc=tc, ne=ne), grid=(d // tc, b // ne),
        in_specs=[seq, seq, kf_spec, pl.BlockSpec((1, tc), lambda c, bi: (0, c)),
                  _const_spec(fwd_half.shape), _const_spec(inv.shape)],
        out_specs=seq, out_shape=jax.ShapeDtypeStruct((b, l, d), _BF16),
        scratch_shapes=[pltpu.VMEM((slots, nb, 2 * p, tc), _F32), pltpu.VMEM((2 * p, tc), _BF16),
                        pltpu.VMEM((2 * p, tc), _BF16)],
        compiler_params=_params("parallel", "arbitrary"), name="hyena_long_conv")(
            gate, z, kf, skip_o.astype(_F32).reshape(1, d), fwd_half, inv)


def _hyena_layer(x, norm_g, w_in, conv_w, conv_b, w1, b1, w2, b2, w3, b3, freq, w4, skip, w_out):
    b, l, d = x.shape
    p = min(l, max(_CONV_P, l // 8))
    x1, x2, v = _hyena_in(x, norm_g, w_in.astype(_BF16), conv_w, conv_b)
    k0, k1 = _hyena_two_sided_filters(l, w1, b1, w2, b2, w3, b3, freq, w4)
    fwd_half, inv = _dft_matrices(p)
    z = v
    for o, (gate, kt) in enumerate(((x1, k0), (x2, k1))):
        kf = _filter_spectrum(kt, fwd_half, p)
        z = _long_conv(gate, z, kf, skip[o], fwd_half, inv, p)
    y = _matmul_residual(z.reshape(b * l, d), w_out.astype(_BF16), x.reshape(b * l, d))
    return y.reshape(b, l, d)


def _trunk(x, rel_bias, ffn1_norm, ffn1_w, mixer_norm, attn_w_qkv, attn_w_o, hy, ffn2_norm, ffn2_w,
           final_norm):
    b, s, d = x.shape
    depth = ffn1_norm.shape[0]
    flat = lambda a: a.reshape(b * s, d)
    for i in range(depth):
        x = _ffn(flat(x), ffn1_norm[i], *ffn1_w[i]).reshape(b, s, d)
        j = i // 2
        if i % 2 == 0:
            x = _dilated_attention_layer(x, mixer_norm[i], attn_w_qkv[j], attn_w_o[j], rel_bias)
        else:
            x = _hyena_layer(x, mixer_norm[i], *[a[j] for a in hy])
        fin = final_norm if i == depth - 1 else None
        x = _ffn(flat(x), ffn2_norm[i], *ffn2_w[i], final_g=fin).reshape(b, s, d)
    return x


def kernel(x_prompt, x_sample, rel_bias, ffn1_norm, ffn1_w_gate_up, ffn1_w_down, mixer_norm, attn_w_qkv, attn_w_o, hyena_w_in, hyena_conv_w, hyena_conv_b, hyena_filt_w1, hyena_filt_b1, hyena_filt_w2, hyena_filt_b2, hyena_filt_w3, hyena_filt_b3, hyena_filt_freq, hyena_filt_w4, hyena_skip, hyena_w_out, ffn2_norm, ffn2_w_gate_up, ffn2_w_down, final_norm):
    depth = ffn1_norm.shape[0]
    ffn1_w = [_ffn_weights(ffn1_w_gate_up[i], ffn1_w_down[i]) for i in range(depth)]
    ffn2_w = [_ffn_weights(ffn2_w_gate_up[i], ffn2_w_down[i]) for i in range(depth)]
    hy = (hyena_w_in, hyena_conv_w, hyena_conv_b, hyena_filt_w1, hyena_filt_b1, hyena_filt_w2,
          hyena_filt_b2, hyena_filt_w3, hyena_filt_b3, hyena_filt_freq, hyena_filt_w4, hyena_skip,
          hyena_w_out)
    args = (rel_bias, ffn1_norm, ffn1_w, mixer_norm, attn_w_qkv, attn_w_o, hy, ffn2_norm, ffn2_w,
            final_norm)
    return (_trunk(x_prompt, *args), _trunk(x_sample, *args))
```

```python
import functools
import math

import numpy as np
import jax
import jax.numpy as jnp
from jax import lax
from jax.experimental import pallas as pl
from jax.experimental.pallas import tpu as pltpu

D_MODEL = 1024
HEAD_DIM = 64
HEADS_PER_GROUP = D_MODEL // HEAD_DIM
DILATED_GROUPS = ((128, 1), (512, 4), (2048, 16))
N_GROUPS = len(DILATED_GROUPS)
NUM_BUCKETS = 32
MAX_DISTANCE = 1024
HYENA_EMB_DIM = 33
HYENA_FILTER_WIDTH = 64
FAST_DECAY_PCT = 0.3
SLOW_DECAY_PCT = 1.5
DECAY_TARGET = 1e-2
D_FF = 2816
RMS_EPS = 1e-6

_F32 = jnp.float32
_BF16 = jnp.bfloat16
_V7X_VMEM_LIMIT_BYTES = 56 * 1024 * 1024
_LANES = 128
_NEG = -1e30
_LOG2E = math.log2(math.e)

_TQ = 128
_HALF_W = 64
_TK = _TQ + 2 * _HALF_W
_ATTN_MAX_SUBTILES = 4
_MXU_DIM = 256
_CONV_P = 512
_CONV_ROWS = 32
_SPECTRUM_DTYPE = jnp.bfloat16
_SPECTRUM_GROUP = 4
_CONV_STATIC_BLOCKS = 4


def _params(*sem):
    return pltpu.CompilerParams(dimension_semantics=sem, vmem_limit_bytes=_V7X_VMEM_LIMIT_BYTES)


def _const_spec(shape):
    nd = len(shape)
    return pl.BlockSpec(shape, lambda *_: (0,) * nd, pipeline_mode=pl.Buffered(1))


def _rms(x, g):
    y = x * lax.rsqrt(jnp.mean(x * x, axis=-1, keepdims=True) + RMS_EPS)
    return y * g


def _ffn_kernel(*refs, final):
    if final:
        x_ref, g_ref, wgu_ref, wd_ref, gf_ref, o_ref = refs
    else:
        x_ref, g_ref, wgu_ref, wd_ref, o_ref = refs
    f = wd_ref.shape[0]
    n_tiles = f // _MXU_DIM
    bounds = [0, (n_tiles + 1) // 2 * _MXU_DIM, f]
    x = x_ref[...]
    xb = _rms(x, g_ref[...]).astype(_BF16)
    acc = jnp.zeros(x.shape, _F32)
    for lo, hi in zip(bounds[:-1], bounds[1:]):
        gate = jnp.dot(xb, wgu_ref[:, lo:hi], preferred_element_type=_F32)
        up = jnp.dot(xb, wgu_ref[:, f + lo:f + hi], preferred_element_type=_F32)
        act = (gate * jax.nn.sigmoid(gate)) * up
        acc = acc + jnp.dot(act.astype(_BF16), wd_ref[lo:hi, :], preferred_element_type=_F32)
    y = x + 0.5 * acc
    if final:
        y = _rms(y, gf_ref[...])
    o_ref[...] = y


def _ffn(x2d, norm_g, wgu, wd, final_g=None, tm=512):
    t, d = x2d.shape
    final = final_g is not None
    in_specs = [pl.BlockSpec((tm, d), lambda i: (i, 0)), _const_spec((1, d)),
                _const_spec(wgu.shape), _const_spec(wd.shape)]
    args = [x2d, norm_g.reshape(1, d), wgu, wd]
    if final:
        in_specs.append(_const_spec((1, d)))
        args.append(final_g.reshape(1, d))
    return pl.pallas_call(
        functools.partial(_ffn_kernel, final=final),
        grid=(t // tm,), in_specs=in_specs,
        out_specs=pl.BlockSpec((tm, d), lambda i: (i, 0)),
        out_shape=jax.ShapeDtypeStruct((t, d), _F32),
        compiler_params=_params("parallel"), name="ffn")(*args)


def _ffn_weights(w_gate_up, w_down):
    return w_gate_up.astype(_BF16), w_down.astype(_BF16)


def _matmul_residual_kernel(a_ref, w_ref, x_ref, o_ref):
    o_ref[...] = x_ref[...] + jnp.dot(a_ref[...], w_ref[...], preferred_element_type=_F32)


def _matmul_residual(a2d, w_bf16, x2d, tm=1024):
    t, d = x2d.shape
    k = a2d.shape[1]
    return pl.pallas_call(
        _matmul_residual_kernel, grid=(t // tm,),
        in_specs=[pl.BlockSpec((tm, k), lambda i: (i, 0)), _const_spec(w_bf16.shape),
                  pl.BlockSpec((tm, d), lambda i: (i, 0))],
        out_specs=pl.BlockSpec((tm, d), lambda i: (i, 0)),
        out_shape=jax.ShapeDtypeStruct((t, d), _F32),
        compiler_params=_params("parallel"), name="matmul_residual")(a2d, w_bf16, x2d)


def _t5_bucket(rel):
    nb = NUM_BUCKETS // 2
    max_exact = nb // 2
    ret = (rel > 0).astype(np.int32) * nb
    n = np.abs(rel)
    large = max_exact + (np.log(np.maximum(n, 1) / max_exact)
                         / math.log(MAX_DISTANCE / max_exact) * (nb - max_exact)).astype(np.int32)
    large = np.minimum(large, nb - 1)
    return ret + np.where(n < max_exact, n, large)


def _attn_bucket_tables(dilation):
    delta = np.arange(_TK)[None, :] - _HALF_W - np.arange(_TQ)[:, None]
    bucket = _t5_bucket(delta * dilation)
    band = np.abs(delta) <= _HALF_W
    kk = np.arange(_TK)[None, :]
    tables = []
    for variant in range(4):
        ok = band
        if variant & 1:
            ok = ok & (kk >= _HALF_W)
        if variant & 2:
            ok = ok & (kk < _TQ + _HALF_W)
        tables.append(np.where(ok, bucket, -1))
    return jnp.asarray(np.stack(tables), jnp.int32)


def _bias_tiles_kernel(rb_ref, bucket_ref, o_ref):
    hp = pl.program_id(1)
    bucket = bucket_ref[...]
    for hh in range(2):
        acc = jnp.full((_TQ, _TK), _NEG, _F32)
        for bkt in range(NUM_BUCKETS):
            acc = jnp.where(bucket == bkt, rb_ref[2 * hp + hh, bkt] * _LOG2E, acc)
        o_ref[hh * _TQ:(hh + 1) * _TQ, :] = acc


def _attn_bias_tiles(rel_bias_g, dilation):
    tables = _attn_bucket_tables(dilation)
    npair = HEADS_PER_GROUP // 2
    return pl.pallas_call(
        _bias_tiles_kernel, grid=(4, npair),
        in_specs=[pl.BlockSpec(memory_space=pltpu.SMEM),
                  pl.BlockSpec((None, _TQ, _TK), lambda v, hp: (v, 0, 0))],
        out_specs=pl.BlockSpec((None, None, 2 * _TQ, _TK), lambda v, hp: (v, hp, 0, 0)),
        out_shape=jax.ShapeDtypeStruct((4, npair, 2 * _TQ, _TK), _F32),
        compiler_params=_params("parallel", "parallel"), name="attn_bias_tiles")(
            rel_bias_g.astype(_F32).T, tables)


def _attn_kernel(q_ref, kp_ref, kc_ref, kn_ref, vp_ref, vc_ref, vn_ref, bias_ref,
                 o_ref, stat_ref, *, nr, nsub):
    i = pl.program_id(2)
    is_first = jnp.where(i == 0, 1, 0)
    is_last = jnp.where(i == pl.num_programs(2) - 1, 2, 0)
    lane = lax.broadcasted_iota(jnp.int32, (1, _LANES), 1)
    low = lane < HEAD_DIM
    zero = jnp.zeros((), _BF16)
    cur_rows = nsub * _TQ

    def window(prev_ref, cur_ref, next_ref, ri, sub, sl):
        start, end = sub * _TQ - _HALF_W, (sub + 1) * _TQ + _HALF_W
        pieces = []
        if start < 0:
            pieces.append(prev_ref[ri, _TQ - _HALF_W:, sl])
        pieces.append(cur_ref[ri, max(start, 0):min(end, cur_rows), sl])
        if end > cur_rows:
            pieces.append(next_ref[ri, :_HALF_W, sl])
        return jnp.concatenate(pieces, axis=0)

    for ri in range(nr):
        for sub in range(nsub):
            rows = slice(sub * _TQ, (sub + 1) * _TQ)
            variant = (is_first if sub == 0 else 0) + (is_last if sub == nsub - 1 else 0)
            stat = jnp.zeros((_TQ, _LANES), _F32)
            for hp in range(HEADS_PER_GROUP // 2):
                sl = slice(hp * _LANES, (hp + 1) * _LANES)
                q2 = q_ref[ri, rows, sl]
                qq = jnp.concatenate([jnp.where(low, q2, zero), jnp.where(low, zero, q2)], axis=0)
                k2 = window(kp_ref, kc_ref, kn_ref, ri, sub, sl)
                v2 = window(vp_ref, vc_ref, vn_ref, ri, sub, sl)
                s = lax.dot_general(qq, k2, (((1,), (1,)), ((), ())),
                                    preferred_element_type=_F32)
                s = s + bias_ref[variant, hp]
                m = jnp.max(s, axis=-1, keepdims=True)
                p = jnp.exp2(s - m)
                l = jnp.sum(p, axis=-1, keepdims=True)
                pv = jnp.dot(p.astype(_BF16), v2, preferred_element_type=_F32)
                o_ref[ri, rows, sl] = jnp.where(low, pv[:_TQ], pv[_TQ:]).astype(o_ref.dtype)
                h0, h1 = 2 * hp, 2 * hp + 1
                stat = jnp.where(lane == h0, m[:_TQ], jnp.where(lane == h1, m[_TQ:], stat))
                stat = jnp.where(lane == HEADS_PER_GROUP + h0, l[:_TQ],
                                 jnp.where(lane == HEADS_PER_GROUP + h1, l[_TQ:], stat))
            stat_ref[ri, rows, :] = stat


def _qkv_group_kernel(x_ref, g_ref, w_ref, o_ref, xn_ref, xp_ref, *, d, tm):
    rows = tm // d
    xn = _rms(x_ref[...], g_ref[...])
    if d == 1:
        xp_ref[...] = xn.astype(_BF16)
    else:
        for c in range(D_MODEL // _LANES):
            xn_ref[c] = xn[:, c * _LANES:(c + 1) * _LANES]
        for r in range(d):
            for c in range(D_MODEL // _LANES):
                xp_ref[r * rows:(r + 1) * rows, c * _LANES:(c + 1) * _LANES] = (
                    xn_ref[c, pl.ds(r, rows, stride=d), :].astype(_BF16))
    for j in range(3):
        cs = slice(j * D_MODEL, (j + 1) * D_MODEL)
        y = jnp.dot(xp_ref[...], w_ref[:, cs], preferred_element_type=_F32).astype(o_ref.dtype)
        for r in range(d):
            o_ref[r, :, cs] = y[r * rows:(r + 1) * rows]


def _qkv_group(x, norm_g, w_g_bf16, dilation, tm=512):
    b, s, dm = x.shape
    d = dilation
    n3 = w_g_bf16.shape[1]
    return pl.pallas_call(
        functools.partial(_qkv_group_kernel, d=d, tm=tm), grid=(b, s // tm),
        in_specs=[pl.BlockSpec((None, tm, dm), lambda bi, i: (bi, i, 0)), _const_spec((1, dm)),
                  _const_spec(w_g_bf16.shape)],
        out_specs=pl.BlockSpec((None, d, tm // d, n3), lambda bi, i: (bi, 0, i, 0)),
        out_shape=jax.ShapeDtypeStruct((b, d, s // d, n3), _BF16),
        scratch_shapes=[pltpu.VMEM((dm // _LANES, tm, _LANES), _F32), pltpu.VMEM((tm, dm), _BF16)],
        compiler_params=_params("parallel", "parallel"), name=f"qkv_d{d}")(
            x, norm_g.reshape(1, dm), w_g_bf16)


def _group_attention(qkv, bias_tiles):
    b, d, lc, _ = qkv.shape
    n_tq = lc // _TQ
    nsub = min(_ATTN_MAX_SUBTILES, n_tq)
    nr = min(d, _ATTN_MAX_SUBTILES // nsub)
    rows = nsub * _TQ

    def cur(which):
        return pl.BlockSpec((None, nr, rows, D_MODEL), lambda bi, r, i: (bi, r, i, which))

    def prev(which):
        return pl.BlockSpec((None, nr, _TQ, D_MODEL),
                            lambda bi, r, i: (bi, r, jnp.maximum(i * nsub - 1, 0), which))

    def nxt(which):
        return pl.BlockSpec((None, nr, _TQ, D_MODEL),
                            lambda bi, r, i: (bi, r, jnp.minimum((i + 1) * nsub, n_tq - 1), which))

    return pl.pallas_call(
        functools.partial(_attn_kernel, nr=nr, nsub=nsub), grid=(b, d // nr, n_tq // nsub),
        in_specs=[cur(0), prev(1), cur(1), nxt(1), prev(2), cur(2), nxt(2),
                  _const_spec(bias_tiles.shape)],
        out_specs=[pl.BlockSpec((None, nr, rows, D_MODEL), lambda bi, r, i: (bi, r, i, 0)),
                   pl.BlockSpec((None, nr, rows, _LANES), lambda bi, r, i: (bi, r, i, 0))],
        out_shape=[jax.ShapeDtypeStruct((b, d, lc, D_MODEL), _BF16),
                   jax.ShapeDtypeStruct((b, d, lc, _LANES), _F32)],
        compiler_params=_params("parallel", "parallel", "arbitrary"), name=f"attn_d{d}")(
            qkv, qkv, qkv, qkv, qkv, qkv, qkv, bias_tiles)


def _merge_kernel(o0_ref, o1_ref, o2_ref, l0_ref, l1_ref, l2_ref, x_ref, e_ref, wo_ref, out_ref,
                  of_ref, lf_ref, *, tm):
    def natural(o_ref, l_ref, slot):
        d = o_ref.shape[0]
        if d == 1:
            return o_ref[0].astype(_F32), l_ref[0]
        rows = tm // d
        nblk = D_MODEL // _LANES
        for r in range(d):
            o_r = o_ref[r].astype(_F32)
            for c in range(nblk):
                of_ref[slot, c, pl.ds(r, rows, stride=d), :] = o_r[:, c * _LANES:(c + 1) * _LANES]
            lf_ref[slot, pl.ds(r, rows, stride=d), :] = l_ref[r]
        return jnp.concatenate([of_ref[slot, c] for c in range(nblk)], axis=1), lf_ref[slot]

    o0, s0 = natural(o0_ref, l0_ref, 0)
    o1, s1 = natural(o1_ref, l1_ref, 0)
    o2, s2 = natural(o2_ref, l2_ref, 1)
    head = lax.broadcasted_iota(jnp.int32, (1, _LANES), 1) < HEADS_PER_GROUP
    ms = [jnp.where(head, st, 0.0) for st in (s0, s1, s2)]
    ls = [jnp.where(head, pltpu.roll(st, _LANES - HEADS_PER_GROUP, 1), 1.0) for st in (s0, s1, s2)]
    m = jnp.maximum(jnp.maximum(ms[0], ms[1]), ms[2])
    a0, a1, a2 = jnp.exp2(ms[0] - m), jnp.exp2(ms[1] - m), jnp.exp2(ms[2] - m)
    inv = 1.0 / (a0 * ls[0] + a1 * ls[1] + a2 * ls[2])

    def expand(a):
        al = a * inv
        hi = al.astype(_BF16)
        lo = (al - hi.astype(_F32)).astype(_BF16)
        return jnp.dot(jnp.concatenate([hi, lo], axis=1), e_ref[...], preferred_element_type=_F32)

    merged = expand(a0) * o0 + expand(a1) * o1 + expand(a2) * o2
    out_ref[...] = x_ref[...] + jnp.dot(merged.astype(_BF16), wo_ref[...],
                                        preferred_element_type=_F32)


def _head_expand_matrix():
    e = np.zeros((2 * _LANES, D_MODEL), np.float32)
    for h in range(HEADS_PER_GROUP):
        e[h, h * HEAD_DIM:(h + 1) * HEAD_DIM] = 1.0
        e[_LANES + h, h * HEAD_DIM:(h + 1) * HEAD_DIM] = 1.0
    return jnp.asarray(e, _BF16)


def _merge(os_, stats, x, wo_bf16, tm=512):
    b, s, dm = x.shape
    e = _head_expand_matrix()

    def cls(a):
        d, w = a.shape[1], a.shape[3]
        return pl.BlockSpec((None, d, tm // d, w), lambda bi, i: (bi, 0, i, 0))

    row = pl.BlockSpec((None, tm, dm), lambda bi, i: (bi, i, 0))
    return pl.pallas_call(
        functools.partial(_merge_kernel, tm=tm), grid=(b, s // tm),
        in_specs=[cls(a) for a in os_] + [cls(a) for a in stats]
        + [row, _const_spec(e.shape), _const_spec(wo_bf16.shape)],
        out_specs=row, out_shape=jax.ShapeDtypeStruct((b, s, dm), _F32),
        scratch_shapes=[pltpu.VMEM((2, dm // _LANES, tm, _LANES), _F32),
                        pltpu.VMEM((2, tm, _LANES), _F32)],
        compiler_params=_params("parallel", "parallel"), name="attn_merge")(
            *os_, *stats, x, e, wo_bf16)


def _dilated_attention_layer(x, norm_g, w_qkv, w_o, rel_bias):
    n3 = 3 * D_MODEL
    os_, stats = [], []
    for g, (window, dil) in enumerate(DILATED_GROUPS):
        assert (window // 2) // dil == _HALF_W
        col_scale = jnp.where(jnp.arange(n3) < D_MODEL, _LOG2E * HEAD_DIM ** -0.5, 1.0).astype(_F32)
        w_g = (w_qkv[:, g * n3:(g + 1) * n3].astype(_F32) * col_scale).astype(_BF16)
        qkv = _qkv_group(x, norm_g, w_g, dil)
        bias = _attn_bias_tiles(rel_bias[:, g * HEADS_PER_GROUP:(g + 1) * HEADS_PER_GROUP], dil)
        o, stat = _group_attention(qkv, bias)
        os_.append(o)
        stats.append(stat)
    return _merge(os_, stats, x, w_o.astype(_BF16))


def _hyena_in_kernel(xp_ref, x_ref, xn_ref, g_ref, w_ref, cw_ref, cb_ref,
                     x1_ref, x2_ref, v_ref, *, tm):
    i = pl.program_id(1)
    last = pl.num_programs(1) - 1
    keep_prev = jnp.where(i > 0, 1.0, 0.0)
    keep_next = jnp.where(i < last, 1.0, 0.0)
    xe = jnp.concatenate([xp_ref[...] * keep_prev, x_ref[...], xn_ref[...] * keep_next], axis=0)
    xb = _rms(xe, g_ref[...]).astype(_BF16)
    n = tm + 16
    for j, o_ref in enumerate((x1_ref, x2_ref, v_ref)):
        cs = slice(j * D_MODEL, (j + 1) * D_MODEL)
        u = jnp.dot(xb, w_ref[:, cs], preferred_element_type=_F32)
        u_prev = pltpu.roll(u, 1, 0)[8:tm + 8]
        u_next = pltpu.roll(u, n - 1, 0)[8:tm + 8]
        cw = cw_ref[:, cs]
        y = (u_prev * cw[0:1] + u[8:tm + 8] * cw[1:2] + u_next * cw[2:3]) + cb_ref[:, cs]
        o_ref[...] = y.astype(o_ref.dtype)


def _hyena_in(x, norm_g, w_in_bf16, conv_w, conv_b, tm=512):
    b, s, d = x.shape
    n3 = w_in_bf16.shape[1]
    nblk8 = s // 8
    r8 = tm // 8
    main = pl.BlockSpec((None, tm, d), lambda bi, i: (bi, i, 0))
    prev = pl.BlockSpec((None, 8, d), lambda bi, i: (bi, jnp.maximum(i * r8 - 1, 0), 0))
    nxt = pl.BlockSpec((None, 8, d), lambda bi, i: (bi, jnp.minimum((i + 1) * r8, nblk8 - 1), 0))
    out = jax.ShapeDtypeStruct((b, s, d), _BF16)
    return pl.pallas_call(
        functools.partial(_hyena_in_kernel, tm=tm), grid=(b, s // tm),
        in_specs=[prev, main, nxt, _const_spec((1, d)), _const_spec(w_in_bf16.shape),
                  _const_spec((3, n3)), _const_spec((1, n3))],
        out_specs=[main, main, main], out_shape=[out, out, out],
        compiler_params=_params("parallel", "arbitrary"), name="hyena_in")(
            x, x, x, norm_g.reshape(1, d), w_in_bf16, conv_w.astype(_F32),
            conv_b.astype(_F32).reshape(1, n3))


def _filter_kernel(fr_ref, w1_ref, b1_ref, w2_ref, b2_ref, w3_ref, b3_ref, fq_ref, w4h_ref, w4l_ref,
                   dl_ref, k0_ref, k1_ref, *, seq_len, tr):
    hi = lax.Precision.HIGHEST
    r = pl.program_id(0) * tr + lax.broadcasted_iota(jnp.int32, (tr, 1), 0)
    pos = jnp.abs(r - seq_len).astype(_F32)
    t = pos / (seq_len - 1.0)
    wpos = (2.0 * math.pi) * pos / seq_len
    lane = lax.broadcasted_iota(jnp.int32, (1, _LANES), 1)
    bands = (HYENA_EMB_DIM - 1) // 2
    ang = fr_ref[...] * wpos
    z = jnp.where(lane == 0, t,
                  jnp.where(lane <= bands, jnp.cos(ang),
                            jnp.where(lane <= 2 * bands, -jnp.sin(ang), 0.0)))
    fq = fq_ref[...]
    h = jnp.sin(fq * (jnp.dot(z, w1_ref[...], precision=hi, preferred_element_type=_F32) + b1_ref[...]))
    h = jnp.sin(fq * (jnp.dot(h, w2_ref[...], precision=hi, preferred_element_type=_F32) + b2_ref[...]))
    h = jnp.sin(fq * (jnp.dot(h, w3_ref[...], precision=hi, preferred_element_type=_F32) + b3_ref[...]))
    decay = jnp.exp(-t * jnp.abs(dl_ref[...]))
    live = r > 0
    h_hi = h.astype(_BF16)
    h_lo = (h - h_hi.astype(_F32)).astype(_BF16)
    for o, k_ref in enumerate((k0_ref, k1_ref)):
        cs = slice(o * D_MODEL, (o + 1) * D_MODEL)
        w_hi = w4h_ref[:, cs]
        y = jnp.dot(h_hi, w_hi, preferred_element_type=_F32)
        y = y + jnp.dot(h_hi, w4l_ref[:, cs], preferred_element_type=_F32)
        y = y + jnp.dot(h_lo, w_hi, preferred_element_type=_F32)
        k_ref[...] = jnp.where(live, y * decay, 0.0)


def _hyena_two_sided_filters(seq_len, w1, b1, w2, b2, w3, b3, freq, w4, tr=512):
    fw = HYENA_FILTER_WIDTH
    n_tiles = 2 * seq_len // tr
    w4d = w4.astype(_F32).reshape(fw, 2, 2, D_MODEL).transpose(2, 0, 1, 3).reshape(2, fw, 2 * D_MODEL)
    w4_hi = w4d.astype(_BF16)
    w4_lo = (w4d - w4_hi.astype(_F32)).astype(_BF16)
    w4_spec = pl.BlockSpec((None, fw, 2 * D_MODEL),
                           lambda i: (jnp.where(i >= n_tiles // 2, 0, 1), 0, 0))
    bands = (HYENA_EMB_DIM - 1) // 2
    fr = jnp.linspace(1e-4, bands - 1, bands, dtype=_F32)
    fr_lanes = jnp.zeros((1, _LANES), _F32).at[0, 1:1 + bands].set(fr).at[0, 1 + bands:1 + 2 * bands].set(fr)
    w1p = jnp.zeros((_LANES, fw), _F32).at[:HYENA_EMB_DIM].set(w1.astype(_F32))
    max_decay = math.log(DECAY_TARGET) / FAST_DECAY_PCT
    min_decay = math.log(DECAY_TARGET) / SLOW_DECAY_PCT
    deltas = jnp.linspace(min_decay, max_decay, D_MODEL, dtype=_F32).reshape(1, D_MODEL)
    vec = lambda a: a.astype(_F32).reshape(1, fw)
    out = jax.ShapeDtypeStruct((2 * seq_len, D_MODEL), _F32)
    cs = _const_spec
    return pl.pallas_call(
        functools.partial(_filter_kernel, seq_len=seq_len, tr=tr), grid=(n_tiles,),
        in_specs=[cs((1, _LANES)), cs((_LANES, fw)), cs((1, fw)), cs((fw, fw)), cs((1, fw)),
                  cs((fw, fw)), cs((1, fw)), cs((1, fw)), w4_spec, w4_spec, cs((1, D_MODEL))],
        out_specs=[pl.BlockSpec((tr, D_MODEL), lambda i: (i, 0))] * 2, out_shape=[out, out],
        compiler_params=_params("parallel"), name="hyena_filter")(
            fr_lanes, w1p, vec(b1), w2.astype(_F32), vec(b2), w3.astype(_F32), vec(b3), vec(freq),
            w4_hi, w4_lo, deltas)


def _dft_matrices(p):
    n = 2 * p
    k = np.arange(p)[:, None].astype(np.float64)
    s = np.arange(p)[None, :].astype(np.float64)
    ang = np.pi * k * s / p
    fre = np.cos(ang)
    fim = -np.sin(ang)
    fim[0, :] = np.cos(np.pi * s[0])
    fwd_half = np.concatenate([fre, fim], axis=0)
    tt = (p + np.arange(p))[:, None].astype(np.float64)
    kk = np.arange(p)[None, :].astype(np.float64)
    ang2 = np.pi * kk * tt / p
    are = 2.0 * np.cos(ang2) / n
    are[:, 0] = 1.0 / n
    aim = -2.0 * np.sin(ang2) / n
    aim[:, 0] = np.cos(np.pi * tt[:, 0]) / n
    inv = np.concatenate([are, aim], axis=1)
    return jnp.asarray(fwd_half, _BF16), jnp.asarray(inv, _BF16)


def _filter_spectrum_kernel(g_ref, f_ref, o_ref, prev_ref):
    q = pl.program_id(1)
    a = jnp.dot(f_ref[...], g_ref[...].astype(_BF16), preferred_element_type=_F32)

    @pl.when(q > 0)
    def _():
        odd = (lax.broadcasted_iota(jnp.int32, (a.shape[0], 1), 0) & 1) == 1
        o_ref[...] = (prev_ref[...] + jnp.where(odd, -a, a)).astype(o_ref.dtype)

    prev_ref[...] = a


def _filter_spectrum(kt, fwd_half, p, tc=256):
    rows, d = kt.shape
    nblk = rows // p
    return pl.pallas_call(
        _filter_spectrum_kernel, grid=(d // tc, nblk),
        in_specs=[pl.BlockSpec((p, tc), lambda c, q: (q, c)), _const_spec(fwd_half.shape)],
        out_specs=pl.BlockSpec((None, 2 * p, tc), lambda c, q: (jnp.maximum(q - 1, 0), 0, c)),
        out_shape=jax.ShapeDtypeStruct((nblk - 1, 2 * p, d), _SPECTRUM_DTYPE),
        scratch_shapes=[pltpu.VMEM((2 * p, tc), _F32)],
        compiler_params=_params("parallel", "arbitrary"), name="hyena_filter_spectrum")(kt, fwd_half)


def _long_conv_kernel(gate_ref, z_ref, kf_ref, skip_ref, ff_ref, ai_ref, o_ref, zf_ref, yfa_ref,
                      yfb_ref, *, p, nb, tc, ne):
    ch = _CONV_ROWS
    slots = zf_ref.shape[0]
    skip = skip_ref[...]
    first_row = lax.broadcasted_iota(jnp.int32, (ch, 1), 0) == 0

    def forward(e, i):
        zb = z_ref[e, pl.ds(pl.multiple_of(i * p, p), p), :]
        zf_ref[e % slots, i] = jnp.dot(ff_ref[...], zb,
                                       preferred_element_type=_F32).astype(zf_ref.dtype)

    def spectrum(e, j, yf_ref):
        for rc in range(p // ch):
            re = slice(rc * ch, (rc + 1) * ch)
            im = slice(p + rc * ch, p + (rc + 1) * ch)
            a = bm = cc = None
            for i0 in range(0, nb, _SPECTRUM_GROUP):
                ga = jnp.zeros((ch, tc), zf_ref.dtype)
                gb = jnp.zeros((ch, tc), zf_ref.dtype)
                gc = jnp.zeros((ch, tc), zf_ref.dtype)
                for i in range(i0, min(i0 + _SPECTRUM_GROUP, nb)):
                    q = j - i + (nb - 1)
                    zre, zim = zf_ref[e % slots, i, re, :], zf_ref[e % slots, i, im, :]
                    kre, kim = kf_ref[q, re, :], kf_ref[q, im, :]
                    ga = ga + zre * kre
                    gb = gb + zim * kim
                    gc = gc + (zre * kim + zim * kre)
                if nb > _SPECTRUM_GROUP:
                    ga, gb, gc = ga.astype(_F32), gb.astype(_F32), gc.astype(_F32)
                a = ga if a is None else a + ga
                bm = gb if bm is None else bm + gb
                cc = gc if cc is None else cc + gc
            if rc == 0:
                yf_ref[re, :] = jnp.where(first_row, a, a - bm).astype(_BF16)
                yf_ref[im, :] = jnp.where(first_row, bm, cc).astype(_BF16)
            else:
                yf_ref[re, :] = (a - bm).astype(_BF16)
                yf_ref[im, :] = cc.astype(_BF16)

    def finish(e, j, yf_ref):
        y = jnp.dot(ai_ref[...], yf_ref[...], preferred_element_type=_F32)
        rows = pl.ds(pl.multiple_of(j * p, p), p)
        zt = z_ref[e, rows, :].astype(_F32)
        gt = gate_ref[e, rows, :].astype(_F32)
        o_ref[e, rows, :] = (gt * (y + zt * skip)).astype(o_ref.dtype)

    yf = (yfa_ref, yfb_ref)
    if nb <= _CONV_STATIC_BLOCKS:
        for i in range(nb):
            forward(0, i)
        pending = None
        for e in range(ne):
            for j in range(nb):
                if pending is not None:
                    finish(*pending)
                if e + 1 < ne:
                    forward(e + 1, j)
                spectrum(e, j, yf[j % 2])
                pending = (e, j, yf[j % 2])
        finish(*pending)
    else:
        assert ne == 1 and nb % 2 == 0
        for i in range(nb):
            forward(0, i)
        spectrum(0, 0, yfa_ref)

        def pair_body(k, carry):
            finish(0, 2 * k, yfa_ref)
            spectrum(0, 2 * k + 1, yfb_ref)
            finish(0, 2 * k + 1, yfb_ref)
            spectrum(0, 2 * k + 2, yfa_ref)
            return carry

        lax.fori_loop(0, nb // 2 - 1, pair_body, 0)
        finish(0, nb - 2, yfa_ref)
        spectrum(0, nb - 1, yfb_ref)
        finish(0, nb - 1, yfb_ref)


def _long_conv(gate, z, kf, skip_o, fwd_half, inv, p):
    b, l, d = z.shape
    nb = l // p
    ne = 2 if (nb <= _CONV_STATIC_BLOCKS and b % 2 == 0) else 1
    slots = min(ne, 2)
    spectra = (2 * nb - 1 + slots * nb) * 2 * p * jnp.dtype(_SPECTRUM_DTYPE).itemsize
    per_lane = spectra + 3 * 2 * ne * l * 2
    dft_bytes = 2 * (2 * p * p * 2)
    tc = 256 if 256 * per_lane + dft_bytes <= _V7X_VMEM_LIMIT_BYTES * 3 // 4 else 128
    seq = pl.BlockSpec((ne, l, tc), lambda c, bi: (bi, 0, c))
    kf_spec = pl.BlockSpec((2 * nb - 1, 2 * p, tc), lambda c, bi: (0, 0, c),
                           pipeline_mode=pl.Buffered(1))
    return pl.pallas_call(
        functools.partial(_long_conv_kernel, p=p, nb=nb, tc=tc, ne=ne), grid=(d // tc, b // ne),
        in_specs=[seq, seq, kf_spec, pl.BlockSpec((1, tc), lambda c, bi: (0, c)),
                  _const_spec(fwd_half.shape), _const_spec(inv.shape)],
        out_specs=seq, out_shape=jax.ShapeDtypeStruct((b, l, d), _BF16),
        scratch_shapes=[pltpu.VMEM((slots, nb, 2 * p, tc), _SPECTRUM_DTYPE),
                        pltpu.VMEM((2 * p, tc), _BF16),
                        pltpu.VMEM((2 * p, tc), _BF16)],
        compiler_params=_params("parallel", "arbitrary"), name="hyena_long_conv")(
            gate, z, kf, skip_o.astype(_F32).reshape(1, d), fwd_half, inv)


def _hyena_layer(x, norm_g, w_in, conv_w, conv_b, w1, b1, w2, b2, w3, b3, freq, w4, skip, w_out):
    b, l, d = x.shape
    p = min(l, max(_CONV_P, l // 8))
    x1, x2, v = _hyena_in(x, norm_g, w_in.astype(_BF16), conv_w, conv_b)
    k0, k1 = _hyena_two_sided_filters(l, w1, b1, w2, b2, w3, b3, freq, w4)
    fwd_half, inv = _dft_matrices(p)
    z = v
    for o, (gate, kt) in enumerate(((x1, k0), (x2, k1))):
        kf = _filter_spectrum(kt, fwd_half, p)
        z = _long_conv(gate, z, kf, skip[o], fwd_half, inv, p)
    y = _matmul_residual(z.reshape(b * l, d), w_out.astype(_BF16), x.reshape(b * l, d))
    return y.reshape(b, l, d)


def _trunk(x, rel_bias, ffn1_norm, ffn1_w, mixer_norm, attn_w_qkv, attn_w_o, hy, ffn2_norm, ffn2_w,
           final_norm):
    b, s, d = x.shape
    depth = ffn1_norm.shape[0]
    flat = lambda a: a.reshape(b * s, d)
    for i in range(depth):
        x = _ffn(flat(x), ffn1_norm[i], *ffn1_w[i]).reshape(b, s, d)
        j = i // 2
        if i % 2 == 0:
            x = _dilated_attention_layer(x, mixer_norm[i], attn_w_qkv[j], attn_w_o[j], rel_bias)
        else:
            x = _hyena_layer(x, mixer_norm[i], *[a[j] for a in hy])
        fin = final_norm if i == depth - 1 else None
        x = _ffn(flat(x), ffn2_norm[i], *ffn2_w[i], final_g=fin).reshape(b, s, d)
    return x


def kernel(x_prompt, x_sample, rel_bias, ffn1_norm, ffn1_w_gate_up, ffn1_w_down, mixer_norm, attn_w_qkv, attn_w_o, hyena_w_in, hyena_conv_w, hyena_conv_b, hyena_filt_w1, hyena_filt_b1, hyena_filt_w2, hyena_filt_b2, hyena_filt_w3, hyena_filt_b3, hyena_filt_freq, hyena_filt_w4, hyena_skip, hyena_w_out, ffn2_norm, ffn2_w_gate_up, ffn2_w_down, final_norm):
    depth = ffn1_norm.shape[0]
    ffn1_w = [_ffn_weights(ffn1_w_gate_up[i], ffn1_w_down[i]) for i in range(depth)]
    ffn2_w = [_ffn_weights(ffn2_w_gate_up[i], ffn2_w_down[i]) for i in range(depth)]
    hy = (hyena_w_in, hyena_conv_w, hyena_conv_b, hyena_filt_w1, hyena_filt_b1, hyena_filt_w2,
          hyena_filt_b2, hyena_filt_w3, hyena_filt_b3, hyena_filt_freq, hyena_filt_w4, hyena_skip,
          hyena_w_out)
    args = (rel_bias, ffn1_norm, ffn1_w, mixer_norm, attn_w_qkv, attn_w_o, hy, ffn2_norm, ffn2_w,
            final_norm)
    return (_trunk(x_prompt, *args), _trunk(x_sample, *args))
```

```python
import functools
import math

import numpy as np
import jax
import jax.numpy as jnp
from jax import lax
from jax.experimental import pallas as pl
from jax.experimental.pallas import tpu as pltpu

D_MODEL = 1024
HEAD_DIM = 64
HEADS_PER_GROUP = D_MODEL // HEAD_DIM
DILATED_GROUPS = ((128, 1), (512, 4), (2048, 16))
N_GROUPS = len(DILATED_GROUPS)
NUM_BUCKETS = 32
MAX_DISTANCE = 1024
HYENA_EMB_DIM = 33
HYENA_FILTER_WIDTH = 64
FAST_DECAY_PCT = 0.3
SLOW_DECAY_PCT = 1.5
DECAY_TARGET = 1e-2
D_FF = 2816
RMS_EPS = 1e-6

_F32 = jnp.float32
_BF16 = jnp.bfloat16
_V7X_VMEM_LIMIT_BYTES = 58 * 1024 * 1024
_LANES = 128
_NEG = -1e30
_LOG2E = math.log2(math.e)

_TQ = 128
_HALF_W = 64
_TK = _TQ + 2 * _HALF_W
_ATTN_MAX_SUBTILES = 4
_MXU_DIM = 256
_CONV_P = 512
_CONV_ROWS = 32
_SPECTRUM_DTYPE = jnp.bfloat16
_SPECTRUM_GROUP = 4
_CONV_STATIC_BLOCKS = 4


def _params(*sem):
    return pltpu.CompilerParams(dimension_semantics=sem, vmem_limit_bytes=_V7X_VMEM_LIMIT_BYTES)


def _const_spec(shape):
    nd = len(shape)
    return pl.BlockSpec(shape, lambda *_: (0,) * nd, pipeline_mode=pl.Buffered(1))


def _rms(x, g):
    y = x * lax.rsqrt(jnp.mean(x * x, axis=-1, keepdims=True) + RMS_EPS)
    return y * g


def _ffn_kernel(*refs, final):
    if final:
        x_ref, g_ref, wgu_ref, wd_ref, gf_ref, o_ref = refs
    else:
        x_ref, g_ref, wgu_ref, wd_ref, o_ref = refs
    f = wd_ref.shape[0]
    n_tiles = f // _MXU_DIM
    bounds = [0, (n_tiles + 1) // 2 * _MXU_DIM, f]
    x = x_ref[...]
    xb = _rms(x, g_ref[...]).astype(_BF16)
    acc = jnp.zeros(x.shape, _F32)
    for lo, hi in zip(bounds[:-1], bounds[1:]):
        gate = jnp.dot(xb, wgu_ref[:, lo:hi], preferred_element_type=_F32)
        up = jnp.dot(xb, wgu_ref[:, f + lo:f + hi], preferred_element_type=_F32)
        act = (gate * jax.nn.sigmoid(gate)) * up
        acc = acc + jnp.dot(act.astype(_BF16), wd_ref[lo:hi, :], preferred_element_type=_F32)
    y = x + 0.5 * acc
    if final:
        y = _rms(y, gf_ref[...])
    o_ref[...] = y


def _ffn(x2d, norm_g, wgu, wd, final_g=None, tm=512):
    t, d = x2d.shape
    final = final_g is not None
    in_specs = [pl.BlockSpec((tm, d), lambda i: (i, 0)), _const_spec((1, d)),
                _const_spec(wgu.shape), _const_spec(wd.shape)]
    args = [x2d, norm_g.reshape(1, d), wgu, wd]
    if final:
        in_specs.append(_const_spec((1, d)))
        args.append(final_g.reshape(1, d))
    return pl.pallas_call(
        functools.partial(_ffn_kernel, final=final),
        grid=(t // tm,), in_specs=in_specs,
        out_specs=pl.BlockSpec((tm, d), lambda i: (i, 0)),
        out_shape=jax.ShapeDtypeStruct((t, d), _F32),
        compiler_params=_params("parallel"), name="ffn")(*args)


def _ffn_weights(w_gate_up, w_down):
    return w_gate_up.astype(_BF16), w_down.astype(_BF16)


def _matmul_residual_kernel(a_ref, w_ref, x_ref, o_ref):
    o_ref[...] = x_ref[...] + jnp.dot(a_ref[...], w_ref[...], preferred_element_type=_F32)


def _matmul_residual(a2d, w_bf16, x2d, tm=1024):
    t, d = x2d.shape
    k = a2d.shape[1]
    return pl.pallas_call(
        _matmul_residual_kernel, grid=(t // tm,),
        in_specs=[pl.BlockSpec((tm, k), lambda i: (i, 0)), _const_spec(w_bf16.shape),
                  pl.BlockSpec((tm, d), lambda i: (i, 0))],
        out_specs=pl.BlockSpec((tm, d), lambda i: (i, 0)),
        out_shape=jax.ShapeDtypeStruct((t, d), _F32),
        compiler_params=_params("parallel"), name="matmul_residual")(a2d, w_bf16, x2d)


def _t5_bucket(rel):
    nb = NUM_BUCKETS // 2
    max_exact = nb // 2
    ret = (rel > 0).astype(np.int32) * nb
    n = np.abs(rel)
    large = max_exact + (np.log(np.maximum(n, 1) / max_exact)
                         / math.log(MAX_DISTANCE / max_exact) * (nb - max_exact)).astype(np.int32)
    large = np.minimum(large, nb - 1)
    return ret + np.where(n < max_exact, n, large)


def _attn_bucket_tables(dilation):
    delta = np.arange(_TK)[None, :] - _HALF_W - np.arange(_TQ)[:, None]
    bucket = _t5_bucket(delta * dilation)
    band = np.abs(delta) <= _HALF_W
    kk = np.arange(_TK)[None, :]
    tables = []
    for variant in range(4):
        ok = band
        if variant & 1:
            ok = ok & (kk >= _HALF_W)
        if variant & 2:
            ok = ok & (kk < _TQ + _HALF_W)
        tables.append(np.where(ok, bucket, -1))
    return jnp.asarray(np.stack(tables), jnp.int32)


def _bias_tiles_kernel(rb_ref, bucket_ref, o_ref):
    hp = pl.program_id(1)
    bucket = bucket_ref[...]
    for hh in range(2):
        acc = jnp.full((_TQ, _TK), _NEG, _F32)
        for bkt in range(NUM_BUCKETS):
            acc = jnp.where(bucket == bkt, rb_ref[2 * hp + hh, bkt] * _LOG2E, acc)
        o_ref[hh * _TQ:(hh + 1) * _TQ, :] = acc


def _attn_bias_tiles(rel_bias_g, dilation):
    tables = _attn_bucket_tables(dilation)
    npair = HEADS_PER_GROUP // 2
    return pl.pallas_call(
        _bias_tiles_kernel, grid=(4, npair),
        in_specs=[pl.BlockSpec(memory_space=pltpu.SMEM),
                  pl.BlockSpec((None, _TQ, _TK), lambda v, hp: (v, 0, 0))],
        out_specs=pl.BlockSpec((None, None, 2 * _TQ, _TK), lambda v, hp: (v, hp, 0, 0)),
        out_shape=jax.ShapeDtypeStruct((4, npair, 2 * _TQ, _TK), _F32),
        compiler_params=_params("parallel", "parallel"), name="attn_bias_tiles")(
            rel_bias_g.astype(_F32).T, tables)


def _attn_kernel(q_ref, kp_ref, kc_ref, kn_ref, vp_ref, vc_ref, vn_ref, bias_ref,
                 o_ref, stat_ref, *, nr, nsub):
    i = pl.program_id(2)
    is_first = jnp.where(i == 0, 1, 0)
    is_last = jnp.where(i == pl.num_programs(2) - 1, 2, 0)
    lane = lax.broadcasted_iota(jnp.int32, (1, _LANES), 1)
    low = lane < HEAD_DIM
    zero = jnp.zeros((), _BF16)
    cur_rows = nsub * _TQ

    def window(prev_ref, cur_ref, next_ref, ri, sub, sl):
        start, end = sub * _TQ - _HALF_W, (sub + 1) * _TQ + _HALF_W
        pieces = []
        if start < 0:
            pieces.append(prev_ref[ri, _TQ - _HALF_W:, sl])
        pieces.append(cur_ref[ri, max(start, 0):min(end, cur_rows), sl])
        if end > cur_rows:
            pieces.append(next_ref[ri, :_HALF_W, sl])
        return jnp.concatenate(pieces, axis=0)

    for ri in range(nr):
        for sub in range(nsub):
            rows = slice(sub * _TQ, (sub + 1) * _TQ)
            variant = (is_first if sub == 0 else 0) + (is_last if sub == nsub - 1 else 0)
            stat = jnp.zeros((_TQ, _LANES), _F32)
            for hp in range(HEADS_PER_GROUP // 2):
                sl = slice(hp * _LANES, (hp + 1) * _LANES)
                q2 = q_ref[ri, rows, sl]
                qq = jnp.concatenate([jnp.where(low, q2, zero), jnp.where(low, zero, q2)], axis=0)
                k2 = window(kp_ref, kc_ref, kn_ref, ri, sub, sl)
                v2 = window(vp_ref, vc_ref, vn_ref, ri, sub, sl)
                s = lax.dot_general(qq, k2, (((1,), (1,)), ((), ())),
                                    preferred_element_type=_F32)
                s = s + bias_ref[variant, hp]
                m = jnp.max(s, axis=-1, keepdims=True)
                p = jnp.exp2(s - m)
                l = jnp.sum(p, axis=-1, keepdims=True)
                pv = jnp.dot(p.astype(_BF16), v2, preferred_element_type=_F32)
                o_ref[ri, rows, sl] = jnp.where(low, pv[:_TQ], pv[_TQ:]).astype(o_ref.dtype)
                h0, h1 = 2 * hp, 2 * hp + 1
                stat = jnp.where(lane == h0, m[:_TQ], jnp.where(lane == h1, m[_TQ:], stat))
                stat = jnp.where(lane == HEADS_PER_GROUP + h0, l[:_TQ],
                                 jnp.where(lane == HEADS_PER_GROUP + h1, l[_TQ:], stat))
            stat_ref[ri, rows, :] = stat


def _qkv_group_kernel(x_ref, g_ref, w_ref, o_ref, xn_ref, xp_ref, *, d, tm):
    rows = tm // d
    xn = _rms(x_ref[...], g_ref[...])
    if d == 1:
        xp_ref[...] = xn.astype(_BF16)
    else:
        for c in range(D_MODEL // _LANES):
            xn_ref[c] = xn[:, c * _LANES:(c + 1) * _LANES]
        for r in range(d):
            for c in range(D_MODEL // _LANES):
                xp_ref[r * rows:(r + 1) * rows, c * _LANES:(c + 1) * _LANES] = (
                    xn_ref[c, pl.ds(r, rows, stride=d), :].astype(_BF16))
    for j in range(3):
        cs = slice(j * D_MODEL, (j + 1) * D_MODEL)
        y = jnp.dot(xp_ref[...], w_ref[:, cs], preferred_element_type=_F32).astype(o_ref.dtype)
        for r in range(d):
            o_ref[r, :, cs] = y[r * rows:(r + 1) * rows]


def _qkv_group(x, norm_g, w_g_bf16, dilation, tm=512):
    b, s, dm = x.shape
    d = dilation
    n3 = w_g_bf16.shape[1]
    return pl.pallas_call(
        functools.partial(_qkv_group_kernel, d=d, tm=tm), grid=(b, s // tm),
        in_specs=[pl.BlockSpec((None, tm, dm), lambda bi, i: (bi, i, 0)), _const_spec((1, dm)),
                  _const_spec(w_g_bf16.shape)],
        out_specs=pl.BlockSpec((None, d, tm // d, n3), lambda bi, i: (bi, 0, i, 0)),
        out_shape=jax.ShapeDtypeStruct((b, d, s // d, n3), _BF16),
        scratch_shapes=[pltpu.VMEM((dm // _LANES, tm, _LANES), _F32), pltpu.VMEM((tm, dm), _BF16)],
        compiler_params=_params("parallel", "parallel"), name=f"qkv_d{d}")(
            x, norm_g.reshape(1, dm), w_g_bf16)


def _group_attention(qkv, bias_tiles):
    b, d, lc, _ = qkv.shape
    n_tq = lc // _TQ
    nsub = min(_ATTN_MAX_SUBTILES, n_tq)
    nr = min(d, _ATTN_MAX_SUBTILES // nsub)
    rows = nsub * _TQ

    def cur(which):
        return pl.BlockSpec((None, nr, rows, D_MODEL), lambda bi, r, i: (bi, r, i, which))

    def prev(which):
        return pl.BlockSpec((None, nr, _TQ, D_MODEL),
                            lambda bi, r, i: (bi, r, jnp.maximum(i * nsub - 1, 0), which))

    def nxt(which):
        return pl.BlockSpec((None, nr, _TQ, D_MODEL),
                            lambda bi, r, i: (bi, r, jnp.minimum((i + 1) * nsub, n_tq - 1), which))

    return pl.pallas_call(
        functools.partial(_attn_kernel, nr=nr, nsub=nsub), grid=(b, d // nr, n_tq // nsub),
        in_specs=[cur(0), prev(1), cur(1), nxt(1), prev(2), cur(2), nxt(2),
                  _const_spec(bias_tiles.shape)],
        out_specs=[pl.BlockSpec((None, nr, rows, D_MODEL), lambda bi, r, i: (bi, r, i, 0)),
                   pl.BlockSpec((None, nr, rows, _LANES), lambda bi, r, i: (bi, r, i, 0))],
        out_shape=[jax.ShapeDtypeStruct((b, d, lc, D_MODEL), _BF16),
                   jax.ShapeDtypeStruct((b, d, lc, _LANES), _F32)],
        compiler_params=_params("parallel", "parallel", "arbitrary"), name=f"attn_d{d}")(
            qkv, qkv, qkv, qkv, qkv, qkv, qkv, bias_tiles)


def _merge_kernel(o0_ref, o1_ref, o2_ref, l0_ref, l1_ref, l2_ref, x_ref, e_ref, wo_ref, out_ref,
                  of_ref, lf_ref, *, tm):
    def natural(o_ref, l_ref, slot):
        d = o_ref.shape[0]
        if d == 1:
            return o_ref[0].astype(_F32), l_ref[0]
        rows = tm // d
        nblk = D_MODEL // _LANES
        for r in range(d):
            o_r = o_ref[r].astype(_F32)
            for c in range(nblk):
                of_ref[slot, c, pl.ds(r, rows, stride=d), :] = o_r[:, c * _LANES:(c + 1) * _LANES]
            lf_ref[slot, pl.ds(r, rows, stride=d), :] = l_ref[r]
        return jnp.concatenate([of_ref[slot, c] for c in range(nblk)], axis=1), lf_ref[slot]

    o0, s0 = natural(o0_ref, l0_ref, 0)
    o1, s1 = natural(o1_ref, l1_ref, 0)
    o2, s2 = natural(o2_ref, l2_ref, 1)
    head = lax.broadcasted_iota(jnp.int32, (1, _LANES), 1) < HEADS_PER_GROUP
    ms = [jnp.where(head, st, 0.0) for st in (s0, s1, s2)]
    ls = [jnp.where(head, pltpu.roll(st, _LANES - HEADS_PER_GROUP, 1), 1.0) for st in (s0, s1, s2)]
    m = jnp.maximum(jnp.maximum(ms[0], ms[1]), ms[2])
    a0, a1, a2 = jnp.exp2(ms[0] - m), jnp.exp2(ms[1] - m), jnp.exp2(ms[2] - m)
    inv = 1.0 / (a0 * ls[0] + a1 * ls[1] + a2 * ls[2])

    def expand(a):
        al = a * inv
        hi = al.astype(_BF16)
        lo = (al - hi.astype(_F32)).astype(_BF16)
        return jnp.dot(jnp.concatenate([hi, lo], axis=1), e_ref[...], preferred_element_type=_F32)

    merged = expand(a0) * o0 + expand(a1) * o1 + expand(a2) * o2
    out_ref[...] = x_ref[...] + jnp.dot(merged.astype(_BF16), wo_ref[...],
                                        preferred_element_type=_F32)


def _head_expand_matrix():
    e = np.zeros((2 * _LANES, D_MODEL), np.float32)
    for h in range(HEADS_PER_GROUP):
        e[h, h * HEAD_DIM:(h + 1) * HEAD_DIM] = 1.0
        e[_LANES + h, h * HEAD_DIM:(h + 1) * HEAD_DIM] = 1.0
    return jnp.asarray(e, _BF16)


def _merge(os_, stats, x, wo_bf16, tm=512):
    b, s, dm = x.shape
    e = _head_expand_matrix()

    def cls(a):
        d, w = a.shape[1], a.shape[3]
        return pl.BlockSpec((None, d, tm // d, w), lambda bi, i: (bi, 0, i, 0))

    row = pl.BlockSpec((None, tm, dm), lambda bi, i: (bi, i, 0))
    return pl.pallas_call(
        functools.partial(_merge_kernel, tm=tm), grid=(b, s // tm),
        in_specs=[cls(a) for a in os_] + [cls(a) for a in stats]
        + [row, _const_spec(e.shape), _const_spec(wo_bf16.shape)],
        out_specs=row, out_shape=jax.ShapeDtypeStruct((b, s, dm), _F32),
        scratch_shapes=[pltpu.VMEM((2, dm // _LANES, tm, _LANES), _F32),
                        pltpu.VMEM((2, tm, _LANES), _F32)],
        compiler_params=_params("parallel", "parallel"), name="attn_merge")(
            *os_, *stats, x, e, wo_bf16)


def _dilated_attention_layer(x, norm_g, w_qkv, w_o, rel_bias):
    n3 = 3 * D_MODEL
    os_, stats = [], []
    for g, (window, dil) in enumerate(DILATED_GROUPS):
        assert (window // 2) // dil == _HALF_W
        col_scale = jnp.where(jnp.arange(n3) < D_MODEL, _LOG2E * HEAD_DIM ** -0.5, 1.0).astype(_F32)
        w_g = (w_qkv[:, g * n3:(g + 1) * n3].astype(_F32) * col_scale).astype(_BF16)
        qkv = _qkv_group(x, norm_g, w_g, dil)
        bias = _attn_bias_tiles(rel_bias[:, g * HEADS_PER_GROUP:(g + 1) * HEADS_PER_GROUP], dil)
        o, stat = _group_attention(qkv, bias)
        os_.append(o)
        stats.append(stat)
    return _merge(os_, stats, x, w_o.astype(_BF16))


def _hyena_in_kernel(xp_ref, x_ref, xn_ref, g_ref, w_ref, cw_ref, cb_ref,
                     x1_ref, x2_ref, v_ref, *, tm):
    i = pl.program_id(1)
    last = pl.num_programs(1) - 1
    keep_prev = jnp.where(i > 0, 1.0, 0.0)
    keep_next = jnp.where(i < last, 1.0, 0.0)
    xe = jnp.concatenate([xp_ref[...] * keep_prev, x_ref[...], xn_ref[...] * keep_next], axis=0)
    xb = _rms(xe, g_ref[...]).astype(_BF16)
    n = tm + 16
    for j, o_ref in enumerate((x1_ref, x2_ref, v_ref)):
        cs = slice(j * D_MODEL, (j + 1) * D_MODEL)
        u = jnp.dot(xb, w_ref[:, cs], preferred_element_type=_F32)
        u_prev = pltpu.roll(u, 1, 0)[8:tm + 8]
        u_next = pltpu.roll(u, n - 1, 0)[8:tm + 8]
        cw = cw_ref[:, cs]
        y = (u_prev * cw[0:1] + u[8:tm + 8] * cw[1:2] + u_next * cw[2:3]) + cb_ref[:, cs]
        o_ref[...] = y.astype(o_ref.dtype)


def _hyena_in(x, norm_g, w_in_bf16, conv_w, conv_b, tm=512):
    b, s, d = x.shape
    n3 = w_in_bf16.shape[1]
    nblk8 = s // 8
    r8 = tm // 8
    main = pl.BlockSpec((None, tm, d), lambda bi, i: (bi, i, 0))
    prev = pl.BlockSpec((None, 8, d), lambda bi, i: (bi, jnp.maximum(i * r8 - 1, 0), 0))
    nxt = pl.BlockSpec((None, 8, d), lambda bi, i: (bi, jnp.minimum((i + 1) * r8, nblk8 - 1), 0))
    out = jax.ShapeDtypeStruct((b, s, d), _BF16)
    return pl.pallas_call(
        functools.partial(_hyena_in_kernel, tm=tm), grid=(b, s // tm),
        in_specs=[prev, main, nxt, _const_spec((1, d)), _const_spec(w_in_bf16.shape),
                  _const_spec((3, n3)), _const_spec((1, n3))],
        out_specs=[main, main, main], out_shape=[out, out, out],
        compiler_params=_params("parallel", "arbitrary"), name="hyena_in")(
            x, x, x, norm_g.reshape(1, d), w_in_bf16, conv_w.astype(_F32),
            conv_b.astype(_F32).reshape(1, n3))


def _filter_kernel(fr_ref, w1_ref, b1_ref, w2_ref, b2_ref, w3_ref, b3_ref, fq_ref, w4h_ref, w4l_ref,
                   dl_ref, k0_ref, k1_ref, *, seq_len, tr):
    hi = lax.Precision.HIGHEST
    r = pl.program_id(0) * tr + lax.broadcasted_iota(jnp.int32, (tr, 1), 0)
    pos = jnp.abs(r - seq_len).astype(_F32)
    t = pos / (seq_len - 1.0)
    wpos = (2.0 * math.pi) * pos / seq_len
    lane = lax.broadcasted_iota(jnp.int32, (1, _LANES), 1)
    bands = (HYENA_EMB_DIM - 1) // 2
    ang = fr_ref[...] * wpos
    z = jnp.where(lane == 0, t,
                  jnp.where(lane <= bands, jnp.cos(ang),
                            jnp.where(lane <= 2 * bands, -jnp.sin(ang), 0.0)))
    fq = fq_ref[...]
    h = jnp.sin(fq * (jnp.dot(z, w1_ref[...], precision=hi, preferred_element_type=_F32) + b1_ref[...]))
    h = jnp.sin(fq * (jnp.dot(h, w2_ref[...], precision=hi, preferred_element_type=_F32) + b2_ref[...]))
    h = jnp.sin(fq * (jnp.dot(h, w3_ref[...], precision=hi, preferred_element_type=_F32) + b3_ref[...]))
    decay = jnp.exp(-t * jnp.abs(dl_ref[...]))
    live = r > 0
    h_hi = h.astype(_BF16)
    h_lo = (h - h_hi.astype(_F32)).astype(_BF16)
    for o, k_ref in enumerate((k0_ref, k1_ref)):
        cs = slice(o * D_MODEL, (o + 1) * D_MODEL)
        w_hi = w4h_ref[:, cs]
        y = jnp.dot(h_hi, w_hi, preferred_element_type=_F32)
        y = y + jnp.dot(h_hi, w4l_ref[:, cs], preferred_element_type=_F32)
        y = y + jnp.dot(h_lo, w_hi, preferred_element_type=_F32)
        k_ref[...] = jnp.where(live, y * decay, 0.0)


def _hyena_two_sided_filters(seq_len, w1, b1, w2, b2, w3, b3, freq, w4, tr=512):
    fw = HYENA_FILTER_WIDTH
    n_tiles = 2 * seq_len // tr
    w4d = w4.astype(_F32).reshape(fw, 2, 2, D_MODEL).transpose(2, 0, 1, 3).reshape(2, fw, 2 * D_MODEL)
    w4_hi = w4d.astype(_BF16)
    w4_lo = (w4d - w4_hi.astype(_F32)).astype(_BF16)
    w4_spec = pl.BlockSpec((None, fw, 2 * D_MODEL),
                           lambda i: (jnp.where(i >= n_tiles // 2, 0, 1), 0, 0))
    bands = (HYENA_EMB_DIM - 1) // 2
    fr = jnp.linspace(1e-4, bands - 1, bands, dtype=_F32)
    fr_lanes = jnp.zeros((1, _LANES), _F32).at[0, 1:1 + bands].set(fr).at[0, 1 + bands:1 + 2 * bands].set(fr)
    w1p = jnp.zeros((_LANES, fw), _F32).at[:HYENA_EMB_DIM].set(w1.astype(_F32))
    max_decay = math.log(DECAY_TARGET) / FAST_DECAY_PCT
    min_decay = math.log(DECAY_TARGET) / SLOW_DECAY_PCT
    deltas = jnp.linspace(min_decay, max_decay, D_MODEL, dtype=_F32).reshape(1, D_MODEL)
    vec = lambda a: a.astype(_F32).reshape(1, fw)
    out = jax.ShapeDtypeStruct((2 * seq_len, D_MODEL), _F32)
    cs = _const_spec
    return pl.pallas_call(
        functools.partial(_filter_kernel, seq_len=seq_len, tr=tr), grid=(n_tiles,),
        in_specs=[cs((1, _LANES)), cs((_LANES, fw)), cs((1, fw)), cs((fw, fw)), cs((1, fw)),
                  cs((fw, fw)), cs((1, fw)), cs((1, fw)), w4_spec, w4_spec, cs((1, D_MODEL))],
        out_specs=[pl.BlockSpec((tr, D_MODEL), lambda i: (i, 0))] * 2, out_shape=[out, out],
        compiler_params=_params("parallel"), name="hyena_filter")(
            fr_lanes, w1p, vec(b1), w2.astype(_F32), vec(b2), w3.astype(_F32), vec(b3), vec(freq),
            w4_hi, w4_lo, deltas)


def _dft_matrices(p):
    n = 2 * p
    k = np.arange(p)[:, None].astype(np.float64)
    s = np.arange(p)[None, :].astype(np.float64)
    ang = np.pi * k * s / p
    fre = np.cos(ang)
    fim = -np.sin(ang)
    fim[0, :] = np.cos(np.pi * s[0])
    fwd_half = np.concatenate([fre, fim], axis=0)
    tt = (p + np.arange(p))[:, None].astype(np.float64)
    kk = np.arange(p)[None, :].astype(np.float64)
    ang2 = np.pi * kk * tt / p
    are = 2.0 * np.cos(ang2) / n
    are[:, 0] = 1.0 / n
    aim = -2.0 * np.sin(ang2) / n
    aim[:, 0] = np.cos(np.pi * tt[:, 0]) / n
    inv = np.concatenate([are, aim], axis=1)
    return jnp.asarray(fwd_half, _BF16), jnp.asarray(inv, _BF16)


def _filter_spectrum_kernel(g_ref, f_ref, o_ref, prev_ref):
    q = pl.program_id(1)
    a = jnp.dot(f_ref[...], g_ref[...].astype(_BF16), preferred_element_type=_F32)

    @pl.when(q > 0)
    def _():
        odd = (lax.broadcasted_iota(jnp.int32, (a.shape[0], 1), 0) & 1) == 1
        o_ref[...] = (prev_ref[...] + jnp.where(odd, -a, a)).astype(o_ref.dtype)

    prev_ref[...] = a


def _filter_spectrum(kt, fwd_half, p, tc=256):
    rows, d = kt.shape
    nblk = rows // p
    return pl.pallas_call(
        _filter_spectrum_kernel, grid=(d // tc, nblk),
        in_specs=[pl.BlockSpec((p, tc), lambda c, q: (q, c)), _const_spec(fwd_half.shape)],
        out_specs=pl.BlockSpec((None, 2 * p, tc), lambda c, q: (jnp.maximum(q - 1, 0), 0, c)),
        out_shape=jax.ShapeDtypeStruct((nblk - 1, 2 * p, d), _SPECTRUM_DTYPE),
        scratch_shapes=[pltpu.VMEM((2 * p, tc), _F32)],
        compiler_params=_params("parallel", "arbitrary"), name="hyena_filter_spectrum")(kt, fwd_half)


def _long_conv_kernel(gate_ref, z_ref, kf_ref, skip_ref, ff_ref, ai_ref, o_ref, zf_ref, yfa_ref,
                      yfb_ref, *, p, nb, tc, ne):
    ch = _CONV_ROWS
    slots = zf_ref.shape[0]
    skip = skip_ref[...]
    first_row = lax.broadcasted_iota(jnp.int32, (ch, 1), 0) == 0

    def forward(e, i):
        zb = z_ref[e, pl.ds(pl.multiple_of(i * p, p), p), :]
        zf_ref[e % slots, i] = jnp.dot(ff_ref[...], zb,
                                       preferred_element_type=_F32).astype(zf_ref.dtype)

    def spectrum(e, j, yf_ref):
        for rc in range(p // ch):
            re = slice(rc * ch, (rc + 1) * ch)
            im = slice(p + rc * ch, p + (rc + 1) * ch)
            a = bm = cc = None
            for i0 in range(0, nb, _SPECTRUM_GROUP):
                ga = jnp.zeros((ch, tc), zf_ref.dtype)
                gb = jnp.zeros((ch, tc), zf_ref.dtype)
                gc = jnp.zeros((ch, tc), zf_ref.dtype)
                for i in range(i0, min(i0 + _SPECTRUM_GROUP, nb)):
                    q = j - i + (nb - 1)
                    zre, zim = zf_ref[e % slots, i, re, :], zf_ref[e % slots, i, im, :]
                    kre, kim = kf_ref[q, re, :], kf_ref[q, im, :]
                    ga = ga + zre * kre
                    gb = gb + zim * kim
                    gc = gc + (zre * kim + zim * kre)
                if nb > _SPECTRUM_GROUP:
                    ga, gb, gc = ga.astype(_F32), gb.astype(_F32), gc.astype(_F32)
                a = ga if a is None else a + ga
                bm = gb if bm is None else bm + gb
                cc = gc if cc is None else cc + gc
            if rc == 0:
                yf_ref[re, :] = jnp.where(first_row, a, a - bm).astype(_BF16)
                yf_ref[im, :] = jnp.where(first_row, bm, cc).astype(_BF16)
            else:
                yf_ref[re, :] = (a - bm).astype(_BF16)
                yf_ref[im, :] = cc.astype(_BF16)

    def finish(e, j, yf_ref):
        y = jnp.dot(ai_ref[...], yf_ref[...], preferred_element_type=_F32)
        rows = pl.ds(pl.multiple_of(j * p, p), p)
        zt = z_ref[e, rows, :].astype(_F32)
        gt = gate_ref[e, rows, :].astype(_F32)
        o_ref[e, rows, :] = (gt * (y + zt * skip)).astype(o_ref.dtype)

    yf = (yfa_ref, yfb_ref)
    if nb <= _CONV_STATIC_BLOCKS:
        for i in range(nb):
            forward(0, i)
        pending = None
        for e in range(ne):
            for j in range(nb):
                if pending is not None:
                    finish(*pending)
                if e + 1 < ne:
                    forward(e + 1, j)
                spectrum(e, j, yf[j % 2])
                pending = (e, j, yf[j % 2])
        finish(*pending)
    else:
        assert ne == 1 and nb % 2 == 0
        for i in range(nb):
            forward(0, i)
        spectrum(0, 0, yfa_ref)

        def pair_body(k, carry):
            finish(0, 2 * k, yfa_ref)
            spectrum(0, 2 * k + 1, yfb_ref)
            finish(0, 2 * k + 1, yfb_ref)
            spectrum(0, 2 * k + 2, yfa_ref)
            return carry

        lax.fori_loop(0, nb // 2 - 1, pair_body, 0)
        finish(0, nb - 2, yfa_ref)
        spectrum(0, nb - 1, yfb_ref)
        finish(0, nb - 1, yfb_ref)


def _long_conv(gate, z, kf, skip_o, fwd_half, inv, p):
    b, l, d = z.shape
    nb = l // p
    ne = 2 if (nb <= _CONV_STATIC_BLOCKS and b % 2 == 0) else 1
    slots = min(ne, 2)
    spectra = (2 * nb - 1 + slots * nb) * 2 * p * jnp.dtype(_SPECTRUM_DTYPE).itemsize
    seq_bytes = ne * l * 2
    dft_bytes = 2 * (2 * p * p * 2)
    budget = _V7X_VMEM_LIMIT_BYTES * 9 // 10
    tc, io_buffers = 128, 2
    for cand_tc, cand_buffers in ((256, 2), (256, 1)):
        if cand_tc * (spectra + (2 + 2 * cand_buffers) * seq_bytes) + dft_bytes <= budget:
            tc, io_buffers = cand_tc, cand_buffers
            break
    io_mode = {} if io_buffers == 2 else {"pipeline_mode": pl.Buffered(1)}
    seq = pl.BlockSpec((ne, l, tc), lambda c, bi: (bi, 0, c))
    seq_io = pl.BlockSpec((ne, l, tc), lambda c, bi: (bi, 0, c), **io_mode)
    kf_spec = pl.BlockSpec((2 * nb - 1, 2 * p, tc), lambda c, bi: (0, 0, c),
                           pipeline_mode=pl.Buffered(1))
    return pl.pallas_call(
        functools.partial(_long_conv_kernel, p=p, nb=nb, tc=tc, ne=ne), grid=(d // tc, b // ne),
        in_specs=[seq_io, seq, kf_spec, pl.BlockSpec((1, tc), lambda c, bi: (0, c)),
                  _const_spec(fwd_half.shape), _const_spec(inv.shape)],
        out_specs=seq_io, out_shape=jax.ShapeDtypeStruct((b, l, d), _BF16),
        scratch_shapes=[pltpu.VMEM((slots, nb, 2 * p, tc), _SPECTRUM_DTYPE),
                        pltpu.VMEM((2 * p, tc), _BF16),
                        pltpu.VMEM((2 * p, tc), _BF16)],
        compiler_params=_params("parallel", "arbitrary"), name="hyena_long_conv")(
            gate, z, kf, skip_o.astype(_F32).reshape(1, d), fwd_half, inv)


def _hyena_layer(x, norm_g, w_in, conv_w, conv_b, w1, b1, w2, b2, w3, b3, freq, w4, skip, w_out):
    b, l, d = x.shape
    p = min(l, max(_CONV_P, l // 8))
    x1, x2, v = _hyena_in(x, norm_g, w_in.astype(_BF16), conv_w, conv_b)
    k0, k1 = _hyena_two_sided_filters(l, w1, b1, w2, b2, w3, b3, freq, w4)
    fwd_half, inv = _dft_matrices(p)
    z = v
    for o, (gate, kt) in enumerate(((x1, k0), (x2, k1))):
        kf = _filter_spectrum(kt, fwd_half, p)
        z = _long_conv(gate, z, kf, skip[o], fwd_half, inv, p)
    y = _matmul_residual(z.reshape(b * l, d), w_out.astype(_BF16), x.reshape(b * l, d))
    return y.reshape(b, l, d)


def _trunk(x, rel_bias, ffn1_norm, ffn1_w, mixer_norm, attn_w_qkv, attn_w_o, hy, ffn2_norm, ffn2_w,
           final_norm):
    b, s, d = x.shape
    depth = ffn1_norm.shape[0]
    flat = lambda a: a.reshape(b * s, d)
    for i in range(depth):
        x = _ffn(flat(x), ffn1_norm[i], *ffn1_w[i]).reshape(b, s, d)
        j = i // 2
        if i % 2 == 0:
            x = _dilated_attention_layer(x, mixer_norm[i], attn_w_qkv[j], attn_w_o[j], rel_bias)
        else:
            x = _hyena_layer(x, mixer_norm[i], *[a[j] for a in hy])
        fin = final_norm if i == depth - 1 else None
        x = _ffn(flat(x), ffn2_norm[i], *ffn2_w[i], final_g=fin).reshape(b, s, d)
    return x


def kernel(x_prompt, x_sample, rel_bias, ffn1_norm, ffn1_w_gate_up, ffn1_w_down, mixer_norm, attn_w_qkv, attn_w_o, hyena_w_in, hyena_conv_w, hyena_conv_b, hyena_filt_w1, hyena_filt_b1, hyena_filt_w2, hyena_filt_b2, hyena_filt_w3, hyena_filt_b3, hyena_filt_freq, hyena_filt_w4, hyena_skip, hyena_w_out, ffn2_norm, ffn2_w_gate_up, ffn2_w_down, final_norm):
    depth = ffn1_norm.shape[0]
    ffn1_w = [_ffn_weights(ffn1_w_gate_up[i], ffn1_w_down[i]) for i in range(depth)]
    ffn2_w = [_ffn_weights(ffn2_w_gate_up[i], ffn2_w_down[i]) for i in range(depth)]
    hy = (hyena_w_in, hyena_conv_w, hyena_conv_b, hyena_filt_w1, hyena_filt_b1, hyena_filt_w2,
          hyena_filt_b2, hyena_filt_w3, hyena_filt_b3, hyena_filt_freq, hyena_filt_w4, hyena_skip,
          hyena_w_out)
    args = (rel_bias, ffn1_norm, ffn1_w, mixer_norm, attn_w_qkv, attn_w_o, hy, ffn2_norm, ffn2_w,
            final_norm)
    return (_trunk(x_prompt, *args), _trunk(x_sample, *args))
```

```python
import functools
import math

import numpy as np
import jax
import jax.numpy as jnp
from jax import lax
from jax.experimental import pallas as pl
from jax.experimental.pallas import tpu as pltpu

D_MODEL = 1024
HEAD_DIM = 64
HEADS_PER_GROUP = D_MODEL // HEAD_DIM
DILATED_GROUPS = ((128, 1), (512, 4), (2048, 16))
N_GROUPS = len(DILATED_GROUPS)
NUM_BUCKETS = 32
MAX_DISTANCE = 1024
HYENA_EMB_DIM = 33
HYENA_FILTER_WIDTH = 64
FAST_DECAY_PCT = 0.3
SLOW_DECAY_PCT = 1.5
DECAY_TARGET = 1e-2
D_FF = 2816
RMS_EPS = 1e-6

_F32 = jnp.float32
_BF16 = jnp.bfloat16
_V7X_VMEM_LIMIT_BYTES = 58 * 1024 * 1024
_LANES = 128
_NEG = -1e30
_LOG2E = math.log2(math.e)

_TQ = 128
_HALF_W = 64
_TK = _TQ + 2 * _HALF_W
_ATTN_MAX_SUBTILES = 4
_MXU_DIM = 256
_CONV_P = 512
_CONV_ROWS = 32
_SPECTRUM_DTYPE = jnp.bfloat16
_SPECTRUM_GROUP = 4
_CONV_STATIC_BLOCKS = 4


def _params(*sem):
    return pltpu.CompilerParams(dimension_semantics=sem, vmem_limit_bytes=_V7X_VMEM_LIMIT_BYTES)


def _const_spec(shape):
    nd = len(shape)
    return pl.BlockSpec(shape, lambda *_: (0,) * nd, pipeline_mode=pl.Buffered(1))


def _rms(x, g):
    y = x * lax.rsqrt(jnp.mean(x * x, axis=-1, keepdims=True) + RMS_EPS)
    return y * g


def _merge_rows(o_refs, s_refs, e_ref, of_ref, lf_ref, h, sub):
    nblk = D_MODEL // _LANES

    def natural(o_ref, s_ref, slot):
        d = o_ref.shape[0]
        n = sub // d
        cls = slice(h * n, (h + 1) * n)
        if d == 1:
            return o_ref[0, cls, :].astype(_F32), s_ref[0, cls, :]
        for r in range(d):
            o_r = o_ref[r, cls, :].astype(_F32)
            for c in range(nblk):
                of_ref[slot, c, pl.ds(r, n, stride=d), :] = o_r[:, c * _LANES:(c + 1) * _LANES]
            lf_ref[slot, pl.ds(r, n, stride=d), :] = s_ref[r, cls, :]
        return jnp.concatenate([of_ref[slot, c] for c in range(nblk)], axis=1), lf_ref[slot]

    outs, stats = [], []
    for g, (o_ref, s_ref) in enumerate(zip(o_refs, s_refs)):
        o, st = natural(o_ref, s_ref, 2 * h + (g % 2))
        outs.append(o)
        stats.append(st)
    head = lax.broadcasted_iota(jnp.int32, (1, _LANES), 1) < HEADS_PER_GROUP
    ms = [jnp.where(head, st, 0.0) for st in stats]
    ls = [jnp.where(head, pltpu.roll(st, _LANES - HEADS_PER_GROUP, 1), 1.0) for st in stats]
    m = functools.reduce(jnp.maximum, ms)
    ws = [jnp.exp2(mg - m) for mg in ms]
    inv = 1.0 / sum(w * l for w, l in zip(ws, ls))
    merged = None
    for w, o in zip(ws, outs):
        al = w * inv
        hi = al.astype(_BF16)
        lo = (al - hi.astype(_F32)).astype(_BF16)
        term = jnp.dot(jnp.concatenate([hi, lo], axis=1), e_ref[...], preferred_element_type=_F32) * o
        merged = term if merged is None else merged + term
    return merged


def _ffn_kernel(*refs, pre, final, sub):
    refs = list(refs)
    x_ref, g_ref, wgu_ref, wd_ref = refs[:4]
    extra = refs[4:]
    if pre == "matmul":
        a_ref, wpre_ref = extra[:2]
        extra = extra[2:]
    elif pre == "merge":
        o_refs, s_refs, (e_ref, wpre_ref) = extra[0:3], extra[3:6], extra[6:8]
        extra = extra[8:]
    gf_ref = extra.pop(0) if final else None
    o_ref = extra.pop(0)
    f = wd_ref.shape[0]
    n_tiles = f // _MXU_DIM
    bounds = [0, (n_tiles + 1) // 2 * _MXU_DIM, f]
    for h in range(x_ref.shape[0] // sub):
        rows = slice(h * sub, (h + 1) * sub)
        x = x_ref[rows, :]
        if pre == "matmul":
            x = x + jnp.dot(a_ref[rows, :], wpre_ref[...], preferred_element_type=_F32)
        elif pre == "merge":
            merged = _merge_rows(o_refs, s_refs, e_ref, extra[0], extra[1], h, sub)
            x = x + jnp.dot(merged.astype(_BF16), wpre_ref[...], preferred_element_type=_F32)
        xb = _rms(x, g_ref[...]).astype(_BF16)
        acc = jnp.zeros(x.shape, _F32)
        for lo, hi in zip(bounds[:-1], bounds[1:]):
            gate = jnp.dot(xb, wgu_ref[:, lo:hi], preferred_element_type=_F32)
            up = jnp.dot(xb, wgu_ref[:, f + lo:f + hi], preferred_element_type=_F32)
            act = (gate * jax.nn.sigmoid(gate)) * up
            acc = acc + jnp.dot(act.astype(_BF16), wd_ref[lo:hi, :], preferred_element_type=_F32)
        y = x + 0.5 * acc
        if final:
            y = _rms(y, gf_ref[...])
        o_ref[rows, :] = y


def _ffn(x2d, norm_g, wgu, wd, pre=None, final_g=None, tm=1024, sub=512):
    t, d = x2d.shape
    kind = None if pre is None else pre[0]
    if kind == "merge":
        tm = sub
    row = lambda w: pl.BlockSpec((tm, w), lambda i: (i, 0))
    in_specs = [row(d), _const_spec((1, d)), _const_spec(wgu.shape), _const_spec(wd.shape)]
    args = [x2d, norm_g.reshape(1, d), wgu, wd]
    scratch = []
    if kind == "matmul":
        _, a2d, w_pre = pre
        in_specs += [row(a2d.shape[1]), _const_spec(w_pre.shape)]
        args += [a2d, w_pre]
    elif kind == "merge":
        _, os_, stats, seq_len, w_pre = pre
        nt = seq_len // tm
        e = _head_expand_matrix()

        def cls(a):
            dil, w = a.shape[1], a.shape[3]
            return pl.BlockSpec((None, dil, tm // dil, w), lambda i: (i // nt, 0, i % nt, 0))

        in_specs += [cls(a) for a in os_] + [cls(a) for a in stats]
        in_specs += [_const_spec(e.shape), _const_spec(w_pre.shape)]
        args += [*os_, *stats, e, w_pre]
        n_slots = 2 * (tm // sub)
        scratch = [pltpu.VMEM((n_slots, d // _LANES, sub, _LANES), _F32),
                   pltpu.VMEM((n_slots, sub, _LANES), _F32)]
    if final_g is not None:
        in_specs.append(_const_spec((1, d)))
        args.append(final_g.reshape(1, d))
    return pl.pallas_call(
        functools.partial(_ffn_kernel, pre=kind, final=final_g is not None, sub=sub),
        grid=(t // tm,), in_specs=in_specs, out_specs=row(d),
        out_shape=jax.ShapeDtypeStruct((t, d), _F32), scratch_shapes=scratch,
        compiler_params=_params("parallel"), name="ffn")(*args)


def _ffn_weights(w_gate_up, w_down):
    return w_gate_up.astype(_BF16), w_down.astype(_BF16)


def _t5_bucket(rel):
    nb = NUM_BUCKETS // 2
    max_exact = nb // 2
    ret = (rel > 0).astype(np.int32) * nb
    n = np.abs(rel)
    large = max_exact + (np.log(np.maximum(n, 1) / max_exact)
                         / math.log(MAX_DISTANCE / max_exact) * (nb - max_exact)).astype(np.int32)
    large = np.minimum(large, nb - 1)
    return ret + np.where(n < max_exact, n, large)


def _attn_bucket_tables(dilation):
    delta = np.arange(_TK)[None, :] - _HALF_W - np.arange(_TQ)[:, None]
    bucket = _t5_bucket(delta * dilation)
    band = np.abs(delta) <= _HALF_W
    kk = np.arange(_TK)[None, :]
    tables = []
    for variant in range(4):
        ok = band
        if variant & 1:
            ok = ok & (kk >= _HALF_W)
        if variant & 2:
            ok = ok & (kk < _TQ + _HALF_W)
        tables.append(np.where(ok, bucket, -1))
    return jnp.asarray(np.stack(tables), jnp.int32)


def _bias_tiles_kernel(rb_ref, bucket_ref, o_ref):
    hp = pl.program_id(1)
    bucket = bucket_ref[...]
    for hh in range(2):
        acc = jnp.full((_TQ, _TK), _NEG, _F32)
        for bkt in range(NUM_BUCKETS):
            acc = jnp.where(bucket == bkt, rb_ref[2 * hp + hh, bkt] * _LOG2E, acc)
        o_ref[hh * _TQ:(hh + 1) * _TQ, :] = acc


def _attn_bias_tiles(rel_bias_g, dilation):
    tables = _attn_bucket_tables(dilation)
    npair = HEADS_PER_GROUP // 2
    return pl.pallas_call(
        _bias_tiles_kernel, grid=(4, npair),
        in_specs=[pl.BlockSpec(memory_space=pltpu.SMEM),
                  pl.BlockSpec((None, _TQ, _TK), lambda v, hp: (v, 0, 0))],
        out_specs=pl.BlockSpec((None, None, 2 * _TQ, _TK), lambda v, hp: (v, hp, 0, 0)),
        out_shape=jax.ShapeDtypeStruct((4, npair, 2 * _TQ, _TK), _F32),
        compiler_params=_params("parallel", "parallel"), name="attn_bias_tiles")(
            rel_bias_g.astype(_F32).T, tables)


def _attn_kernel(q_ref, kp_ref, kc_ref, kn_ref, vp_ref, vc_ref, vn_ref, bias_ref,
                 o_ref, stat_ref, *, nr, nsub):
    i = pl.program_id(2)
    is_first = jnp.where(i == 0, 1, 0)
    is_last = jnp.where(i == pl.num_programs(2) - 1, 2, 0)
    lane = lax.broadcasted_iota(jnp.int32, (1, _LANES), 1)
    low = lane < HEAD_DIM
    zero = jnp.zeros((), _BF16)
    cur_rows = nsub * _TQ

    def window(prev_ref, cur_ref, next_ref, ri, sub, sl):
        start, end = sub * _TQ - _HALF_W, (sub + 1) * _TQ + _HALF_W
        pieces = []
        if start < 0:
            pieces.append(prev_ref[ri, _TQ - _HALF_W:, sl])
        pieces.append(cur_ref[ri, max(start, 0):min(end, cur_rows), sl])
        if end > cur_rows:
            pieces.append(next_ref[ri, :_HALF_W, sl])
        return jnp.concatenate(pieces, axis=0)

    for ri in range(nr):
        for sub in range(nsub):
            rows = slice(sub * _TQ, (sub + 1) * _TQ)
            variant = (is_first if sub == 0 else 0) + (is_last if sub == nsub - 1 else 0)
            stat = jnp.zeros((_TQ, _LANES), _F32)
            for hp in range(HEADS_PER_GROUP // 2):
                sl = slice(hp * _LANES, (hp + 1) * _LANES)
                q2 = q_ref[ri, rows, sl]
                qq = jnp.concatenate([jnp.where(low, q2, zero), jnp.where(low, zero, q2)], axis=0)
                k2 = window(kp_ref, kc_ref, kn_ref, ri, sub, sl)
                v2 = window(vp_ref, vc_ref, vn_ref, ri, sub, sl)
                s = lax.dot_general(qq, k2, (((1,), (1,)), ((), ())),
                                    preferred_element_type=_F32)
                s = s + bias_ref[variant, hp]
                m = jnp.max(s, axis=-1, keepdims=True)
                p = jnp.exp2(s - m)
                l = jnp.sum(p, axis=-1, keepdims=True)
                pv = jnp.dot(p.astype(_BF16), v2, preferred_element_type=_F32)
                o_ref[ri, rows, sl] = jnp.where(low, pv[:_TQ], pv[_TQ:]).astype(o_ref.dtype)
                h0, h1 = 2 * hp, 2 * hp + 1
                stat = jnp.where(lane == h0, m[:_TQ], jnp.where(lane == h1, m[_TQ:], stat))
                stat = jnp.where(lane == HEADS_PER_GROUP + h0, l[:_TQ],
                                 jnp.where(lane == HEADS_PER_GROUP + h1, l[_TQ:], stat))
            stat_ref[ri, rows, :] = stat


def _qkv_group_kernel(x_ref, g_ref, w_ref, o_ref, xn_ref, xp_ref, *, d, tm):
    rows = tm // d
    xn = _rms(x_ref[...], g_ref[...])
    if d == 1:
        xp_ref[...] = xn.astype(_BF16)
    else:
        for c in range(D_MODEL // _LANES):
            xn_ref[c] = xn[:, c * _LANES:(c + 1) * _LANES]
        for r in range(d):
            for c in range(D_MODEL // _LANES):
                xp_ref[r * rows:(r + 1) * rows, c * _LANES:(c + 1) * _LANES] = (
                    xn_ref[c, pl.ds(r, rows, stride=d), :].astype(_BF16))
    for j in range(3):
        cs = slice(j * D_MODEL, (j + 1) * D_MODEL)
        y = jnp.dot(xp_ref[...], w_ref[:, cs], preferred_element_type=_F32).astype(o_ref.dtype)
        for r in range(d):
            o_ref[r, :, cs] = y[r * rows:(r + 1) * rows]


def _qkv_group(x, norm_g, w_g_bf16, dilation, tm=512):
    b, s, dm = x.shape
    d = dilation
    n3 = w_g_bf16.shape[1]
    return pl.pallas_call(
        functools.partial(_qkv_group_kernel, d=d, tm=tm), grid=(b, s // tm),
        in_specs=[pl.BlockSpec((None, tm, dm), lambda bi, i: (bi, i, 0)), _const_spec((1, dm)),
                  _const_spec(w_g_bf16.shape)],
        out_specs=pl.BlockSpec((None, d, tm // d, n3), lambda bi, i: (bi, 0, i, 0)),
        out_shape=jax.ShapeDtypeStruct((b, d, s // d, n3), _BF16),
        scratch_shapes=[pltpu.VMEM((dm // _LANES, tm, _LANES), _F32), pltpu.VMEM((tm, dm), _BF16)],
        compiler_params=_params("parallel", "parallel"), name=f"qkv_d{d}")(
            x, norm_g.reshape(1, dm), w_g_bf16)


def _group_attention(qkv, bias_tiles):
    b, d, lc, _ = qkv.shape
    n_tq = lc // _TQ
    nsub = min(_ATTN_MAX_SUBTILES, n_tq)
    nr = min(d, _ATTN_MAX_SUBTILES // nsub)
    rows = nsub * _TQ

    def cur(which):
        return pl.BlockSpec((None, nr, rows, D_MODEL), lambda bi, r, i: (bi, r, i, which))

    def prev(which):
        return pl.BlockSpec((None, nr, _TQ, D_MODEL),
                            lambda bi, r, i: (bi, r, jnp.maximum(i * nsub - 1, 0), which))

    def nxt(which):
        return pl.BlockSpec((None, nr, _TQ, D_MODEL),
                            lambda bi, r, i: (bi, r, jnp.minimum((i + 1) * nsub, n_tq - 1), which))

    return pl.pallas_call(
        functools.partial(_attn_kernel, nr=nr, nsub=nsub), grid=(b, d // nr, n_tq // nsub),
        in_specs=[cur(0), prev(1), cur(1), nxt(1), prev(2), cur(2), nxt(2),
                  _const_spec(bias_tiles.shape)],
        out_specs=[pl.BlockSpec((None, nr, rows, D_MODEL), lambda bi, r, i: (bi, r, i, 0)),
                   pl.BlockSpec((None, nr, rows, _LANES), lambda bi, r, i: (bi, r, i, 0))],
        out_shape=[jax.ShapeDtypeStruct((b, d, lc, D_MODEL), _BF16),
                   jax.ShapeDtypeStruct((b, d, lc, _LANES), _F32)],
        compiler_params=_params("parallel", "parallel", "arbitrary"), name=f"attn_d{d}")(
            qkv, qkv, qkv, qkv, qkv, qkv, qkv, bias_tiles)


def _head_expand_matrix():
    e = np.zeros((2 * _LANES, D_MODEL), np.float32)
    for h in range(HEADS_PER_GROUP):
        e[h, h * HEAD_DIM:(h + 1) * HEAD_DIM] = 1.0
        e[_LANES + h, h * HEAD_DIM:(h + 1) * HEAD_DIM] = 1.0
    return jnp.asarray(e, _BF16)


def _dilated_attention_layer(x, norm_g, w_qkv, w_o, rel_bias):
    n3 = 3 * D_MODEL
    os_, stats = [], []
    for g, (window, dil) in enumerate(DILATED_GROUPS):
        assert (window // 2) // dil == _HALF_W
        col_scale = jnp.where(jnp.arange(n3) < D_MODEL, _LOG2E * HEAD_DIM ** -0.5, 1.0).astype(_F32)
        w_g = (w_qkv[:, g * n3:(g + 1) * n3].astype(_F32) * col_scale).astype(_BF16)
        qkv = _qkv_group(x, norm_g, w_g, dil)
        bias = _attn_bias_tiles(rel_bias[:, g * HEADS_PER_GROUP:(g + 1) * HEADS_PER_GROUP], dil)
        o, stat = _group_attention(qkv, bias)
        os_.append(o)
        stats.append(stat)
    return os_, stats, w_o.astype(_BF16)


def _hyena_in_kernel(xp_ref, x_ref, xn_ref, g_ref, w_ref, cw_ref, cb_ref,
                     x1_ref, x2_ref, v_ref, *, tm, sub):
    i = pl.program_id(1)
    last = pl.num_programs(1) - 1
    keep_prev = jnp.where(i > 0, 1.0, 0.0)
    keep_next = jnp.where(i < last, 1.0, 0.0)
    xe = jnp.concatenate([xp_ref[...] * keep_prev, x_ref[...], xn_ref[...] * keep_next], axis=0)
    n = sub + 16
    for h in range(tm // sub):
        xb = _rms(xe[h * sub:h * sub + n], g_ref[...]).astype(_BF16)
        rows = slice(h * sub, (h + 1) * sub)
        for j, o_ref in enumerate((x1_ref, x2_ref, v_ref)):
            cs = slice(j * D_MODEL, (j + 1) * D_MODEL)
            u = jnp.dot(xb, w_ref[:, cs], preferred_element_type=_F32)
            u_prev = pltpu.roll(u, 1, 0)[8:sub + 8]
            u_next = pltpu.roll(u, n - 1, 0)[8:sub + 8]
            cw = cw_ref[:, cs]
            y = (u_prev * cw[0:1] + u[8:sub + 8] * cw[1:2] + u_next * cw[2:3]) + cb_ref[:, cs]
            o_ref[rows, :] = y.astype(o_ref.dtype)


def _hyena_in(x, norm_g, w_in_bf16, conv_w, conv_b, tm=1024, sub=512):
    b, s, d = x.shape
    n3 = w_in_bf16.shape[1]
    nblk8 = s // 8
    r8 = tm // 8
    main = pl.BlockSpec((None, tm, d), lambda bi, i: (bi, i, 0))
    prev = pl.BlockSpec((None, 8, d), lambda bi, i: (bi, jnp.maximum(i * r8 - 1, 0), 0))
    nxt = pl.BlockSpec((None, 8, d), lambda bi, i: (bi, jnp.minimum((i + 1) * r8, nblk8 - 1), 0))
    out = jax.ShapeDtypeStruct((b, s, d), _BF16)
    return pl.pallas_call(
        functools.partial(_hyena_in_kernel, tm=tm, sub=sub), grid=(b, s // tm),
        in_specs=[prev, main, nxt, _const_spec((1, d)), _const_spec(w_in_bf16.shape),
                  _const_spec((3, n3)), _const_spec((1, n3))],
        out_specs=[main, main, main], out_shape=[out, out, out],
        compiler_params=_params("parallel", "arbitrary"), name="hyena_in")(
            x, x, x, norm_g.reshape(1, d), w_in_bf16, conv_w.astype(_F32),
            conv_b.astype(_F32).reshape(1, n3))


def _filter_kernel(fr_ref, w1_ref, b1_ref, w2_ref, b2_ref, w3_ref, b3_ref, fq_ref, w4h_ref, w4l_ref,
                   dl_ref, k0_ref, k1_ref, *, seq_len, tr):
    hi = lax.Precision.HIGHEST
    r = pl.program_id(0) * tr + lax.broadcasted_iota(jnp.int32, (tr, 1), 0)
    pos = jnp.abs(r - seq_len).astype(_F32)
    t = pos / (seq_len - 1.0)
    wpos = (2.0 * math.pi) * pos / seq_len
    lane = lax.broadcasted_iota(jnp.int32, (1, _LANES), 1)
    bands = (HYENA_EMB_DIM - 1) // 2
    ang = fr_ref[...] * wpos
    z = jnp.where(lane == 0, t,
                  jnp.where(lane <= bands, jnp.cos(ang),
                            jnp.where(lane <= 2 * bands, -jnp.sin(ang), 0.0)))
    fq = fq_ref[...]
    h = jnp.sin(fq * (jnp.dot(z, w1_ref[...], precision=hi, preferred_element_type=_F32) + b1_ref[...]))
    h = jnp.sin(fq * (jnp.dot(h, w2_ref[...], precision=hi, preferred_element_type=_F32) + b2_ref[...]))
    h = jnp.sin(fq * (jnp.dot(h, w3_ref[...], precision=hi, preferred_element_type=_F32) + b3_ref[...]))
    decay = jnp.exp(-t * jnp.abs(dl_ref[...]))
    live = r > 0
    h_hi = h.astype(_BF16)
    h_lo = (h - h_hi.astype(_F32)).astype(_BF16)
    for o, k_ref in enumerate((k0_ref, k1_ref)):
        cs = slice(o * D_MODEL, (o + 1) * D_MODEL)
        w_hi = w4h_ref[:, cs]
        y = jnp.dot(h_hi, w_hi, preferred_element_type=_F32)
        y = y + jnp.dot(h_hi, w4l_ref[:, cs], preferred_element_type=_F32)
        y = y + jnp.dot(h_lo, w_hi, preferred_element_type=_F32)
        k_ref[...] = jnp.where(live, y * decay, 0.0)


def _hyena_two_sided_filters(seq_len, w1, b1, w2, b2, w3, b3, freq, w4, tr=512):
    fw = HYENA_FILTER_WIDTH
    n_tiles = 2 * seq_len // tr
    w4d = w4.astype(_F32).reshape(fw, 2, 2, D_MODEL).transpose(2, 0, 1, 3).reshape(2, fw, 2 * D_MODEL)
    w4_hi = w4d.astype(_BF16)
    w4_lo = (w4d - w4_hi.astype(_F32)).astype(_BF16)
    w4_spec = pl.BlockSpec((None, fw, 2 * D_MODEL),
                           lambda i: (jnp.where(i >= n_tiles // 2, 0, 1), 0, 0))
    bands = (HYENA_EMB_DIM - 1) // 2
    fr = jnp.linspace(1e-4, bands - 1, bands, dtype=_F32)
    fr_lanes = jnp.zeros((1, _LANES), _F32).at[0, 1:1 + bands].set(fr).at[0, 1 + bands:1 + 2 * bands].set(fr)
    w1p = jnp.zeros((_LANES, fw), _F32).at[:HYENA_EMB_DIM].set(w1.astype(_F32))
    max_decay = math.log(DECAY_TARGET) / FAST_DECAY_PCT
    min_decay = math.log(DECAY_TARGET) / SLOW_DECAY_PCT
    deltas = jnp.linspace(min_decay, max_decay, D_MODEL, dtype=_F32).reshape(1, D_MODEL)
    vec = lambda a: a.astype(_F32).reshape(1, fw)
    out = jax.ShapeDtypeStruct((2 * seq_len, D_MODEL), _F32)
    cs = _const_spec
    return pl.pallas_call(
        functools.partial(_filter_kernel, seq_len=seq_len, tr=tr), grid=(n_tiles,),
        in_specs=[cs((1, _LANES)), cs((_LANES, fw)), cs((1, fw)), cs((fw, fw)), cs((1, fw)),
                  cs((fw, fw)), cs((1, fw)), cs((1, fw)), w4_spec, w4_spec, cs((1, D_MODEL))],
        out_specs=[pl.BlockSpec((tr, D_MODEL), lambda i: (i, 0))] * 2, out_shape=[out, out],
        compiler_params=_params("parallel"), name="hyena_filter")(
            fr_lanes, w1p, vec(b1), w2.astype(_F32), vec(b2), w3.astype(_F32), vec(b3), vec(freq),
            w4_hi, w4_lo, deltas)


def _dft_matrices(p):
    n = 2 * p
    k = np.arange(p)[:, None].astype(np.float64)
    s = np.arange(p)[None, :].astype(np.float64)
    ang = np.pi * k * s / p
    fre = np.cos(ang)
    fim = -np.sin(ang)
    fim[0, :] = np.cos(np.pi * s[0])
    fwd_half = np.concatenate([fre, fim], axis=0)
    tt = (p + np.arange(p))[:, None].astype(np.float64)
    kk = np.arange(p)[None, :].astype(np.float64)
    ang2 = np.pi * kk * tt / p
    are = 2.0 * np.cos(ang2) / n
    are[:, 0] = 1.0 / n
    aim = -2.0 * np.sin(ang2) / n
    aim[:, 0] = np.cos(np.pi * tt[:, 0]) / n
    inv = np.concatenate([are, aim], axis=1)
    return jnp.asarray(fwd_half, _BF16), jnp.asarray(inv, _BF16)


def _filter_spectrum_kernel(g_ref, f_ref, o_ref, prev_ref):
    q = pl.program_id(1)
    a = jnp.dot(f_ref[...], g_ref[...].astype(_BF16), preferred_element_type=_F32)

    @pl.when(q > 0)
    def _():
        odd = (lax.broadcasted_iota(jnp.int32, (a.shape[0], 1), 0) & 1) == 1
        o_ref[...] = (prev_ref[...] + jnp.where(odd, -a, a)).astype(o_ref.dtype)

    prev_ref[...] = a


def _filter_spectrum(kt, fwd_half, p, tc=256):
    rows, d = kt.shape
    nblk = rows // p
    return pl.pallas_call(
        _filter_spectrum_kernel, grid=(d // tc, nblk),
        in_specs=[pl.BlockSpec((p, tc), lambda c, q: (q, c)), _const_spec(fwd_half.shape)],
        out_specs=pl.BlockSpec((None, 2 * p, tc), lambda c, q: (jnp.maximum(q - 1, 0), 0, c)),
        out_shape=jax.ShapeDtypeStruct((nblk - 1, 2 * p, d), _SPECTRUM_DTYPE),
        scratch_shapes=[pltpu.VMEM((2 * p, tc), _F32)],
        compiler_params=_params("parallel", "arbitrary"), name="hyena_filter_spectrum")(kt, fwd_half)


def _long_conv_kernel(gate_ref, z_ref, kf_ref, skip_ref, ff_ref, ai_ref, o_ref, zf_ref, yfa_ref,
                      yfb_ref, *, p, nb, tc, ne):
    ch = _CONV_ROWS
    slots = zf_ref.shape[0]
    skip = skip_ref[...]
    first_row = lax.broadcasted_iota(jnp.int32, (ch, 1), 0) == 0

    def forward(e, i):
        zb = z_ref[e, pl.ds(pl.multiple_of(i * p, p), p), :]
        zf_ref[e % slots, i] = jnp.dot(ff_ref[...], zb,
                                       preferred_element_type=_F32).astype(zf_ref.dtype)

    def spectrum(e, j, yf_ref):
        for rc in range(p // ch):
            re = slice(rc * ch, (rc + 1) * ch)
            im = slice(p + rc * ch, p + (rc + 1) * ch)
            a = bm = cc = None
            for i0 in range(0, nb, _SPECTRUM_GROUP):
                ga = jnp.zeros((ch, tc), zf_ref.dtype)
                gb = jnp.zeros((ch, tc), zf_ref.dtype)
                gc = jnp.zeros((ch, tc), zf_ref.dtype)
                for i in range(i0, min(i0 + _SPECTRUM_GROUP, nb)):
                    q = j - i + (nb - 1)
                    zre, zim = zf_ref[e % slots, i, re, :], zf_ref[e % slots, i, im, :]
                    kre, kim = kf_ref[q, re, :], kf_ref[q, im, :]
                    ga = ga + zre * kre
                    gb = gb + zim * kim
                    gc = gc + (zre * kim + zim * kre)
                if nb > _SPECTRUM_GROUP:
                    ga, gb, gc = ga.astype(_F32), gb.astype(_F32), gc.astype(_F32)
                a = ga if a is None else a + ga
                bm = gb if bm is None else bm + gb
                cc = gc if cc is None else cc + gc
            if rc == 0:
                yf_ref[re, :] = jnp.where(first_row, a, a - bm).astype(_BF16)
                yf_ref[im, :] = jnp.where(first_row, bm, cc).astype(_BF16)
            else:
                yf_ref[re, :] = (a - bm).astype(_BF16)
                yf_ref[im, :] = cc.astype(_BF16)

    def finish(e, j, yf_ref):
        y = jnp.dot(ai_ref[...], yf_ref[...], preferred_element_type=_F32)
        rows = pl.ds(pl.multiple_of(j * p, p), p)
        zt = z_ref[e, rows, :].astype(_F32)
        gt = gate_ref[e, rows, :].astype(_F32)
        o_ref[e, rows, :] = (gt * (y + zt * skip)).astype(o_ref.dtype)

    yf = (yfa_ref, yfb_ref)
    if nb <= _CONV_STATIC_BLOCKS:
        for i in range(nb):
            forward(0, i)
        pending = None
        for e in range(ne):
            for j in range(nb):
                if pending is not None:
                    finish(*pending)
                if e + 1 < ne:
                    forward(e + 1, j)
                spectrum(e, j, yf[j % 2])
                pending = (e, j, yf[j % 2])
        finish(*pending)
    else:
        assert ne == 1 and nb % 2 == 0
        for i in range(nb):
            forward(0, i)
        spectrum(0, 0, yfa_ref)

        def pair_body(k, carry):
            finish(0, 2 * k, yfa_ref)
            spectrum(0, 2 * k + 1, yfb_ref)
            finish(0, 2 * k + 1, yfb_ref)
            spectrum(0, 2 * k + 2, yfa_ref)
            return carry

        lax.fori_loop(0, nb // 2 - 1, pair_body, 0)
        finish(0, nb - 2, yfa_ref)
        spectrum(0, nb - 1, yfb_ref)
        finish(0, nb - 1, yfb_ref)


def _long_conv(gate, z, kf, skip_o, fwd_half, inv, p):
    b, l, d = z.shape
    nb = l // p
    ne = 2 if (nb <= _CONV_STATIC_BLOCKS and b % 2 == 0) else 1
    slots = min(ne, 2)
    spectra = (2 * nb - 1 + slots * nb) * 2 * p * jnp.dtype(_SPECTRUM_DTYPE).itemsize
    seq_bytes = ne * l * 2
    dft_bytes = 2 * (2 * p * p * 2)
    budget = _V7X_VMEM_LIMIT_BYTES * 9 // 10
    tc, io_buffers = 128, 2
    for cand_tc, cand_buffers in ((256, 2), (256, 1)):
        if cand_tc * (spectra + (2 + 2 * cand_buffers) * seq_bytes) + dft_bytes <= budget:
            tc, io_buffers = cand_tc, cand_buffers
            break
    io_mode = {} if io_buffers == 2 else {"pipeline_mode": pl.Buffered(1)}
    seq = pl.BlockSpec((ne, l, tc), lambda c, bi: (bi, 0, c))
    seq_io = pl.BlockSpec((ne, l, tc), lambda c, bi: (bi, 0, c), **io_mode)
    kf_spec = pl.BlockSpec((2 * nb - 1, 2 * p, tc), lambda c, bi: (0, 0, c),
                           pipeline_mode=pl.Buffered(1))
    return pl.pallas_call(
        functools.partial(_long_conv_kernel, p=p, nb=nb, tc=tc, ne=ne), grid=(d // tc, b // ne),
        in_specs=[seq_io, seq, kf_spec, pl.BlockSpec((1, tc), lambda c, bi: (0, c)),
                  _const_spec(fwd_half.shape), _const_spec(inv.shape)],
        out_specs=seq_io, out_shape=jax.ShapeDtypeStruct((b, l, d), _BF16),
        scratch_shapes=[pltpu.VMEM((slots, nb, 2 * p, tc), _SPECTRUM_DTYPE),
                        pltpu.VMEM((2 * p, tc), _BF16),
                        pltpu.VMEM((2 * p, tc), _BF16)],
        compiler_params=_params("parallel", "arbitrary"), name="hyena_long_conv")(
            gate, z, kf, skip_o.astype(_F32).reshape(1, d), fwd_half, inv)


def _hyena_layer(x, norm_g, w_in, conv_w, conv_b, w1, b1, w2, b2, w3, b3, freq, w4, skip, w_out):
    b, l, d = x.shape
    p = min(l, max(_CONV_P, l // 8))
    x1, x2, v = _hyena_in(x, norm_g, w_in.astype(_BF16), conv_w, conv_b)
    k0, k1 = _hyena_two_sided_filters(l, w1, b1, w2, b2, w3, b3, freq, w4)
    fwd_half, inv = _dft_matrices(p)
    z = v
    for o, (gate, kt) in enumerate(((x1, k0), (x2, k1))):
        kf = _filter_spectrum(kt, fwd_half, p)
        z = _long_conv(gate, z, kf, skip[o], fwd_half, inv, p)
    return z.reshape(b * l, d), w_out.astype(_BF16)


def _trunk(x, rel_bias, ffn1_norm, ffn1_w, mixer_norm, attn_w_qkv, attn_w_o, hy, ffn2_norm, ffn2_w,
           final_norm):
    b, s, d = x.shape
    depth = ffn1_norm.shape[0]
    flat = lambda a: a.reshape(b * s, d)
    for i in range(depth):
        x = _ffn(flat(x), ffn1_norm[i], *ffn1_w[i]).reshape(b, s, d)
        j = i // 2
        if i % 2 == 0:
            os_, stats, w_o = _dilated_attention_layer(x, mixer_norm[i], attn_w_qkv[j], attn_w_o[j],
                                                       rel_bias)
            pre = ("merge", os_, stats, s, w_o)
        else:
            pre = ("matmul", *_hyena_layer(x, mixer_norm[i], *[a[j] for a in hy]))
        fin = final_norm if i == depth - 1 else None
        x = _ffn(flat(x), ffn2_norm[i], *ffn2_w[i], pre=pre, final_g=fin).reshape(b, s, d)
    return x


def kernel(x_prompt, x_sample, rel_bias, ffn1_norm, ffn1_w_gate_up, ffn1_w_down, mixer_norm, attn_w_qkv, attn_w_o, hyena_w_in, hyena_conv_w, hyena_conv_b, hyena_filt_w1, hyena_filt_b1, hyena_filt_w2, hyena_filt_b2, hyena_filt_w3, hyena_filt_b3, hyena_filt_freq, hyena_filt_w4, hyena_skip, hyena_w_out, ffn2_norm, ffn2_w_gate_up, ffn2_w_down, final_norm):
    depth = ffn1_norm.shape[0]
    ffn1_w = [_ffn_weights(ffn1_w_gate_up[i], ffn1_w_down[i]) for i in range(depth)]
    ffn2_w = [_ffn_weights(ffn2_w_gate_up[i], ffn2_w_down[i]) for i in range(depth)]
    hy = (hyena_w_in, hyena_conv_w, hyena_conv_b, hyena_filt_w1, hyena_filt_b1, hyena_filt_w2,
          hyena_filt_b2, hyena_filt_w3, hyena_filt_b3, hyena_filt_freq, hyena_filt_w4, hyena_skip,
          hyena_w_out)
    args = (rel_bias, ffn1_norm, ffn1_w, mixer_norm, attn_w_qkv, attn_w_o, hy, ffn2_norm, ffn2_w,
            final_norm)
    return (_trunk(x_prompt, *args), _trunk(x_sample, *args))
```

```python
import functools
import math

import numpy as np
import jax
import jax.numpy as jnp
from jax import lax
from jax.experimental import pallas as pl
from jax.experimental.pallas import tpu as pltpu

D_MODEL = 1024
HEAD_DIM = 64
HEADS_PER_GROUP = D_MODEL // HEAD_DIM
DILATED_GROUPS = ((128, 1), (512, 4), (2048, 16))
N_GROUPS = len(DILATED_GROUPS)
NUM_BUCKETS = 32
MAX_DISTANCE = 1024
HYENA_EMB_DIM = 33
HYENA_FILTER_WIDTH = 64
FAST_DECAY_PCT = 0.3
SLOW_DECAY_PCT = 1.5
DECAY_TARGET = 1e-2
D_FF = 2816
RMS_EPS = 1e-6

_F32 = jnp.float32
_BF16 = jnp.bfloat16
_V7X_VMEM_LIMIT_BYTES = 58 * 1024 * 1024
_LANES = 128
_NEG = -1e30
_LOG2E = math.log2(math.e)

_TQ = 128
_HALF_W = 64
_TK = _TQ + 2 * _HALF_W
_ATTN_MAX_SUBTILES = 4
_MXU_DIM = 256
_CONV_P = 512
_CONV_ROWS = 32
_SPECTRUM_DTYPE = jnp.bfloat16
_SPECTRUM_GROUP = 4
_CONV_STATIC_BLOCKS = 4


def _params(*sem):
    return pltpu.CompilerParams(dimension_semantics=sem, vmem_limit_bytes=_V7X_VMEM_LIMIT_BYTES)


def _const_spec(shape):
    nd = len(shape)
    return pl.BlockSpec(shape, lambda *_: (0,) * nd, pipeline_mode=pl.Buffered(1))


def _rms(x, g):
    y = x * lax.rsqrt(jnp.mean(x * x, axis=-1, keepdims=True) + RMS_EPS)
    return y * g


def _merge_rows(o_refs, s_refs, e_ref, of_ref, lf_ref, h, sub):
    nblk = D_MODEL // _LANES

    def natural(o_ref, s_ref, slot):
        d = o_ref.shape[0]
        n = sub // d
        cls = slice(h * n, (h + 1) * n)
        if d == 1:
            return o_ref[0, cls, :].astype(_F32), s_ref[0, cls, :]
        for r in range(d):
            o_r = o_ref[r, cls, :].astype(_F32)
            for c in range(nblk):
                of_ref[slot, c, pl.ds(r, n, stride=d), :] = o_r[:, c * _LANES:(c + 1) * _LANES]
            lf_ref[slot, pl.ds(r, n, stride=d), :] = s_ref[r, cls, :]
        return jnp.concatenate([of_ref[slot, c] for c in range(nblk)], axis=1), lf_ref[slot]

    outs, stats = [], []
    for g, (o_ref, s_ref) in enumerate(zip(o_refs, s_refs)):
        o, st = natural(o_ref, s_ref, 2 * h + (g % 2))
        outs.append(o)
        stats.append(st)
    head = lax.broadcasted_iota(jnp.int32, (1, _LANES), 1) < HEADS_PER_GROUP
    ms = [jnp.where(head, st, 0.0) for st in stats]
    ls = [jnp.where(head, pltpu.roll(st, _LANES - HEADS_PER_GROUP, 1), 1.0) for st in stats]
    m = functools.reduce(jnp.maximum, ms)
    ws = [jnp.exp2(mg - m) for mg in ms]
    inv = 1.0 / sum(w * l for w, l in zip(ws, ls))
    merged = None
    for w, o in zip(ws, outs):
        al = w * inv
        hi = al.astype(_BF16)
        lo = (al - hi.astype(_F32)).astype(_BF16)
        term = jnp.dot(jnp.concatenate([hi, lo], axis=1), e_ref[...], preferred_element_type=_F32) * o
        merged = term if merged is None else merged + term
    return merged


def _ffn_kernel(*refs, pre, final, sub):
    refs = list(refs)
    x_ref, g_ref, wgu_ref, wd_ref = refs[:4]
    extra = refs[4:]
    if pre == "matmul":
        a_ref, wpre_ref = extra[:2]
        extra = extra[2:]
    elif pre == "merge":
        o_refs, s_refs, (e_ref, wpre_ref) = extra[0:3], extra[3:6], extra[6:8]
        extra = extra[8:]
    gf_ref = extra.pop(0) if final else None
    o_ref = extra.pop(0)
    f = wd_ref.shape[0]
    n_tiles = f // _MXU_DIM
    bounds = [0, (n_tiles + 1) // 2 * _MXU_DIM, f]
    for h in range(x_ref.shape[0] // sub):
        rows = slice(h * sub, (h + 1) * sub)
        x = x_ref[rows, :]
        if pre == "matmul":
            x = x + jnp.dot(a_ref[rows, :], wpre_ref[...], preferred_element_type=_F32)
        elif pre == "merge":
            merged = _merge_rows(o_refs, s_refs, e_ref, extra[0], extra[1], h, sub)
            x = x + jnp.dot(merged.astype(_BF16), wpre_ref[...], preferred_element_type=_F32)
        xb = _rms(x, g_ref[...]).astype(_BF16)
        acc = jnp.zeros(x.shape, _F32)
        for lo, hi in zip(bounds[:-1], bounds[1:]):
            gate = jnp.dot(xb, wgu_ref[:, lo:hi], preferred_element_type=_F32)
            up = jnp.dot(xb, wgu_ref[:, f + lo:f + hi], preferred_element_type=_F32)
            act = (gate * jax.nn.sigmoid(gate)) * up
            acc = acc + jnp.dot(act.astype(_BF16), wd_ref[lo:hi, :], preferred_element_type=_F32)
        y = x + 0.5 * acc
        if final:
            y = _rms(y, gf_ref[...])
        o_ref[rows, :] = y


def _ffn(x2d, norm_g, wgu, wd, pre=None, final_g=None, tm=1024, sub=512):
    t, d = x2d.shape
    kind = None if pre is None else pre[0]
    if kind == "merge":
        tm = sub
    row = lambda w: pl.BlockSpec((tm, w), lambda i: (i, 0))
    in_specs = [row(d), _const_spec((1, d)), _const_spec(wgu.shape), _const_spec(wd.shape)]
    args = [x2d, norm_g.reshape(1, d), wgu, wd]
    scratch = []
    if kind == "matmul":
        _, a2d, w_pre = pre
        in_specs += [row(a2d.shape[1]), _const_spec(w_pre.shape)]
        args += [a2d, w_pre]
    elif kind == "merge":
        _, os_, stats, seq_len, w_pre = pre
        nt = seq_len // tm
        e = _head_expand_matrix()

        def cls(a):
            dil, w = a.shape[1], a.shape[3]
            return pl.BlockSpec((None, dil, tm // dil, w), lambda i: (i // nt, 0, i % nt, 0))

        in_specs += [cls(a) for a in os_] + [cls(a) for a in stats]
        in_specs += [_const_spec(e.shape), _const_spec(w_pre.shape)]
        args += [*os_, *stats, e, w_pre]
        n_slots = 2 * (tm // sub)
        scratch = [pltpu.VMEM((n_slots, d // _LANES, sub, _LANES), _F32),
                   pltpu.VMEM((n_slots, sub, _LANES), _F32)]
    if final_g is not None:
        in_specs.append(_const_spec((1, d)))
        args.append(final_g.reshape(1, d))
    return pl.pallas_call(
        functools.partial(_ffn_kernel, pre=kind, final=final_g is not None, sub=sub),
        grid=(t // tm,), in_specs=in_specs, out_specs=row(d),
        out_shape=jax.ShapeDtypeStruct((t, d), _F32), scratch_shapes=scratch,
        compiler_params=_params("parallel"), name="ffn")(*args)


def _ffn_weights(w_gate_up, w_down):
    return w_gate_up.astype(_BF16), w_down.astype(_BF16)


def _t5_bucket(rel):
    nb = NUM_BUCKETS // 2
    max_exact = nb // 2
    ret = (rel > 0).astype(np.int32) * nb
    n = np.abs(rel)
    large = max_exact + (np.log(np.maximum(n, 1) / max_exact)
                         / math.log(MAX_DISTANCE / max_exact) * (nb - max_exact)).astype(np.int32)
    large = np.minimum(large, nb - 1)
    return ret + np.where(n < max_exact, n, large)


def _attn_bucket_tables(dilation):
    delta = np.arange(_TK)[None, :] - _HALF_W - np.arange(_TQ)[:, None]
    bucket = _t5_bucket(delta * dilation)
    band = np.abs(delta) <= _HALF_W
    kk = np.arange(_TK)[None, :]
    tables = []
    for variant in range(4):
        ok = band
        if variant & 1:
            ok = ok & (kk >= _HALF_W)
        if variant & 2:
            ok = ok & (kk < _TQ + _HALF_W)
        tables.append(np.where(ok, bucket, -1))
    return jnp.asarray(np.stack(tables), jnp.int32)


def _bias_tiles_kernel(rb_ref, bucket_ref, o_ref):
    hp = pl.program_id(1)
    bucket = bucket_ref[...]
    for hh in range(2):
        acc = jnp.full((_TQ, _TK), _NEG, _F32)
        for bkt in range(NUM_BUCKETS):
            acc = jnp.where(bucket == bkt, rb_ref[2 * hp + hh, bkt] * _LOG2E, acc)
        o_ref[hh * _TQ:(hh + 1) * _TQ, :] = acc


def _attn_bias_tiles(rel_bias_g, dilation):
    tables = _attn_bucket_tables(dilation)
    npair = HEADS_PER_GROUP // 2
    return pl.pallas_call(
        _bias_tiles_kernel, grid=(4, npair),
        in_specs=[pl.BlockSpec(memory_space=pltpu.SMEM),
                  pl.BlockSpec((None, _TQ, _TK), lambda v, hp: (v, 0, 0))],
        out_specs=pl.BlockSpec((None, None, 2 * _TQ, _TK), lambda v, hp: (v, hp, 0, 0)),
        out_shape=jax.ShapeDtypeStruct((4, npair, 2 * _TQ, _TK), _F32),
        compiler_params=_params("parallel", "parallel"), name="attn_bias_tiles")(
            rel_bias_g.astype(_F32).T, tables)


def _attn_kernel(q_ref, kp_ref, kc_ref, kn_ref, vp_ref, vc_ref, vn_ref, bias_ref,
                 o_ref, stat_ref, *, nr, nsub):
    i = pl.program_id(2)
    is_first = jnp.where(i == 0, 1, 0)
    is_last = jnp.where(i == pl.num_programs(2) - 1, 2, 0)
    lane = lax.broadcasted_iota(jnp.int32, (1, _LANES), 1)
    low = lane < HEAD_DIM
    zero = jnp.zeros((), _BF16)
    cur_rows = nsub * _TQ

    def window(prev_ref, cur_ref, next_ref, ri, sub, sl):
        start, end = sub * _TQ - _HALF_W, (sub + 1) * _TQ + _HALF_W
        pieces = []
        if start < 0:
            pieces.append(prev_ref[ri, _TQ - _HALF_W:, sl])
        pieces.append(cur_ref[ri, max(start, 0):min(end, cur_rows), sl])
        if end > cur_rows:
            pieces.append(next_ref[ri, :_HALF_W, sl])
        return jnp.concatenate(pieces, axis=0)

    for ri in range(nr):
        for sub in range(nsub):
            rows = slice(sub * _TQ, (sub + 1) * _TQ)
            variant = (is_first if sub == 0 else 0) + (is_last if sub == nsub - 1 else 0)
            stat = jnp.zeros((_TQ, _LANES), _F32)
            for hp in range(HEADS_PER_GROUP // 2):
                sl = slice(hp * _LANES, (hp + 1) * _LANES)
                q2 = q_ref[ri, rows, sl]
                qq = jnp.concatenate([jnp.where(low, q2, zero), jnp.where(low, zero, q2)], axis=0)
                k2 = window(kp_ref, kc_ref, kn_ref, ri, sub, sl)
                v2 = window(vp_ref, vc_ref, vn_ref, ri, sub, sl)
                s = lax.dot_general(qq, k2, (((1,), (1,)), ((), ())),
                                    preferred_element_type=_F32)
                s = s + bias_ref[variant, hp]
                m = jnp.max(s, axis=-1, keepdims=True)
                p = jnp.exp2(s - m)
                l = jnp.sum(p, axis=-1, keepdims=True)
                pv = jnp.dot(p.astype(_BF16), v2, preferred_element_type=_F32)
                o_ref[ri, rows, sl] = jnp.where(low, pv[:_TQ], pv[_TQ:]).astype(o_ref.dtype)
                h0, h1 = 2 * hp, 2 * hp + 1
                stat = jnp.where(lane == h0, m[:_TQ], jnp.where(lane == h1, m[_TQ:], stat))
                stat = jnp.where(lane == HEADS_PER_GROUP + h0, l[:_TQ],
                                 jnp.where(lane == HEADS_PER_GROUP + h1, l[_TQ:], stat))
            stat_ref[ri, rows, :] = stat


def _qkv_kernel(x_ref, g_ref, w_ref, *rest, dilations, tm):
    o_refs = rest[:len(dilations)]
    xn_ref, xp_ref = rest[len(dilations):]
    nblk = D_MODEL // _LANES
    xn = _rms(x_ref[...], g_ref[...])
    for c in range(nblk):
        xn_ref[c] = xn[:, c * _LANES:(c + 1) * _LANES]
    for g, (d, o_ref) in enumerate(zip(dilations, o_refs)):
        rows = tm // d
        if d == 1:
            xp = xn.astype(_BF16)
        else:
            for r in range(d):
                for c in range(nblk):
                    xp_ref[g % 2, r * rows:(r + 1) * rows, c * _LANES:(c + 1) * _LANES] = (
                        xn_ref[c, pl.ds(r, rows, stride=d), :].astype(_BF16))
            xp = xp_ref[g % 2]
        for j in range(3):
            cs = slice(j * D_MODEL, (j + 1) * D_MODEL)
            y = jnp.dot(xp, w_ref[g, :, cs], preferred_element_type=_F32).astype(o_ref.dtype)
            for r in range(d):
                o_ref[r, :, cs] = y[r * rows:(r + 1) * rows]


def _qkv(x, norm_g, w_bf16, dilations, tm=512):
    b, s, dm = x.shape
    n3 = w_bf16.shape[2]
    return pl.pallas_call(
        functools.partial(_qkv_kernel, dilations=dilations, tm=tm), grid=(b, s // tm),
        in_specs=[pl.BlockSpec((None, tm, dm), lambda bi, i: (bi, i, 0)), _const_spec((1, dm)),
                  _const_spec(w_bf16.shape)],
        out_specs=[pl.BlockSpec((None, d, tm // d, n3), lambda bi, i: (bi, 0, i, 0))
                   for d in dilations],
        out_shape=[jax.ShapeDtypeStruct((b, d, s // d, n3), _BF16) for d in dilations],
        scratch_shapes=[pltpu.VMEM((dm // _LANES, tm, _LANES), _F32),
                        pltpu.VMEM((2, tm, dm), _BF16)],
        compiler_params=_params("parallel", "parallel"), name="qkv")(
            x, norm_g.reshape(1, dm), w_bf16)


def _group_attention(qkv, bias_tiles):
    b, d, lc, _ = qkv.shape
    n_tq = lc // _TQ
    nsub = min(_ATTN_MAX_SUBTILES, n_tq)
    nr = min(d, _ATTN_MAX_SUBTILES // nsub)
    rows = nsub * _TQ

    def cur(which):
        return pl.BlockSpec((None, nr, rows, D_MODEL), lambda bi, r, i: (bi, r, i, which))

    def prev(which):
        return pl.BlockSpec((None, nr, _TQ, D_MODEL),
                            lambda bi, r, i: (bi, r, jnp.maximum(i * nsub - 1, 0), which))

    def nxt(which):
        return pl.BlockSpec((None, nr, _TQ, D_MODEL),
                            lambda bi, r, i: (bi, r, jnp.minimum((i + 1) * nsub, n_tq - 1), which))

    return pl.pallas_call(
        functools.partial(_attn_kernel, nr=nr, nsub=nsub), grid=(b, d // nr, n_tq // nsub),
        in_specs=[cur(0), prev(1), cur(1), nxt(1), prev(2), cur(2), nxt(2),
                  _const_spec(bias_tiles.shape)],
        out_specs=[pl.BlockSpec((None, nr, rows, D_MODEL), lambda bi, r, i: (bi, r, i, 0)),
                   pl.BlockSpec((None, nr, rows, _LANES), lambda bi, r, i: (bi, r, i, 0))],
        out_shape=[jax.ShapeDtypeStruct((b, d, lc, D_MODEL), _BF16),
                   jax.ShapeDtypeStruct((b, d, lc, _LANES), _F32)],
        compiler_params=_params("parallel", "parallel", "arbitrary"), name=f"attn_d{d}")(
            qkv, qkv, qkv, qkv, qkv, qkv, qkv, bias_tiles)


def _head_expand_matrix():
    e = np.zeros((2 * _LANES, D_MODEL), np.float32)
    for h in range(HEADS_PER_GROUP):
        e[h, h * HEAD_DIM:(h + 1) * HEAD_DIM] = 1.0
        e[_LANES + h, h * HEAD_DIM:(h + 1) * HEAD_DIM] = 1.0
    return jnp.asarray(e, _BF16)


def _dilated_attention_layer(x, norm_g, w_qkv, w_o, rel_bias):
    n3 = 3 * D_MODEL
    dilations = tuple(dil for _, dil in DILATED_GROUPS)
    assert all((window // 2) // dil == _HALF_W for window, dil in DILATED_GROUPS)
    col_scale = jnp.where(jnp.arange(n3) < D_MODEL, _LOG2E * HEAD_DIM ** -0.5, 1.0).astype(_F32)
    w_groups = w_qkv.astype(_F32).reshape(D_MODEL, N_GROUPS, n3).transpose(1, 0, 2) * col_scale
    qkvs = _qkv(x, norm_g, w_groups.astype(_BF16), dilations)
    os_, stats = [], []
    for g, (dil, qkv) in enumerate(zip(dilations, qkvs)):
        bias = _attn_bias_tiles(rel_bias[:, g * HEADS_PER_GROUP:(g + 1) * HEADS_PER_GROUP], dil)
        o, stat = _group_attention(qkv, bias)
        os_.append(o)
        stats.append(stat)
    return os_, stats, w_o.astype(_BF16)


def _hyena_in_kernel(xp_ref, x_ref, xn_ref, g_ref, w_ref, cw_ref, cb_ref,
                     x1_ref, x2_ref, v_ref, *, tm, sub):
    i = pl.program_id(1)
    last = pl.num_programs(1) - 1
    keep_prev = jnp.where(i > 0, 1.0, 0.0)
    keep_next = jnp.where(i < last, 1.0, 0.0)
    xe = jnp.concatenate([xp_ref[...] * keep_prev, x_ref[...], xn_ref[...] * keep_next], axis=0)
    n = sub + 16
    for h in range(tm // sub):
        xb = _rms(xe[h * sub:h * sub + n], g_ref[...]).astype(_BF16)
        rows = slice(h * sub, (h + 1) * sub)
        for j, o_ref in enumerate((x1_ref, x2_ref, v_ref)):
            cs = slice(j * D_MODEL, (j + 1) * D_MODEL)
            u = jnp.dot(xb, w_ref[:, cs], preferred_element_type=_F32)
            u_prev = pltpu.roll(u, 1, 0)[8:sub + 8]
            u_next = pltpu.roll(u, n - 1, 0)[8:sub + 8]
            cw = cw_ref[:, cs]
            y = (u_prev * cw[0:1] + u[8:sub + 8] * cw[1:2] + u_next * cw[2:3]) + cb_ref[:, cs]
            o_ref[rows, :] = y.astype(o_ref.dtype)


def _hyena_in(x, norm_g, w_in_bf16, conv_w, conv_b, tm=1024, sub=512):
    b, s, d = x.shape
    n3 = w_in_bf16.shape[1]
    nblk8 = s // 8
    r8 = tm // 8
    main = pl.BlockSpec((None, tm, d), lambda bi, i: (bi, i, 0))
    prev = pl.BlockSpec((None, 8, d), lambda bi, i: (bi, jnp.maximum(i * r8 - 1, 0), 0))
    nxt = pl.BlockSpec((None, 8, d), lambda bi, i: (bi, jnp.minimum((i + 1) * r8, nblk8 - 1), 0))
    out = jax.ShapeDtypeStruct((b, s, d), _BF16)
    return pl.pallas_call(
        functools.partial(_hyena_in_kernel, tm=tm, sub=sub), grid=(b, s // tm),
        in_specs=[prev, main, nxt, _const_spec((1, d)), _const_spec(w_in_bf16.shape),
                  _const_spec((3, n3)), _const_spec((1, n3))],
        out_specs=[main, main, main], out_shape=[out, out, out],
        compiler_params=_params("parallel", "arbitrary"), name="hyena_in")(
            x, x, x, norm_g.reshape(1, d), w_in_bf16, conv_w.astype(_F32),
            conv_b.astype(_F32).reshape(1, n3))


def _filter_kernel(fr_ref, ph_ref, w1a_ref, w1b_ref, b1_ref, w2_ref, b2_ref, w3_ref, b3_ref, fq_ref,
                   w4h_ref, w4l_ref, dl_ref, k0_ref, k1_ref, *, seq_len, tr):
    hi = lax.Precision.HIGHEST
    half = tr // 2
    r = pl.program_id(0) * tr + lax.broadcasted_iota(jnp.int32, (tr, 1), 0)
    pos = jnp.abs(r - seq_len).astype(_F32)
    t = pos / (seq_len - 1.0)
    wpos = (2.0 * math.pi) * pos / seq_len
    lane = lax.broadcasted_iota(jnp.int32, (1, _LANES), 1)
    z = jnp.where(lane == 0, t,
                  jnp.where(lane < HYENA_EMB_DIM, jnp.cos(fr_ref[...] * wpos + ph_ref[...]), 0.0))
    dot = functools.partial(jnp.dot, precision=hi, preferred_element_type=_F32)
    fq = fq_ref[...]
    h = jnp.sin(fq * (dot(z[:half], w1a_ref[...]) + dot(z[half:], w1b_ref[...]) + b1_ref[...]))
    h = jnp.sin(fq * (dot(h, w2_ref[...]) + b2_ref[...]))
    h = jnp.sin(fq * (dot(h, w3_ref[...]) + b3_ref[...]))
    decay = jnp.exp(-t * jnp.abs(dl_ref[...]))
    live = r > 0
    h_hi = h.astype(_BF16)
    h_lo = (h - h_hi.astype(_F32)).astype(_BF16)
    for o, k_ref in enumerate((k0_ref, k1_ref)):
        cs = slice(o * D_MODEL, (o + 1) * D_MODEL)
        for part in range(2):
            w_hi = w4h_ref[part, :, cs]
            y = jnp.dot(h_hi, w_hi, preferred_element_type=_F32)
            y = y + jnp.dot(h_hi, w4l_ref[part, :, cs], preferred_element_type=_F32)
            y = y + jnp.dot(h_lo, w_hi, preferred_element_type=_F32)
            rows = slice(part * half, (part + 1) * half)
            k_ref[rows, :] = jnp.where(live[rows], y * decay[rows], 0.0)


def _hyena_two_sided_filters(seq_len, w1, b1, w2, b2, w3, b3, freq, w4, tr=512):
    fw = HYENA_FILTER_WIDTH
    assert 2 * fw == _LANES
    n_tiles = 2 * seq_len // tr
    f32 = lambda a: a.astype(_F32)
    zeros = jnp.zeros((fw, 2 * D_MODEL), _F32)
    w4d = f32(w4).reshape(fw, 2, 2, D_MODEL).transpose(2, 0, 1, 3).reshape(2, fw, 2 * D_MODEL)
    w4p = jnp.stack([jnp.stack([jnp.concatenate([w4d[dr], zeros]), jnp.concatenate([zeros, w4d[dr]])])
                     for dr in range(2)])
    w4_hi = w4p.astype(_BF16)
    w4_lo = (w4p - w4_hi.astype(_F32)).astype(_BF16)
    w4_spec = pl.BlockSpec((None, 2, _LANES, 2 * D_MODEL),
                           lambda i: (jnp.where(i >= n_tiles // 2, 0, 1), 0, 0, 0))
    bands = (HYENA_EMB_DIM - 1) // 2
    fr = jnp.linspace(1e-4, bands - 1, bands, dtype=_F32)
    fr_lanes = jnp.zeros((1, _LANES), _F32).at[0, 1:1 + bands].set(fr).at[0, 1 + bands:1 + 2 * bands].set(fr)
    phase = jnp.zeros((1, _LANES), _F32).at[0, 1 + bands:1 + 2 * bands].set(0.5 * math.pi)
    w1p = jnp.zeros((_LANES, fw), _F32).at[:HYENA_EMB_DIM].set(f32(w1))
    z64 = jnp.zeros((_LANES, fw), _F32)
    w1a = jnp.concatenate([w1p, z64], axis=1)
    w1b = jnp.concatenate([z64, w1p], axis=1)
    zz = jnp.zeros((fw, fw), _F32)
    bdiag = lambda w: jnp.block([[f32(w), zz], [zz, f32(w)]])
    twice = lambda a: jnp.tile(f32(a).reshape(1, fw), (1, 2))
    max_decay = math.log(DECAY_TARGET) / FAST_DECAY_PCT
    min_decay = math.log(DECAY_TARGET) / SLOW_DECAY_PCT
    deltas = jnp.linspace(min_decay, max_decay, D_MODEL, dtype=_F32).reshape(1, D_MODEL)
    out = jax.ShapeDtypeStruct((2 * seq_len, D_MODEL), _F32)
    cs = _const_spec
    sq, vec = cs((_LANES, _LANES)), cs((1, _LANES))
    return pl.pallas_call(
        functools.partial(_filter_kernel, seq_len=seq_len, tr=tr), grid=(n_tiles,),
        in_specs=[vec, vec, sq, sq, vec, sq, vec, sq, vec, vec, w4_spec, w4_spec, cs((1, D_MODEL))],
        out_specs=[pl.BlockSpec((tr, D_MODEL), lambda i: (i, 0))] * 2, out_shape=[out, out],
        compiler_params=_params("parallel"), name="hyena_filter")(
            fr_lanes, phase, w1a, w1b, twice(b1), bdiag(w2), twice(b2), bdiag(w3), twice(b3),
            twice(freq), w4_hi, w4_lo, deltas)


def _dft_matrices(p):
    n = 2 * p
    k = np.arange(p)[:, None].astype(np.float64)
    s = np.arange(p)[None, :].astype(np.float64)
    ang = np.pi * k * s / p
    fre = np.cos(ang)
    fim = -np.sin(ang)
    fim[0, :] = np.cos(np.pi * s[0])
    fwd_half = np.concatenate([fre, fim], axis=0)
    tt = (p + np.arange(p))[:, None].astype(np.float64)
    kk = np.arange(p)[None, :].astype(np.float64)
    ang2 = np.pi * kk * tt / p
    are = 2.0 * np.cos(ang2) / n
    are[:, 0] = 1.0 / n
    aim = -2.0 * np.sin(ang2) / n
    aim[:, 0] = np.cos(np.pi * tt[:, 0]) / n
    inv = np.concatenate([are, aim], axis=1)
    return jnp.asarray(fwd_half, _BF16), jnp.asarray(inv, _BF16)


def _filter_spectrum_kernel(g_ref, f_ref, o_ref, prev_ref):
    q = pl.program_id(1)
    a = jnp.dot(f_ref[...], g_ref[...].astype(_BF16), preferred_element_type=_F32)

    @pl.when(q > 0)
    def _():
        odd = (lax.broadcasted_iota(jnp.int32, (a.shape[0], 1), 0) & 1) == 1
        o_ref[...] = (prev_ref[...] + jnp.where(odd, -a, a)).astype(o_ref.dtype)

    prev_ref[...] = a


def _filter_spectrum(kt, fwd_half, p, tc=256):
    rows, d = kt.shape
    nblk = rows // p
    return pl.pallas_call(
        _filter_spectrum_kernel, grid=(d // tc, nblk),
        in_specs=[pl.BlockSpec((p, tc), lambda c, q: (q, c)), _const_spec(fwd_half.shape)],
        out_specs=pl.BlockSpec((None, 2 * p, tc), lambda c, q: (jnp.maximum(q - 1, 0), 0, c)),
        out_shape=jax.ShapeDtypeStruct((nblk - 1, 2 * p, d), _SPECTRUM_DTYPE),
        scratch_shapes=[pltpu.VMEM((2 * p, tc), _F32)],
        compiler_params=_params("parallel", "arbitrary"), name="hyena_filter_spectrum")(kt, fwd_half)


def _long_conv_kernel(gate_ref, z_ref, kf_ref, skip_ref, ff_ref, ai_ref, o_ref, zf_ref, yfa_ref,
                      yfb_ref, *, p, nb, tc, ne):
    ch = _CONV_ROWS
    slots = zf_ref.shape[0]
    skip = skip_ref[...]
    first_row = lax.broadcasted_iota(jnp.int32, (ch, 1), 0) == 0

    def forward(e, i):
        zb = z_ref[e, pl.ds(pl.multiple_of(i * p, p), p), :]
        zf_ref[e % slots, i] = jnp.dot(ff_ref[...], zb,
                                       preferred_element_type=_F32).astype(zf_ref.dtype)

    def spectrum(e, j, yf_ref):
        for rc in range(p // ch):
            re = slice(rc * ch, (rc + 1) * ch)
            im = slice(p + rc * ch, p + (rc + 1) * ch)
            a = bm = cc = None
            for i0 in range(0, nb, _SPECTRUM_GROUP):
                ga = jnp.zeros((ch, tc), zf_ref.dtype)
                gb = jnp.zeros((ch, tc), zf_ref.dtype)
                gc = jnp.zeros((ch, tc), zf_ref.dtype)
                for i in range(i0, min(i0 + _SPECTRUM_GROUP, nb)):
                    q = j - i + (nb - 1)
                    zre, zim = zf_ref[e % slots, i, re, :], zf_ref[e % slots, i, im, :]
                    kre, kim = kf_ref[q, re, :], kf_ref[q, im, :]
                    ga = ga + zre * kre
                    gb = gb + zim * kim
                    gc = gc + (zre * kim + zim * kre)
                if nb > _SPECTRUM_GROUP:
                    ga, gb, gc = ga.astype(_F32), gb.astype(_F32), gc.astype(_F32)
                a = ga if a is None else a + ga
                bm = gb if bm is None else bm + gb
                cc = gc if cc is None else cc + gc
            if rc == 0:
                yf_ref[re, :] = jnp.where(first_row, a, a - bm).astype(_BF16)
                yf_ref[im, :] = jnp.where(first_row, bm, cc).astype(_BF16)
            else:
                yf_ref[re, :] = (a - bm).astype(_BF16)
                yf_ref[im, :] = cc.astype(_BF16)

    def finish(e, j, yf_ref):
        y = jnp.dot(ai_ref[...], yf_ref[...], preferred_element_type=_F32)
        rows = pl.ds(pl.multiple_of(j * p, p), p)
        zt = z_ref[e, rows, :].astype(_F32)
        gt = gate_ref[e, rows, :].astype(_F32)
        o_ref[e, rows, :] = (gt * (y + zt * skip)).astype(o_ref.dtype)

    yf = (yfa_ref, yfb_ref)
    if nb <= _CONV_STATIC_BLOCKS:
        for i in range(nb):
            forward(0, i)
        pending = None
        for e in range(ne):
            for j in range(nb):
                if pending is not None:
                    finish(*pending)
                if e + 1 < ne:
                    forward(e + 1, j)
                spectrum(e, j, yf[j % 2])
                pending = (e, j, yf[j % 2])
        finish(*pending)
    else:
        assert ne == 1 and nb % 2 == 0
        for i in range(nb):
            forward(0, i)
        spectrum(0, 0, yfa_ref)

        def pair_body(k, carry):
            finish(0, 2 * k, yfa_ref)
            spectrum(0, 2 * k + 1, yfb_ref)
            finish(0, 2 * k + 1, yfb_ref)
            spectrum(0, 2 * k + 2, yfa_ref)
            return carry

        lax.fori_loop(0, nb // 2 - 1, pair_body, 0)
        finish(0, nb - 2, yfa_ref)
        spectrum(0, nb - 1, yfb_ref)
        finish(0, nb - 1, yfb_ref)


def _long_conv(gate, z, kf, skip_o, fwd_half, inv, p):
    b, l, d = z.shape
    nb = l // p
    ne = 2 if (nb <= _CONV_STATIC_BLOCKS and b % 2 == 0) else 1
    slots = min(ne, 2)
    spectra = (2 * nb - 1 + slots * nb) * 2 * p * jnp.dtype(_SPECTRUM_DTYPE).itemsize
    seq_bytes = ne * l * 2
    dft_bytes = 2 * (2 * p * p * 2)
    budget = _V7X_VMEM_LIMIT_BYTES * 9 // 10
    tc, io_buffers = 128, 2
    for cand_tc, cand_buffers in ((256, 2), (256, 1)):
        if cand_tc * (spectra + (2 + 2 * cand_buffers) * seq_bytes) + dft_bytes <= budget:
            tc, io_buffers = cand_tc, cand_buffers
            break
    io_mode = {} if io_buffers == 2 else {"pipeline_mode": pl.Buffered(1)}
    seq = pl.BlockSpec((ne, l, tc), lambda c, bi: (bi, 0, c))
    seq_io = pl.BlockSpec((ne, l, tc), lambda c, bi: (bi, 0, c), **io_mode)
    kf_spec = pl.BlockSpec((2 * nb - 1, 2 * p, tc), lambda c, bi: (0, 0, c),
                           pipeline_mode=pl.Buffered(1))
    return pl.pallas_call(
        functools.partial(_long_conv_kernel, p=p, nb=nb, tc=tc, ne=ne), grid=(d // tc, b // ne),
        in_specs=[seq_io, seq, kf_spec, pl.BlockSpec((1, tc), lambda c, bi: (0, c)),
                  _const_spec(fwd_half.shape), _const_spec(inv.shape)],
        out_specs=seq_io, out_shape=jax.ShapeDtypeStruct((b, l, d), _BF16),
        scratch_shapes=[pltpu.VMEM((slots, nb, 2 * p, tc), _SPECTRUM_DTYPE),
                        pltpu.VMEM((2 * p, tc), _BF16),
                        pltpu.VMEM((2 * p, tc), _BF16)],
        compiler_params=_params("parallel", "arbitrary"), name="hyena_long_conv")(
            gate, z, kf, skip_o.astype(_F32).reshape(1, d), fwd_half, inv)


def _hyena_layer(x, norm_g, w_in, conv_w, conv_b, w1, b1, w2, b2, w3, b3, freq, w4, skip, w_out):
    b, l, d = x.shape
    p = min(l, max(_CONV_P, l // 8))
    x1, x2, v = _hyena_in(x, norm_g, w_in.astype(_BF16), conv_w, conv_b)
    k0, k1 = _hyena_two_sided_filters(l, w1, b1, w2, b2, w3, b3, freq, w4)
    fwd_half, inv = _dft_matrices(p)
    z = v
    for o, (gate, kt) in enumerate(((x1, k0), (x2, k1))):
        kf = _filter_spectrum(kt, fwd_half, p)
        z = _long_conv(gate, z, kf, skip[o], fwd_half, inv, p)
    return z.reshape(b * l, d), w_out.astype(_BF16)


def _trunk(x, rel_bias, ffn1_norm, ffn1_w, mixer_norm, attn_w_qkv, attn_w_o, hy, ffn2_norm, ffn2_w,
           final_norm):
    b, s, d = x.shape
    depth = ffn1_norm.shape[0]
    flat = lambda a: a.reshape(b * s, d)
    for i in range(depth):
        x = _ffn(flat(x), ffn1_norm[i], *ffn1_w[i]).reshape(b, s, d)
        j = i // 2
        if i % 2 == 0:
            os_, stats, w_o = _dilated_attention_layer(x, mixer_norm[i], attn_w_qkv[j], attn_w_o[j],
                                                       rel_bias)
            pre = ("merge", os_, stats, s, w_o)
        else:
            pre = ("matmul", *_hyena_layer(x, mixer_norm[i], *[a[j] for a in hy]))
        fin = final_norm if i == depth - 1 else None
        x = _ffn(flat(x), ffn2_norm[i], *ffn2_w[i], pre=pre, final_g=fin).reshape(b, s, d)
    return x


def kernel(x_prompt, x_sample, rel_bias, ffn1_norm, ffn1_w_gate_up, ffn1_w_down, mixer_norm, attn_w_qkv, attn_w_o, hyena_w_in, hyena_conv_w, hyena_conv_b, hyena_filt_w1, hyena_filt_b1, hyena_filt_w2, hyena_filt_b2, hyena_filt_w3, hyena_filt_b3, hyena_filt_freq, hyena_filt_w4, hyena_skip, hyena_w_out, ffn2_norm, ffn2_w_gate_up, ffn2_w_down, final_norm):
    depth = ffn1_norm.shape[0]
    ffn1_w = [_ffn_weights(ffn1_w_gate_up[i], ffn1_w_down[i]) for i in range(depth)]
    ffn2_w = [_ffn_weights(ffn2_w_gate_up[i], ffn2_w_down[i]) for i in range(depth)]
    hy = (hyena_w_in, hyena_conv_w, hyena_conv_b, hyena_filt_w1, hyena_filt_b1, hyena_filt_w2,
          hyena_filt_b2, hyena_filt_w3, hyena_filt_b3, hyena_filt_freq, hyena_filt_w4, hyena_skip,
          hyena_w_out)
    args = (rel_bias, ffn1_norm, ffn1_w, mixer_norm, attn_w_qkv, attn_w_o, hy, ffn2_norm, ffn2_w,
            final_norm)
    return (_trunk(x_prompt, *args), _trunk(x_sample, *args))
```

```python
import functools
import math

import numpy as np
import jax
import jax.numpy as jnp
from jax import lax
from jax.experimental import pallas as pl
from jax.experimental.pallas import tpu as pltpu

D_MODEL = 1024
HEAD_DIM = 64
HEADS_PER_GROUP = D_MODEL // HEAD_DIM
DILATED_GROUPS = ((128, 1), (512, 4), (2048, 16))
N_GROUPS = len(DILATED_GROUPS)
NUM_BUCKETS = 32
MAX_DISTANCE = 1024
HYENA_EMB_DIM = 33
HYENA_FILTER_WIDTH = 64
FAST_DECAY_PCT = 0.3
SLOW_DECAY_PCT = 1.5
DECAY_TARGET = 1e-2
D_FF = 2816
RMS_EPS = 1e-6

_F32 = jnp.float32
_BF16 = jnp.bfloat16
_V7X_VMEM_LIMIT_BYTES = 58 * 1024 * 1024
_LANES = 128
_NEG = -1e30
_LOG2E = math.log2(math.e)

_TQ = 128
_HALF_W = 64
_TK = _TQ + 2 * _HALF_W
_ATTN_MAX_SUBTILES = 8
_MXU_DIM = 256
_CONV_P = 512
_CONV_ROWS = 32
_SPECTRUM_DTYPE = jnp.bfloat16
_SPECTRUM_GROUP = 4
_CONV_STATIC_BLOCKS = 4


def _params(*sem):
    return pltpu.CompilerParams(dimension_semantics=sem, vmem_limit_bytes=_V7X_VMEM_LIMIT_BYTES)


def _const_spec(shape):
    nd = len(shape)
    return pl.BlockSpec(shape, lambda *_: (0,) * nd, pipeline_mode=pl.Buffered(1))


def _rms(x, g):
    y = x * lax.rsqrt(jnp.mean(x * x, axis=-1, keepdims=True) + RMS_EPS)
    return y * g


def _merge_rows(o_refs, s_refs, e_ref, of_ref, lf_ref, h, sub):
    nblk = D_MODEL // _LANES

    def natural(o_ref, s_ref, slot):
        d = o_ref.shape[0]
        n = sub // d
        cls = slice(h * n, (h + 1) * n)
        if d == 1:
            return o_ref[0, cls, :].astype(_F32), s_ref[0, cls, :]
        for r in range(d):
            o_r = o_ref[r, cls, :].astype(_F32)
            for c in range(nblk):
                of_ref[slot, c, pl.ds(r, n, stride=d), :] = o_r[:, c * _LANES:(c + 1) * _LANES]
            lf_ref[slot, pl.ds(r, n, stride=d), :] = s_ref[r, cls, :]
        return jnp.concatenate([of_ref[slot, c] for c in range(nblk)], axis=1), lf_ref[slot]

    outs, stats = [], []
    for g, (o_ref, s_ref) in enumerate(zip(o_refs, s_refs)):
        o, st = natural(o_ref, s_ref, 2 * h + (g % 2))
        outs.append(o)
        stats.append(st)
    head = lax.broadcasted_iota(jnp.int32, (1, _LANES), 1) < HEADS_PER_GROUP
    ms = [jnp.where(head, st, 0.0) for st in stats]
    ls = [jnp.where(head, pltpu.roll(st, _LANES - HEADS_PER_GROUP, 1), 1.0) for st in stats]
    m = functools.reduce(jnp.maximum, ms)
    ws = [jnp.exp2(mg - m) for mg in ms]
    inv = 1.0 / sum(w * l for w, l in zip(ws, ls))
    merged = None
    for w, o in zip(ws, outs):
        al = w * inv
        hi = al.astype(_BF16)
        lo = (al - hi.astype(_F32)).astype(_BF16)
        term = jnp.dot(jnp.concatenate([hi, lo], axis=1), e_ref[...], preferred_element_type=_F32) * o
        merged = term if merged is None else merged + term
    return merged


def _ffn_kernel(*refs, pre, final, sub):
    refs = list(refs)
    x_ref, g_ref, wgu_ref, wd_ref = refs[:4]
    extra = refs[4:]
    if pre == "matmul":
        a_ref, wpre_ref = extra[:2]
        extra = extra[2:]
    elif pre == "merge":
        o_refs, s_refs, (e_ref, wpre_ref) = extra[0:3], extra[3:6], extra[6:8]
        extra = extra[8:]
    gf_ref = extra.pop(0) if final else None
    o_ref = extra.pop(0)
    f = wd_ref.shape[0]
    n_tiles = f // _MXU_DIM
    bounds = [0, (n_tiles + 1) // 2 * _MXU_DIM, f]
    for h in range(x_ref.shape[0] // sub):
        rows = slice(h * sub, (h + 1) * sub)
        x = x_ref[rows, :]
        if pre == "matmul":
            x = x + jnp.dot(a_ref[rows, :], wpre_ref[...], preferred_element_type=_F32)
        elif pre == "merge":
            merged = _merge_rows(o_refs, s_refs, e_ref, extra[0], extra[1], h, sub)
            x = x + jnp.dot(merged.astype(_BF16), wpre_ref[...], preferred_element_type=_F32)
        xb = _rms(x, g_ref[...]).astype(_BF16)
        acc = jnp.zeros(x.shape, _F32)
        for lo, hi in zip(bounds[:-1], bounds[1:]):
            gate = jnp.dot(xb, wgu_ref[:, lo:hi], preferred_element_type=_F32)
            up = jnp.dot(xb, wgu_ref[:, f + lo:f + hi], preferred_element_type=_F32)
            act = (gate * jax.nn.sigmoid(gate)) * up
            acc = acc + jnp.dot(act.astype(_BF16), wd_ref[lo:hi, :], preferred_element_type=_F32)
        y = x + 0.5 * acc
        if final:
            y = _rms(y, gf_ref[...])
        o_ref[rows, :] = y


def _ffn(x2d, norm_g, wgu, wd, pre=None, final_g=None, tm=1024, sub=512):
    t, d = x2d.shape
    kind = None if pre is None else pre[0]
    if kind == "merge":
        tm = sub
    row = lambda w: pl.BlockSpec((tm, w), lambda i: (i, 0))
    in_specs = [row(d), _const_spec((1, d)), _const_spec(wgu.shape), _const_spec(wd.shape)]
    args = [x2d, norm_g.reshape(1, d), wgu, wd]
    scratch = []
    if kind == "matmul":
        _, a2d, w_pre = pre
        in_specs += [row(a2d.shape[1]), _const_spec(w_pre.shape)]
        args += [a2d, w_pre]
    elif kind == "merge":
        _, os_, stats, seq_len, w_pre = pre
        nt = seq_len // tm
        e = _head_expand_matrix()

        def cls(a):
            dil, w = a.shape[1], a.shape[3]
            return pl.BlockSpec((None, dil, tm // dil, w), lambda i: (i // nt, 0, i % nt, 0))

        in_specs += [cls(a) for a in os_] + [cls(a) for a in stats]
        in_specs += [_const_spec(e.shape), _const_spec(w_pre.shape)]
        args += [*os_, *stats, e, w_pre]
        n_slots = 2 * (tm // sub)
        scratch = [pltpu.VMEM((n_slots, d // _LANES, sub, _LANES), _F32),
                   pltpu.VMEM((n_slots, sub, _LANES), _F32)]
    if final_g is not None:
        in_specs.append(_const_spec((1, d)))
        args.append(final_g.reshape(1, d))
    return pl.pallas_call(
        functools.partial(_ffn_kernel, pre=kind, final=final_g is not None, sub=sub),
        grid=(t // tm,), in_specs=in_specs, out_specs=row(d),
        out_shape=jax.ShapeDtypeStruct((t, d), _F32), scratch_shapes=scratch,
        compiler_params=_params("parallel"), name="ffn")(*args)


def _ffn_weights(w_gate_up, w_down):
    return w_gate_up.astype(_BF16), w_down.astype(_BF16)


def _t5_bucket(rel):
    nb = NUM_BUCKETS // 2
    max_exact = nb // 2
    ret = (rel > 0).astype(np.int32) * nb
    n = np.abs(rel)
    large = max_exact + (np.log(np.maximum(n, 1) / max_exact)
                         / math.log(MAX_DISTANCE / max_exact) * (nb - max_exact)).astype(np.int32)
    large = np.minimum(large, nb - 1)
    return ret + np.where(n < max_exact, n, large)


def _attn_bucket_tables(dilation):
    delta = np.arange(_TK)[None, :] - _HALF_W - np.arange(_TQ)[:, None]
    bucket = _t5_bucket(delta * dilation)
    band = np.abs(delta) <= _HALF_W
    kk = np.arange(_TK)[None, :]
    tables = []
    for variant in range(4):
        ok = band
        if variant & 1:
            ok = ok & (kk >= _HALF_W)
        if variant & 2:
            ok = ok & (kk < _TQ + _HALF_W)
        tables.append(np.where(ok, bucket, -1))
    return jnp.asarray(np.stack(tables), jnp.int32)


def _bias_tiles_kernel(rb_ref, bucket_ref, o_ref):
    hp = pl.program_id(1)
    bucket = bucket_ref[...]
    for hh in range(2):
        acc = jnp.full((_TQ, _TK), _NEG, _F32)
        for bkt in range(NUM_BUCKETS):
            acc = jnp.where(bucket == bkt, rb_ref[2 * hp + hh, bkt] * _LOG2E, acc)
        o_ref[hh * _TQ:(hh + 1) * _TQ, :] = acc


def _attn_bias_tiles(rel_bias_g, dilation):
    tables = _attn_bucket_tables(dilation)
    npair = HEADS_PER_GROUP // 2
    return pl.pallas_call(
        _bias_tiles_kernel, grid=(4, npair),
        in_specs=[pl.BlockSpec(memory_space=pltpu.SMEM),
                  pl.BlockSpec((None, _TQ, _TK), lambda v, hp: (v, 0, 0))],
        out_specs=pl.BlockSpec((None, None, 2 * _TQ, _TK), lambda v, hp: (v, hp, 0, 0)),
        out_shape=jax.ShapeDtypeStruct((4, npair, 2 * _TQ, _TK), _F32),
        compiler_params=_params("parallel", "parallel"), name="attn_bias_tiles")(
            rel_bias_g.astype(_F32).T, tables)


def _attn_kernel(q_ref, kp_ref, kc_ref, kn_ref, vp_ref, vc_ref, vn_ref, bias_ref,
                 o_ref, stat_ref, *, nr, nsub):
    i = pl.program_id(2)
    is_first = jnp.where(i == 0, 1, 0)
    is_last = jnp.where(i == pl.num_programs(2) - 1, 2, 0)
    lane = lax.broadcasted_iota(jnp.int32, (1, _LANES), 1)
    low = lane < HEAD_DIM
    zero = jnp.zeros((), _BF16)
    cur_rows = nsub * _TQ

    def window(prev_ref, cur_ref, next_ref, ri, sub, sl):
        start, end = sub * _TQ - _HALF_W, (sub + 1) * _TQ + _HALF_W
        pieces = []
        if start < 0:
            pieces.append(prev_ref[ri, _TQ - _HALF_W:, sl])
        pieces.append(cur_ref[ri, max(start, 0):min(end, cur_rows), sl])
        if end > cur_rows:
            pieces.append(next_ref[ri, :_HALF_W, sl])
        return jnp.concatenate(pieces, axis=0)

    for ri in range(nr):
        for sub in range(nsub):
            rows = slice(sub * _TQ, (sub + 1) * _TQ)
            variant = (is_first if sub == 0 else 0) + (is_last if sub == nsub - 1 else 0)
            stat = jnp.zeros((_TQ, _LANES), _F32)
            for hp in range(HEADS_PER_GROUP // 2):
                sl = slice(hp * _LANES, (hp + 1) * _LANES)
                q2 = q_ref[ri, rows, sl]
                qq = jnp.concatenate([jnp.where(low, q2, zero), jnp.where(low, zero, q2)], axis=0)
                k2 = window(kp_ref, kc_ref, kn_ref, ri, sub, sl)
                v2 = window(vp_ref, vc_ref, vn_ref, ri, sub, sl)
                s = lax.dot_general(qq, k2, (((1,), (1,)), ((), ())),
                                    preferred_element_type=_F32)
                s = s + bias_ref[variant, hp]
                m = jnp.max(s, axis=-1, keepdims=True)
                p = jnp.exp2(s - m)
                l = jnp.sum(p, axis=-1, keepdims=True)
                pv = jnp.dot(p.astype(_BF16), v2, preferred_element_type=_F32)
                o_ref[ri, rows, sl] = jnp.where(low, pv[:_TQ], pv[_TQ:]).astype(o_ref.dtype)
                h0, h1 = 2 * hp, 2 * hp + 1
                stat = jnp.where(lane == h0, m[:_TQ], jnp.where(lane == h1, m[_TQ:], stat))
                stat = jnp.where(lane == HEADS_PER_GROUP + h0, l[:_TQ],
                                 jnp.where(lane == HEADS_PER_GROUP + h1, l[_TQ:], stat))
            stat_ref[ri, rows, :] = stat


def _qkv_kernel(x_ref, g_ref, w_ref, *rest, dilations, tm):
    o_refs = rest[:len(dilations)]
    xn_ref, xp_ref = rest[len(dilations):]
    nblk = D_MODEL // _LANES
    xn = _rms(x_ref[...], g_ref[...])
    for c in range(nblk):
        xn_ref[c] = xn[:, c * _LANES:(c + 1) * _LANES]
    for g, (d, o_ref) in enumerate(zip(dilations, o_refs)):
        rows = tm // d
        if d == 1:
            xp = xn.astype(_BF16)
        else:
            for r in range(d):
                for c in range(nblk):
                    xp_ref[g % 2, r * rows:(r + 1) * rows, c * _LANES:(c + 1) * _LANES] = (
                        xn_ref[c, pl.ds(r, rows, stride=d), :].astype(_BF16))
            xp = xp_ref[g % 2]
        for j in range(3):
            cs = slice(j * D_MODEL, (j + 1) * D_MODEL)
            y = jnp.dot(xp, w_ref[g, :, cs], preferred_element_type=_F32).astype(o_ref.dtype)
            for r in range(d):
                o_ref[r, :, cs] = y[r * rows:(r + 1) * rows]


def _qkv(x, norm_g, w_bf16, dilations, tm=512):
    b, s, dm = x.shape
    n3 = w_bf16.shape[2]
    return pl.pallas_call(
        functools.partial(_qkv_kernel, dilations=dilations, tm=tm), grid=(b, s // tm),
        in_specs=[pl.BlockSpec((None, tm, dm), lambda bi, i: (bi, i, 0)), _const_spec((1, dm)),
                  _const_spec(w_bf16.shape)],
        out_specs=[pl.BlockSpec((None, d, tm // d, n3), lambda bi, i: (bi, 0, i, 0))
                   for d in dilations],
        out_shape=[jax.ShapeDtypeStruct((b, d, s // d, n3), _BF16) for d in dilations],
        scratch_shapes=[pltpu.VMEM((dm // _LANES, tm, _LANES), _F32),
                        pltpu.VMEM((2, tm, dm), _BF16)],
        compiler_params=_params("parallel", "parallel"), name="qkv")(
            x, norm_g.reshape(1, dm), w_bf16)


def _group_attention(qkv, bias_tiles):
    b, d, lc, _ = qkv.shape
    n_tq = lc // _TQ
    nsub = min(_ATTN_MAX_SUBTILES, n_tq)
    nr = min(d, _ATTN_MAX_SUBTILES // nsub)
    rows = nsub * _TQ

    def cur(which):
        return pl.BlockSpec((None, nr, rows, D_MODEL), lambda bi, r, i: (bi, r, i, which))

    def prev(which):
        return pl.BlockSpec((None, nr, _TQ, D_MODEL),
                            lambda bi, r, i: (bi, r, jnp.maximum(i * nsub - 1, 0), which))

    def nxt(which):
        return pl.BlockSpec((None, nr, _TQ, D_MODEL),
                            lambda bi, r, i: (bi, r, jnp.minimum((i + 1) * nsub, n_tq - 1), which))

    return pl.pallas_call(
        functools.partial(_attn_kernel, nr=nr, nsub=nsub), grid=(b, d // nr, n_tq // nsub),
        in_specs=[cur(0), prev(1), cur(1), nxt(1), prev(2), cur(2), nxt(2),
                  _const_spec(bias_tiles.shape)],
        out_specs=[pl.BlockSpec((None, nr, rows, D_MODEL), lambda bi, r, i: (bi, r, i, 0)),
                   pl.BlockSpec((None, nr, rows, _LANES), lambda bi, r, i: (bi, r, i, 0))],
        out_shape=[jax.ShapeDtypeStruct((b, d, lc, D_MODEL), _BF16),
                   jax.ShapeDtypeStruct((b, d, lc, _LANES), _F32)],
        compiler_params=_params("parallel", "parallel", "arbitrary"), name=f"attn_d{d}")(
            qkv, qkv, qkv, qkv, qkv, qkv, qkv, bias_tiles)


def _head_expand_matrix():
    e = np.zeros((2 * _LANES, D_MODEL), np.float32)
    for h in range(HEADS_PER_GROUP):
        e[h, h * HEAD_DIM:(h + 1) * HEAD_DIM] = 1.0
        e[_LANES + h, h * HEAD_DIM:(h + 1) * HEAD_DIM] = 1.0
    return jnp.asarray(e, _BF16)


def _dilated_attention_layer(x, norm_g, w_qkv, w_o, rel_bias):
    n3 = 3 * D_MODEL
    dilations = tuple(dil for _, dil in DILATED_GROUPS)
    assert all((window // 2) // dil == _HALF_W for window, dil in DILATED_GROUPS)
    col_scale = jnp.where(jnp.arange(n3) < D_MODEL, _LOG2E * HEAD_DIM ** -0.5, 1.0).astype(_F32)
    w_groups = w_qkv.astype(_F32).reshape(D_MODEL, N_GROUPS, n3).transpose(1, 0, 2) * col_scale
    qkvs = _qkv(x, norm_g, w_groups.astype(_BF16), dilations)
    os_, stats = [], []
    for g, (dil, qkv) in enumerate(zip(dilations, qkvs)):
        bias = _attn_bias_tiles(rel_bias[:, g * HEADS_PER_GROUP:(g + 1) * HEADS_PER_GROUP], dil)
        o, stat = _group_attention(qkv, bias)
        os_.append(o)
        stats.append(stat)
    return os_, stats, w_o.astype(_BF16)


def _hyena_in_kernel(xp_ref, x_ref, xn_ref, g_ref, w_ref, cw_ref, cb_ref,
                     x1_ref, x2_ref, v_ref, *, tm, sub):
    i = pl.program_id(1)
    last = pl.num_programs(1) - 1
    keep_prev = jnp.where(i > 0, 1.0, 0.0)
    keep_next = jnp.where(i < last, 1.0, 0.0)
    xe = jnp.concatenate([xp_ref[...] * keep_prev, x_ref[...], xn_ref[...] * keep_next], axis=0)
    n = sub + 16
    for h in range(tm // sub):
        xb = _rms(xe[h * sub:h * sub + n], g_ref[...]).astype(_BF16)
        rows = slice(h * sub, (h + 1) * sub)
        for j, o_ref in enumerate((x1_ref, x2_ref, v_ref)):
            cs = slice(j * D_MODEL, (j + 1) * D_MODEL)
            u = jnp.dot(xb, w_ref[:, cs], preferred_element_type=_F32)
            u_prev = pltpu.roll(u, 1, 0)[8:sub + 8]
            u_next = pltpu.roll(u, n - 1, 0)[8:sub + 8]
            cw = cw_ref[:, cs]
            y = (u_prev * cw[0:1] + u[8:sub + 8] * cw[1:2] + u_next * cw[2:3]) + cb_ref[:, cs]
            o_ref[rows, :] = y.astype(o_ref.dtype)


def _hyena_in(x, norm_g, w_in_bf16, conv_w, conv_b, tm=1024, sub=512):
    b, s, d = x.shape
    n3 = w_in_bf16.shape[1]
    nblk8 = s // 8
    r8 = tm // 8
    main = pl.BlockSpec((None, tm, d), lambda bi, i: (bi, i, 0))
    prev = pl.BlockSpec((None, 8, d), lambda bi, i: (bi, jnp.maximum(i * r8 - 1, 0), 0))
    nxt = pl.BlockSpec((None, 8, d), lambda bi, i: (bi, jnp.minimum((i + 1) * r8, nblk8 - 1), 0))
    out = jax.ShapeDtypeStruct((b, s, d), _BF16)
    return pl.pallas_call(
        functools.partial(_hyena_in_kernel, tm=tm, sub=sub), grid=(b, s // tm),
        in_specs=[prev, main, nxt, _const_spec((1, d)), _const_spec(w_in_bf16.shape),
                  _const_spec((3, n3)), _const_spec((1, n3))],
        out_specs=[main, main, main], out_shape=[out, out, out],
        compiler_params=_params("parallel", "arbitrary"), name="hyena_in")(
            x, x, x, norm_g.reshape(1, d), w_in_bf16, conv_w.astype(_F32),
            conv_b.astype(_F32).reshape(1, n3))


def _filter_kernel(fr_ref, ph_ref, w1a_ref, w1b_ref, b1_ref, w2_ref, b2_ref, w3_ref, b3_ref, fq_ref,
                   w4h_ref, w4l_ref, dl_ref, k0_ref, k1_ref, *, seq_len, tr):
    hi = lax.Precision.HIGHEST
    half = tr // 2
    r = pl.program_id(0) * tr + lax.broadcasted_iota(jnp.int32, (tr, 1), 0)
    pos = jnp.abs(r - seq_len).astype(_F32)
    t = pos / (seq_len - 1.0)
    wpos = (2.0 * math.pi) * pos / seq_len
    lane = lax.broadcasted_iota(jnp.int32, (1, _LANES), 1)
    z = jnp.where(lane == 0, t,
                  jnp.where(lane < HYENA_EMB_DIM, jnp.cos(fr_ref[...] * wpos + ph_ref[...]), 0.0))
    dot = functools.partial(jnp.dot, precision=hi, preferred_element_type=_F32)
    fq = fq_ref[...]
    h = jnp.sin(fq * (dot(z[:half], w1a_ref[...]) + dot(z[half:], w1b_ref[...]) + b1_ref[...]))
    h = jnp.sin(fq * (dot(h, w2_ref[...]) + b2_ref[...]))
    h = jnp.sin(fq * (dot(h, w3_ref[...]) + b3_ref[...]))
    decay = jnp.exp(-t * jnp.abs(dl_ref[...]))
    live = r > 0
    h_hi = h.astype(_BF16)
    h_lo = (h - h_hi.astype(_F32)).astype(_BF16)
    for o, k_ref in enumerate((k0_ref, k1_ref)):
        cs = slice(o * D_MODEL, (o + 1) * D_MODEL)
        for part in range(2):
            w_hi = w4h_ref[part, :, cs]
            y = jnp.dot(h_hi, w_hi, preferred_element_type=_F32)
            y = y + jnp.dot(h_hi, w4l_ref[part, :, cs], preferred_element_type=_F32)
            y = y + jnp.dot(h_lo, w_hi, preferred_element_type=_F32)
            rows = slice(part * half, (part + 1) * half)
            k_ref[rows, :] = jnp.where(live[rows], y * decay[rows], 0.0)


def _hyena_two_sided_filters(seq_len, w1, b1, w2, b2, w3, b3, freq, w4, tr=512):
    fw = HYENA_FILTER_WIDTH
    assert 2 * fw == _LANES
    n_tiles = 2 * seq_len // tr
    f32 = lambda a: a.astype(_F32)
    zeros = jnp.zeros((fw, 2 * D_MODEL), _F32)
    w4d = f32(w4).reshape(fw, 2, 2, D_MODEL).transpose(2, 0, 1, 3).reshape(2, fw, 2 * D_MODEL)
    w4p = jnp.stack([jnp.stack([jnp.concatenate([w4d[dr], zeros]), jnp.concatenate([zeros, w4d[dr]])])
                     for dr in range(2)])
    w4_hi = w4p.astype(_BF16)
    w4_lo = (w4p - w4_hi.astype(_F32)).astype(_BF16)
    w4_spec = pl.BlockSpec((None, 2, _LANES, 2 * D_MODEL),
                           lambda i: (jnp.where(i >= n_tiles // 2, 0, 1), 0, 0, 0))
    bands = (HYENA_EMB_DIM - 1) // 2
    fr = jnp.linspace(1e-4, bands - 1, bands, dtype=_F32)
    fr_lanes = jnp.zeros((1, _LANES), _F32).at[0, 1:1 + bands].set(fr).at[0, 1 + bands:1 + 2 * bands].set(fr)
    phase = jnp.zeros((1, _LANES), _F32).at[0, 1 + bands:1 + 2 * bands].set(0.5 * math.pi)
    w1p = jnp.zeros((_LANES, fw), _F32).at[:HYENA_EMB_DIM].set(f32(w1))
    z64 = jnp.zeros((_LANES, fw), _F32)
    w1a = jnp.concatenate([w1p, z64], axis=1)
    w1b = jnp.concatenate([z64, w1p], axis=1)
    zz = jnp.zeros((fw, fw), _F32)
    bdiag = lambda w: jnp.block([[f32(w), zz], [zz, f32(w)]])
    twice = lambda a: jnp.tile(f32(a).reshape(1, fw), (1, 2))
    max_decay = math.log(DECAY_TARGET) / FAST_DECAY_PCT
    min_decay = math.log(DECAY_TARGET) / SLOW_DECAY_PCT
    deltas = jnp.linspace(min_decay, max_decay, D_MODEL, dtype=_F32).reshape(1, D_MODEL)
    out = jax.ShapeDtypeStruct((2 * seq_len, D_MODEL), _F32)
    cs = _const_spec
    sq, vec = cs((_LANES, _LANES)), cs((1, _LANES))
    return pl.pallas_call(
        functools.partial(_filter_kernel, seq_len=seq_len, tr=tr), grid=(n_tiles,),
        in_specs=[vec, vec, sq, sq, vec, sq, vec, sq, vec, vec, w4_spec, w4_spec, cs((1, D_MODEL))],
        out_specs=[pl.BlockSpec((tr, D_MODEL), lambda i: (i, 0))] * 2, out_shape=[out, out],
        compiler_params=_params("parallel"), name="hyena_filter")(
            fr_lanes, phase, w1a, w1b, twice(b1), bdiag(w2), twice(b2), bdiag(w3), twice(b3),
            twice(freq), w4_hi, w4_lo, deltas)


def _dft_matrices(p):
    n = 2 * p
    k = np.arange(p)[:, None].astype(np.float64)
    s = np.arange(p)[None, :].astype(np.float64)
    ang = np.pi * k * s / p
    fre = np.cos(ang)
    fim = -np.sin(ang)
    fim[0, :] = np.cos(np.pi * s[0])
    fwd_half = np.concatenate([fre, fim], axis=0)
    tt = (p + np.arange(p))[:, None].astype(np.float64)
    kk = np.arange(p)[None, :].astype(np.float64)
    ang2 = np.pi * kk * tt / p
    are = 2.0 * np.cos(ang2) / n
    are[:, 0] = 1.0 / n
    aim = -2.0 * np.sin(ang2) / n
    aim[:, 0] = np.cos(np.pi * tt[:, 0]) / n
    inv = np.concatenate([are, aim], axis=1)
    return jnp.asarray(fwd_half, _BF16), jnp.asarray(inv, _BF16)


def _filter_spectrum_kernel(g0_ref, ga_ref, gb_ref, f_ref, o_ref, prev_ref):
    spectrum = lambda g_ref: jnp.dot(f_ref[...], g_ref[...].astype(_BF16),
                                     preferred_element_type=_F32)

    @pl.when(pl.program_id(1) == 0)
    def _():
        prev_ref[...] = spectrum(g0_ref)

    aa, ab = spectrum(ga_ref), spectrum(gb_ref)
    odd = (lax.broadcasted_iota(jnp.int32, (aa.shape[0], 1), 0) & 1) == 1
    o_ref[0] = (prev_ref[...] + jnp.where(odd, -aa, aa)).astype(o_ref.dtype)
    o_ref[1] = (aa + jnp.where(odd, -ab, ab)).astype(o_ref.dtype)
    prev_ref[...] = ab


def _filter_spectrum(kt, fwd_half, p, tc=256):
    rows, d = kt.shape
    nblk = rows // p
    blk = lambda imap: pl.BlockSpec((p, tc), imap)
    return pl.pallas_call(
        _filter_spectrum_kernel, grid=(d // tc, nblk // 2),
        in_specs=[blk(lambda c, s: (0, c)), blk(lambda c, s: (2 * s + 1, c)),
                  blk(lambda c, s: (jnp.minimum(2 * s + 2, nblk - 1), c)),
                  _const_spec(fwd_half.shape)],
        out_specs=pl.BlockSpec((2, 2 * p, tc), lambda c, s: (s, 0, c)),
        out_shape=jax.ShapeDtypeStruct((nblk, 2 * p, d), _SPECTRUM_DTYPE),
        scratch_shapes=[pltpu.VMEM((2 * p, tc), _F32)],
        compiler_params=_params("parallel", "arbitrary"), name="hyena_filter_spectrum")(
            kt, kt, kt, fwd_half)


def _long_conv_kernel(gate_ref, z_ref, kf_ref, skip_ref, ff_ref, ai_ref, o_ref, zf_ref, yfa_ref,
                      yfb_ref, *, p, nb, tc, ne):
    ch = _CONV_ROWS
    slots = zf_ref.shape[0]
    skip = skip_ref[...]
    first_row = lax.broadcasted_iota(jnp.int32, (ch, 1), 0) == 0

    def forward(e, i):
        zb = z_ref[e, pl.ds(pl.multiple_of(i * p, p), p), :]
        zf_ref[e % slots, i] = jnp.dot(ff_ref[...], zb,
                                       preferred_element_type=_F32).astype(zf_ref.dtype)

    def spectrum(e, j, yf_ref):
        for rc in range(p // ch):
            re = slice(rc * ch, (rc + 1) * ch)
            im = slice(p + rc * ch, p + (rc + 1) * ch)
            a = bm = cc = None
            for i0 in range(0, nb, _SPECTRUM_GROUP):
                ga = jnp.zeros((ch, tc), zf_ref.dtype)
                gb = jnp.zeros((ch, tc), zf_ref.dtype)
                gc = jnp.zeros((ch, tc), zf_ref.dtype)
                for i in range(i0, min(i0 + _SPECTRUM_GROUP, nb)):
                    q = j - i + (nb - 1)
                    zre, zim = zf_ref[e % slots, i, re, :], zf_ref[e % slots, i, im, :]
                    kre, kim = kf_ref[q, re, :], kf_ref[q, im, :]
                    ga = ga + zre * kre
                    gb = gb + zim * kim
                    gc = gc + (zre * kim + zim * kre)
                if nb > _SPECTRUM_GROUP:
                    ga, gb, gc = ga.astype(_F32), gb.astype(_F32), gc.astype(_F32)
                a = ga if a is None else a + ga
                bm = gb if bm is None else bm + gb
                cc = gc if cc is None else cc + gc
            if rc == 0:
                yf_ref[re, :] = jnp.where(first_row, a, a - bm).astype(_BF16)
                yf_ref[im, :] = jnp.where(first_row, bm, cc).astype(_BF16)
            else:
                yf_ref[re, :] = (a - bm).astype(_BF16)
                yf_ref[im, :] = cc.astype(_BF16)

    def finish(e, j, yf_ref):
        y = jnp.dot(ai_ref[...], yf_ref[...], preferred_element_type=_F32)
        rows = pl.ds(pl.multiple_of(j * p, p), p)
        zt = z_ref[e, rows, :].astype(_F32)
        gt = gate_ref[e, rows, :].astype(_F32)
        o_ref[e, rows, :] = (gt * (y + zt * skip)).astype(o_ref.dtype)

    yf = (yfa_ref, yfb_ref)
    if nb <= _CONV_STATIC_BLOCKS:
        for i in range(nb):
            forward(0, i)
        pending = None
        for e in range(ne):
            for j in range(nb):
                if pending is not None:
                    finish(*pending)
                if e + 1 < ne:
                    forward(e + 1, j)
                spectrum(e, j, yf[j % 2])
                pending = (e, j, yf[j % 2])
        finish(*pending)
    else:
        assert ne == 1 and nb % 2 == 0
        for i in range(nb):
            forward(0, i)
        spectrum(0, 0, yfa_ref)

        def pair_body(k, carry):
            finish(0, 2 * k, yfa_ref)
            spectrum(0, 2 * k + 1, yfb_ref)
            finish(0, 2 * k + 1, yfb_ref)
            spectrum(0, 2 * k + 2, yfa_ref)
            return carry

        lax.fori_loop(0, nb // 2 - 1, pair_body, 0)
        finish(0, nb - 2, yfa_ref)
        spectrum(0, nb - 1, yfb_ref)
        finish(0, nb - 1, yfb_ref)


def _long_conv(gate, z, kf, skip_o, fwd_half, inv, p):
    b, l, d = z.shape
    nb = l // p
    ne = 2 if (nb <= _CONV_STATIC_BLOCKS and b % 2 == 0) else 1
    slots = min(ne, 2)
    spectra = (2 * nb - 1 + slots * nb) * 2 * p * jnp.dtype(_SPECTRUM_DTYPE).itemsize
    seq_bytes = ne * l * 2
    dft_bytes = 2 * (2 * p * p * 2)
    budget = _V7X_VMEM_LIMIT_BYTES * 9 // 10
    tc, io_buffers = 128, 2
    for cand_tc, cand_buffers in ((256, 2), (256, 1)):
        if cand_tc * (spectra + (2 + 2 * cand_buffers) * seq_bytes) + dft_bytes <= budget:
            tc, io_buffers = cand_tc, cand_buffers
            break
    io_mode = {} if io_buffers == 2 else {"pipeline_mode": pl.Buffered(1)}
    seq = pl.BlockSpec((ne, l, tc), lambda c, bi: (bi, 0, c))
    seq_io = pl.BlockSpec((ne, l, tc), lambda c, bi: (bi, 0, c), **io_mode)
    kf_spec = pl.BlockSpec((2 * nb - 1, 2 * p, tc), lambda c, bi: (0, 0, c),
                           pipeline_mode=pl.Buffered(1))
    return pl.pallas_call(
        functools.partial(_long_conv_kernel, p=p, nb=nb, tc=tc, ne=ne), grid=(d // tc, b // ne),
        in_specs=[seq_io, seq, kf_spec, pl.BlockSpec((1, tc), lambda c, bi: (0, c)),
                  _const_spec(fwd_half.shape), _const_spec(inv.shape)],
        out_specs=seq_io, out_shape=jax.ShapeDtypeStruct((b, l, d), _BF16),
        scratch_shapes=[pltpu.VMEM((slots, nb, 2 * p, tc), _SPECTRUM_DTYPE),
                        pltpu.VMEM((2 * p, tc), _BF16),
                        pltpu.VMEM((2 * p, tc), _BF16)],
        compiler_params=_params("parallel", "arbitrary"), name="hyena_long_conv")(
            gate, z, kf, skip_o.astype(_F32).reshape(1, d), fwd_half, inv)


def _hyena_layer(x, norm_g, w_in, conv_w, conv_b, w1, b1, w2, b2, w3, b3, freq, w4, skip, w_out):
    b, l, d = x.shape
    p = min(l, max(_CONV_P, l // 8))
    x1, x2, v = _hyena_in(x, norm_g, w_in.astype(_BF16), conv_w, conv_b)
    k0, k1 = _hyena_two_sided_filters(l, w1, b1, w2, b2, w3, b3, freq, w4)
    fwd_half, inv = _dft_matrices(p)
    z = v
    for o, (gate, kt) in enumerate(((x1, k0), (x2, k1))):
        kf = _filter_spectrum(kt, fwd_half, p)
        z = _long_conv(gate, z, kf, skip[o], fwd_half, inv, p)
    return z.reshape(b * l, d), w_out.astype(_BF16)


def _trunk(x, rel_bias, ffn1_norm, ffn1_w, mixer_norm, attn_w_qkv, attn_w_o, hy, ffn2_norm, ffn2_w,
           final_norm):
    b, s, d = x.shape
    depth = ffn1_norm.shape[0]
    flat = lambda a: a.reshape(b * s, d)
    for i in range(depth):
        x = _ffn(flat(x), ffn1_norm[i], *ffn1_w[i]).reshape(b, s, d)
        j = i // 2
        if i % 2 == 0:
            os_, stats, w_o = _dilated_attention_layer(x, mixer_norm[i], attn_w_qkv[j], attn_w_o[j],
                                                       rel_bias)
            pre = ("merge", os_, stats, s, w_o)
        else:
            pre = ("matmul", *_hyena_layer(x, mixer_norm[i], *[a[j] for a in hy]))
        fin = final_norm if i == depth - 1 else None
        x = _ffn(flat(x), ffn2_norm[i], *ffn2_w[i], pre=pre, final_g=fin).reshape(b, s, d)
    return x


def kernel(x_prompt, x_sample, rel_bias, ffn1_norm, ffn1_w_gate_up, ffn1_w_down, mixer_norm, attn_w_qkv, attn_w_o, hyena_w_in, hyena_conv_w, hyena_conv_b, hyena_filt_w1, hyena_filt_b1, hyena_filt_w2, hyena_filt_b2, hyena_filt_w3, hyena_filt_b3, hyena_filt_freq, hyena_filt_w4, hyena_skip, hyena_w_out, ffn2_norm, ffn2_w_gate_up, ffn2_w_down, final_norm):
    depth = ffn1_norm.shape[0]
    ffn1_w = [_ffn_weights(ffn1_w_gate_up[i], ffn1_w_down[i]) for i in range(depth)]
    ffn2_w = [_ffn_weights(ffn2_w_gate_up[i], ffn2_w_down[i]) for i in range(depth)]
    hy = (hyena_w_in, hyena_conv_w, hyena_conv_b, hyena_filt_w1, hyena_filt_b1, hyena_filt_w2,
          hyena_filt_b2, hyena_filt_w3, hyena_filt_b3, hyena_filt_freq, hyena_filt_w4, hyena_skip,
          hyena_w_out)
    args = (rel_bias, ffn1_norm, ffn1_w, mixer_norm, attn_w_qkv, attn_w_o, hy, ffn2_norm, ffn2_w,
            final_norm)
    return (_trunk(x_prompt, *args), _trunk(x_sample, *args))
```

```python
import functools
import math

import numpy as np
import jax
import jax.numpy as jnp
from jax import lax
from jax.experimental import pallas as pl
from jax.experimental.pallas import tpu as pltpu

D_MODEL = 1024
HEAD_DIM = 64
HEADS_PER_GROUP = D_MODEL // HEAD_DIM
DILATED_GROUPS = ((128, 1), (512, 4), (2048, 16))
N_GROUPS = len(DILATED_GROUPS)
NUM_BUCKETS = 32
MAX_DISTANCE = 1024
HYENA_EMB_DIM = 33
HYENA_FILTER_WIDTH = 64
FAST_DECAY_PCT = 0.3
SLOW_DECAY_PCT = 1.5
DECAY_TARGET = 1e-2
RMS_EPS = 1e-6

_F32 = jnp.float32
_BF16 = jnp.bfloat16
_V7X_VMEM_LIMIT_BYTES = 58 * 1024 * 1024
_LANES = 128
_NEG = -1e30
_LOG2E = math.log2(math.e)

_TQ = 128
_HALF_W = 64
_TK = _TQ + 2 * _HALF_W
_ATTN_MAX_SUBTILES = 8
_MXU_DIM = 256
_CONV_P = 512
_CONV_ROWS = 32
_SPECTRUM_DTYPE = jnp.bfloat16
_SPECTRUM_GROUP = 4
_CONV_STATIC_BLOCKS = 4


def _params(*sem):
    return pltpu.CompilerParams(dimension_semantics=sem, vmem_limit_bytes=_V7X_VMEM_LIMIT_BYTES)


def _const_spec(shape):
    nd = len(shape)
    return pl.BlockSpec(shape, lambda *_: (0,) * nd, pipeline_mode=pl.Buffered(1))


def _rms(x, g):
    y = x * lax.rsqrt(jnp.mean(x * x, axis=-1, keepdims=True) + RMS_EPS)
    return y * g


def _merge_rows(o_refs, s_refs, e_ref, of_ref, lf_ref, h, sub):
    nblk = D_MODEL // _LANES

    def natural(o_ref, s_ref, slot):
        d = o_ref.shape[0]
        n = sub // d
        cls = slice(h * n, (h + 1) * n)
        if d == 1:
            return o_ref[0, cls, :].astype(_F32), s_ref[0, cls, :]
        for r in range(d):
            o_r = o_ref[r, cls, :].astype(_F32)
            for c in range(nblk):
                of_ref[slot, c, pl.ds(r, n, stride=d), :] = o_r[:, c * _LANES:(c + 1) * _LANES]
            lf_ref[slot, pl.ds(r, n, stride=d), :] = s_ref[r, cls, :]
        return jnp.concatenate([of_ref[slot, c] for c in range(nblk)], axis=1), lf_ref[slot]

    outs, stats = [], []
    for g, (o_ref, s_ref) in enumerate(zip(o_refs, s_refs)):
        o, st = natural(o_ref, s_ref, 2 * h + (g % 2))
        outs.append(o)
        stats.append(st)
    head = lax.broadcasted_iota(jnp.int32, (1, _LANES), 1) < HEADS_PER_GROUP
    ms = [jnp.where(head, st, 0.0) for st in stats]
    ls = [jnp.where(head, pltpu.roll(st, _LANES - HEADS_PER_GROUP, 1), 1.0) for st in stats]
    m = functools.reduce(jnp.maximum, ms)
    ws = [jnp.exp2(mg - m) for mg in ms]
    inv = 1.0 / sum(w * l for w, l in zip(ws, ls))
    merged = None
    for w, o in zip(ws, outs):
        al = w * inv
        hi = al.astype(_BF16)
        lo = (al - hi.astype(_F32)).astype(_BF16)
        term = jnp.dot(jnp.concatenate([hi, lo], axis=1), e_ref[...], preferred_element_type=_F32) * o
        merged = term if merged is None else merged + term
    return merged


def _ffn_kernel(*refs, pre, final, sub):
    refs = list(refs)
    x_ref, g_ref, wgu_ref, wd_ref = refs[:4]
    extra = refs[4:]
    if pre == "matmul":
        a_ref, wpre_ref = extra[:2]
        extra = extra[2:]
    elif pre == "merge":
        o_refs, s_refs, (e_ref, wpre_ref) = extra[0:3], extra[3:6], extra[6:8]
        extra = extra[8:]
    gf_ref = extra.pop(0) if final else None
    o_ref = extra.pop(0)
    f = wd_ref.shape[0]
    n_tiles = f // _MXU_DIM
    bounds = [0, (n_tiles + 1) // 2 * _MXU_DIM, f]
    for h in range(x_ref.shape[0] // sub):
        rows = slice(h * sub, (h + 1) * sub)
        x = x_ref[rows, :]
        if pre == "matmul":
            x = x + jnp.dot(a_ref[rows, :], wpre_ref[...], preferred_element_type=_F32)
        elif pre == "merge":
            merged = _merge_rows(o_refs, s_refs, e_ref, extra[0], extra[1], h, sub)
            x = x + jnp.dot(merged.astype(_BF16), wpre_ref[...], preferred_element_type=_F32)
        xb = _rms(x, g_ref[...]).astype(_BF16)
        acc = jnp.zeros(x.shape, _F32)
        for lo, hi in zip(bounds[:-1], bounds[1:]):
            gate = jnp.dot(xb, wgu_ref[:, lo:hi], preferred_element_type=_F32)
            up = jnp.dot(xb, wgu_ref[:, f + lo:f + hi], preferred_element_type=_F32)
            act = (gate * jax.nn.sigmoid(gate)) * up
            acc = acc + jnp.dot(act.astype(_BF16), wd_ref[lo:hi, :], preferred_element_type=_F32)
        y = x + 0.5 * acc
        if final:
            y = _rms(y, gf_ref[...])
        o_ref[rows, :] = y


def _ffn(x2d, norm_g, wgu, wd, pre=None, final_g=None, tm=1024, sub=512):
    t, d = x2d.shape
    kind = None if pre is None else pre[0]
    if kind == "merge":
        tm = sub
    row = lambda w: pl.BlockSpec((tm, w), lambda i: (i, 0))
    in_specs = [row(d), _const_spec((1, d)), _const_spec(wgu.shape), _const_spec(wd.shape)]
    args = [x2d, norm_g.reshape(1, d), wgu, wd]
    scratch = []
    if kind == "matmul":
        _, a2d, w_pre = pre
        in_specs += [row(a2d.shape[1]), _const_spec(w_pre.shape)]
        args += [a2d, w_pre]
    elif kind == "merge":
        _, os_, stats, seq_len, w_pre = pre
        nt = seq_len // tm
        e = _head_expand_matrix()

        def cls(a):
            dil, w = a.shape[1], a.shape[3]
            return pl.BlockSpec((None, dil, tm // dil, w), lambda i: (i // nt, 0, i % nt, 0))

        in_specs += [cls(a) for a in os_] + [cls(a) for a in stats]
        in_specs += [_const_spec(e.shape), _const_spec(w_pre.shape)]
        args += [*os_, *stats, e, w_pre]
        n_slots = 2 * (tm // sub)
        scratch = [pltpu.VMEM((n_slots, d // _LANES, sub, _LANES), _F32),
                   pltpu.VMEM((n_slots, sub, _LANES), _F32)]
    if final_g is not None:
        in_specs.append(_const_spec((1, d)))
        args.append(final_g.reshape(1, d))
    return pl.pallas_call(
        functools.partial(_ffn_kernel, pre=kind, final=final_g is not None, sub=sub),
        grid=(t // tm,), in_specs=in_specs, out_specs=row(d),
        out_shape=jax.ShapeDtypeStruct((t, d), _F32), scratch_shapes=scratch,
        compiler_params=_params("parallel"), name="ffn")(*args)


def _ffn_weights(w_gate_up, w_down):
    return w_gate_up.astype(_BF16), w_down.astype(_BF16)


def _t5_bucket(rel):
    nb = NUM_BUCKETS // 2
    max_exact = nb // 2
    ret = (rel > 0).astype(np.int32) * nb
    n = np.abs(rel)
    large = max_exact + (np.log(np.maximum(n, 1) / max_exact)
                         / math.log(MAX_DISTANCE / max_exact) * (nb - max_exact)).astype(np.int32)
    large = np.minimum(large, nb - 1)
    return ret + np.where(n < max_exact, n, large)


def _attn_bucket_table(dilation):
    delta = np.arange(_TK)[None, :] - _HALF_W - np.arange(_TQ)[:, None]
    bucket = _t5_bucket(delta * dilation)
    return jnp.asarray(np.where(np.abs(delta) <= _HALF_W, bucket, -1), jnp.int32)


def _bias_tiles_kernel(rb_ref, bucket_ref, o_ref):
    hp = pl.program_id(0)
    bucket = bucket_ref[...]
    kk = lax.broadcasted_iota(jnp.int32, (1, _TK), 1)
    after_start = kk >= _HALF_W
    before_end = kk < _TQ + _HALF_W
    for hh in range(2):
        acc = jnp.full((_TQ, _TK), _NEG, _F32)
        for bkt in range(NUM_BUCKETS):
            acc = jnp.where(bucket == bkt, rb_ref[2 * hp + hh, bkt] * _LOG2E, acc)
        rows = slice(hh * _TQ, (hh + 1) * _TQ)
        o_ref[0, rows, :] = acc
        o_ref[1, rows, :] = jnp.where(after_start, acc, _NEG)
        o_ref[2, rows, :] = jnp.where(before_end, acc, _NEG)
        o_ref[3, rows, :] = jnp.where(after_start & before_end, acc, _NEG)


def _attn_bias_tiles(rel_bias_g, dilation):
    npair = HEADS_PER_GROUP // 2
    return pl.pallas_call(
        _bias_tiles_kernel, grid=(npair,),
        in_specs=[pl.BlockSpec(memory_space=pltpu.SMEM), _const_spec((_TQ, _TK))],
        out_specs=pl.BlockSpec((4, None, 2 * _TQ, _TK), lambda hp: (0, hp, 0, 0)),
        out_shape=jax.ShapeDtypeStruct((4, npair, 2 * _TQ, _TK), _F32),
        compiler_params=_params("parallel"), name="attn_bias_tiles")(
            rel_bias_g.astype(_F32).T, _attn_bucket_table(dilation))


def _attn_kernel(q_ref, kp_ref, kc_ref, kn_ref, vp_ref, vc_ref, vn_ref, bias_ref,
                 o_ref, stat_ref, *, nr, nsub):
    i = pl.program_id(2)
    is_first = jnp.where(i == 0, 1, 0)
    is_last = jnp.where(i == pl.num_programs(2) - 1, 2, 0)
    lane = lax.broadcasted_iota(jnp.int32, (1, _LANES), 1)
    low = lane < HEAD_DIM
    zero = jnp.zeros((), _BF16)
    cur_rows = nsub * _TQ

    def window(prev_ref, cur_ref, next_ref, ri, sub, sl):
        start, end = sub * _TQ - _HALF_W, (sub + 1) * _TQ + _HALF_W
        pieces = []
        if start < 0:
            pieces.append(prev_ref[ri, _TQ - _HALF_W:, sl])
        pieces.append(cur_ref[ri, max(start, 0):min(end, cur_rows), sl])
        if end > cur_rows:
            pieces.append(next_ref[ri, :_HALF_W, sl])
        return jnp.concatenate(pieces, axis=0)

    for ri in range(nr):
        for sub in range(nsub):
            rows = slice(sub * _TQ, (sub + 1) * _TQ)
            variant = (is_first if sub == 0 else 0) + (is_last if sub == nsub - 1 else 0)
            stat = jnp.zeros((_TQ, _LANES), _F32)
            for hp in range(HEADS_PER_GROUP // 2):
                sl = slice(hp * _LANES, (hp + 1) * _LANES)
                q2 = q_ref[ri, rows, sl]
                qq = jnp.concatenate([jnp.where(low, q2, zero), jnp.where(low, zero, q2)], axis=0)
                k2 = window(kp_ref, kc_ref, kn_ref, ri, sub, sl)
                v2 = window(vp_ref, vc_ref, vn_ref, ri, sub, sl)
                s = lax.dot_general(qq, k2, (((1,), (1,)), ((), ())),
                                    preferred_element_type=_F32)
                s = s + bias_ref[variant, hp]
                m = jnp.max(s, axis=-1, keepdims=True)
                p = jnp.exp2(s - m)
                l = jnp.sum(p, axis=-1, keepdims=True)
                pv = jnp.dot(p.astype(_BF16), v2, preferred_element_type=_F32)
                o_ref[ri, rows, sl] = jnp.where(low, pv[:_TQ], pv[_TQ:]).astype(o_ref.dtype)
                h0, h1 = 2 * hp, 2 * hp + 1
                stat = jnp.where(lane == h0, m[:_TQ], jnp.where(lane == h1, m[_TQ:], stat))
                stat = jnp.where(lane == HEADS_PER_GROUP + h0, l[:_TQ],
                                 jnp.where(lane == HEADS_PER_GROUP + h1, l[_TQ:], stat))
            stat_ref[ri, rows, :] = stat


def _qkv_kernel(x_ref, g_ref, w_ref, *rest, dilations, tm):
    o_refs = rest[:len(dilations)]
    xn_ref, xp_ref = rest[len(dilations):]
    nblk = D_MODEL // _LANES
    xn = _rms(x_ref[...], g_ref[...])
    for c in range(nblk):
        xn_ref[c] = xn[:, c * _LANES:(c + 1) * _LANES]
    for g, (d, o_ref) in enumerate(zip(dilations, o_refs)):
        rows = tm // d
        if d == 1:
            xp = xn.astype(_BF16)
        else:
            for r in range(d):
                for c in range(nblk):
                    xp_ref[g % 2, r * rows:(r + 1) * rows, c * _LANES:(c + 1) * _LANES] = (
                        xn_ref[c, pl.ds(r, rows, stride=d), :].astype(_BF16))
            xp = xp_ref[g % 2]
        for j in range(3):
            cs = slice(j * D_MODEL, (j + 1) * D_MODEL)
            y = jnp.dot(xp, w_ref[g, :, cs], preferred_element_type=_F32).astype(o_ref.dtype)
            for r in range(d):
                o_ref[r, :, cs] = y[r * rows:(r + 1) * rows]


def _qkv(x, norm_g, w_bf16, dilations, tm=512):
    b, s, dm = x.shape
    n3 = w_bf16.shape[2]
    return pl.pallas_call(
        functools.partial(_qkv_kernel, dilations=dilations, tm=tm), grid=(b, s // tm),
        in_specs=[pl.BlockSpec((None, tm, dm), lambda bi, i: (bi, i, 0)), _const_spec((1, dm)),
                  _const_spec(w_bf16.shape)],
        out_specs=[pl.BlockSpec((None, d, tm // d, n3), lambda bi, i: (bi, 0, i, 0))
                   for d in dilations],
        out_shape=[jax.ShapeDtypeStruct((b, d, s // d, n3), _BF16) for d in dilations],
        scratch_shapes=[pltpu.VMEM((dm // _LANES, tm, _LANES), _F32),
                        pltpu.VMEM((2, tm, dm), _BF16)],
        compiler_params=_params("parallel", "parallel"), name="qkv")(
            x, norm_g.reshape(1, dm), w_bf16)


def _group_attention(qkv, bias_tiles):
    b, d, lc, _ = qkv.shape
    n_tq = lc // _TQ
    nsub = min(_ATTN_MAX_SUBTILES, n_tq)
    nr = min(d, _ATTN_MAX_SUBTILES // nsub)
    rows = nsub * _TQ

    def cur(which):
        return pl.BlockSpec((None, nr, rows, D_MODEL), lambda bi, r, i: (bi, r, i, which))

    def prev(which):
        return pl.BlockSpec((None, nr, _TQ, D_MODEL),
                            lambda bi, r, i: (bi, r, jnp.maximum(i * nsub - 1, 0), which))

    def nxt(which):
        return pl.BlockSpec((None, nr, _TQ, D_MODEL),
                            lambda bi, r, i: (bi, r, jnp.minimum((i + 1) * nsub, n_tq - 1), which))

    return pl.pallas_call(
        functools.partial(_attn_kernel, nr=nr, nsub=nsub), grid=(b, d // nr, n_tq // nsub),
        in_specs=[cur(0), prev(1), cur(1), nxt(1), prev(2), cur(2), nxt(2),
                  _const_spec(bias_tiles.shape)],
        out_specs=[pl.BlockSpec((None, nr, rows, D_MODEL), lambda bi, r, i: (bi, r, i, 0)),
                   pl.BlockSpec((None, nr, rows, _LANES), lambda bi, r, i: (bi, r, i, 0))],
        out_shape=[jax.ShapeDtypeStruct((b, d, lc, D_MODEL), _BF16),
                   jax.ShapeDtypeStruct((b, d, lc, _LANES), _F32)],
        compiler_params=_params("parallel", "parallel", "arbitrary"), name=f"attn_d{d}")(
            qkv, qkv, qkv, qkv, qkv, qkv, qkv, bias_tiles)


def _head_expand_matrix():
    e = np.zeros((2 * _LANES, D_MODEL), np.float32)
    for h in range(HEADS_PER_GROUP):
        e[h, h * HEAD_DIM:(h + 1) * HEAD_DIM] = 1.0
        e[_LANES + h, h * HEAD_DIM:(h + 1) * HEAD_DIM] = 1.0
    return jnp.asarray(e, _BF16)


def _dilated_attention_layer(x, norm_g, w_qkv, w_o, rel_bias):
    n3 = 3 * D_MODEL
    dilations = tuple(dil for _, dil in DILATED_GROUPS)
    assert all((window // 2) // dil == _HALF_W for window, dil in DILATED_GROUPS)
    col_scale = jnp.where(jnp.arange(n3) < D_MODEL, _LOG2E * HEAD_DIM ** -0.5, 1.0).astype(_F32)
    w_groups = w_qkv.astype(_F32).reshape(D_MODEL, N_GROUPS, n3).transpose(1, 0, 2) * col_scale
    qkvs = _qkv(x, norm_g, w_groups.astype(_BF16), dilations)
    os_, stats = [], []
    for g, (dil, qkv) in enumerate(zip(dilations, qkvs)):
        bias = _attn_bias_tiles(rel_bias[:, g * HEADS_PER_GROUP:(g + 1) * HEADS_PER_GROUP], dil)
        o, stat = _group_attention(qkv, bias)
        os_.append(o)
        stats.append(stat)
    return os_, stats, w_o.astype(_BF16)


def _hyena_in_kernel(xp_ref, x_ref, xn_ref, g_ref, w_ref, cw_ref, cb_ref,
                     x1_ref, x2_ref, v_ref, *, tm, sub):
    i = pl.program_id(1)
    last = pl.num_programs(1) - 1
    keep_prev = jnp.where(i > 0, 1.0, 0.0)
    keep_next = jnp.where(i < last, 1.0, 0.0)
    xe = jnp.concatenate([xp_ref[...] * keep_prev, x_ref[...], xn_ref[...] * keep_next], axis=0)
    n = sub + 16
    for h in range(tm // sub):
        xb = _rms(xe[h * sub:h * sub + n], g_ref[...]).astype(_BF16)
        rows = slice(h * sub, (h + 1) * sub)
        for j, o_ref in enumerate((x1_ref, x2_ref, v_ref)):
            cs = slice(j * D_MODEL, (j + 1) * D_MODEL)
            u = jnp.dot(xb, w_ref[:, cs], preferred_element_type=_F32)
            u_prev = pltpu.roll(u, 1, 0)[8:sub + 8]
            u_next = pltpu.roll(u, n - 1, 0)[8:sub + 8]
            cw = cw_ref[:, cs]
            y = (u_prev * cw[0:1] + u[8:sub + 8] * cw[1:2] + u_next * cw[2:3]) + cb_ref[:, cs]
            o_ref[rows, :] = y.astype(o_ref.dtype)


def _hyena_in(x, norm_g, w_in_bf16, conv_w, conv_b, tm=1024, sub=512):
    b, s, d = x.shape
    n3 = w_in_bf16.shape[1]
    nblk8 = s // 8
    r8 = tm // 8
    main = pl.BlockSpec((None, tm, d), lambda bi, i: (bi, i, 0))
    prev = pl.BlockSpec((None, 8, d), lambda bi, i: (bi, jnp.maximum(i * r8 - 1, 0), 0))
    nxt = pl.BlockSpec((None, 8, d), lambda bi, i: (bi, jnp.minimum((i + 1) * r8, nblk8 - 1), 0))
    out = jax.ShapeDtypeStruct((b, s, d), _BF16)
    return pl.pallas_call(
        functools.partial(_hyena_in_kernel, tm=tm, sub=sub), grid=(b, s // tm),
        in_specs=[prev, main, nxt, _const_spec((1, d)), _const_spec(w_in_bf16.shape),
                  _const_spec((3, n3)), _const_spec((1, n3))],
        out_specs=[main, main, main], out_shape=[out, out, out],
        compiler_params=_params("parallel", "arbitrary"), name="hyena_in")(
            x, x, x, norm_g.reshape(1, d), w_in_bf16, conv_w.astype(_F32),
            conv_b.astype(_F32).reshape(1, n3))


def _filter_kernel(fr_ref, ph_ref, w1a_ref, w1b_ref, b1_ref, w2_ref, b2_ref, w3_ref, b3_ref, fq_ref,
                   w4h_ref, w4l_ref, dl_ref, k0_ref, k1_ref, *, seq_len, tr):
    hi = lax.Precision.HIGHEST
    half = tr // 2
    r = pl.program_id(0) * tr + lax.broadcasted_iota(jnp.int32, (tr, 1), 0)
    pos = jnp.abs(r - seq_len).astype(_F32)
    t = pos / (seq_len - 1.0)
    wpos = (2.0 * math.pi) * pos / seq_len
    lane = lax.broadcasted_iota(jnp.int32, (1, _LANES), 1)
    z = jnp.where(lane == 0, t,
                  jnp.where(lane < HYENA_EMB_DIM, jnp.cos(fr_ref[...] * wpos + ph_ref[...]), 0.0))
    dot = functools.partial(jnp.dot, precision=hi, preferred_element_type=_F32)
    fq = fq_ref[...]
    h = jnp.sin(fq * (dot(z[:half], w1a_ref[...]) + dot(z[half:], w1b_ref[...]) + b1_ref[...]))
    h = jnp.sin(fq * (dot(h, w2_ref[...]) + b2_ref[...]))
    h = jnp.sin(fq * (dot(h, w3_ref[...]) + b3_ref[...]))
    decay = jnp.exp(-t * jnp.abs(dl_ref[...]))
    live = r > 0
    h_hi = h.astype(_BF16)
    h_lo = (h - h_hi.astype(_F32)).astype(_BF16)
    for o, k_ref in enumerate((k0_ref, k1_ref)):
        cs = slice(o * D_MODEL, (o + 1) * D_MODEL)
        for part in range(2):
            w_hi = w4h_ref[part, :, cs]
            y = jnp.dot(h_hi, w_hi, preferred_element_type=_F32)
            y = y + jnp.dot(h_hi, w4l_ref[part, :, cs], preferred_element_type=_F32)
            y = y + jnp.dot(h_lo, w_hi, preferred_element_type=_F32)
            rows = slice(part * half, (part + 1) * half)
            k_ref[rows, :] = jnp.where(live[rows], y * decay[rows], 0.0)


def _hyena_two_sided_filters(seq_len, w1, b1, w2, b2, w3, b3, freq, w4, tr=512):
    fw = HYENA_FILTER_WIDTH
    assert 2 * fw == _LANES
    n_tiles = 2 * seq_len // tr
    f32 = lambda a: a.astype(_F32)
    zeros = jnp.zeros((fw, 2 * D_MODEL), _F32)
    w4d = f32(w4).reshape(fw, 2, 2, D_MODEL).transpose(2, 0, 1, 3).reshape(2, fw, 2 * D_MODEL)
    w4p = jnp.stack([jnp.stack([jnp.concatenate([w4d[dr], zeros]), jnp.concatenate([zeros, w4d[dr]])])
                     for dr in range(2)])
    w4_hi = w4p.astype(_BF16)
    w4_lo = (w4p - w4_hi.astype(_F32)).astype(_BF16)
    w4_spec = pl.BlockSpec((None, 2, _LANES, 2 * D_MODEL),
                           lambda i: (jnp.where(i >= n_tiles // 2, 0, 1), 0, 0, 0))
    bands = (HYENA_EMB_DIM - 1) // 2
    fr = jnp.linspace(1e-4, bands - 1, bands, dtype=_F32)
    fr_lanes = jnp.zeros((1, _LANES), _F32).at[0, 1:1 + bands].set(fr).at[0, 1 + bands:1 + 2 * bands].set(fr)
    phase = jnp.zeros((1, _LANES), _F32).at[0, 1 + bands:1 + 2 * bands].set(0.5 * math.pi)
    w1p = jnp.zeros((_LANES, fw), _F32).at[:HYENA_EMB_DIM].set(f32(w1))
    z64 = jnp.zeros((_LANES, fw), _F32)
    w1a = jnp.concatenate([w1p, z64], axis=1)
    w1b = jnp.concatenate([z64, w1p], axis=1)
    zz = jnp.zeros((fw, fw), _F32)
    bdiag = lambda w: jnp.block([[f32(w), zz], [zz, f32(w)]])
    twice = lambda a: jnp.tile(f32(a).reshape(1, fw), (1, 2))
    max_decay = math.log(DECAY_TARGET) / FAST_DECAY_PCT
    min_decay = math.log(DECAY_TARGET) / SLOW_DECAY_PCT
    deltas = jnp.linspace(min_decay, max_decay, D_MODEL, dtype=_F32).reshape(1, D_MODEL)
    out = jax.ShapeDtypeStruct((2 * seq_len, D_MODEL), _F32)
    cs = _const_spec
    sq, vec = cs((_LANES, _LANES)), cs((1, _LANES))
    return pl.pallas_call(
        functools.partial(_filter_kernel, seq_len=seq_len, tr=tr), grid=(n_tiles,),
        in_specs=[vec, vec, sq, sq, vec, sq, vec, sq, vec, vec, w4_spec, w4_spec, cs((1, D_MODEL))],
        out_specs=[pl.BlockSpec((tr, D_MODEL), lambda i: (i, 0))] * 2, out_shape=[out, out],
        compiler_params=_params("parallel"), name="hyena_filter")(
            fr_lanes, phase, w1a, w1b, twice(b1), bdiag(w2), twice(b2), bdiag(w3), twice(b3),
            twice(freq), w4_hi, w4_lo, deltas)


def _dft_matrices(p):
    n = 2 * p
    k = np.arange(p)[:, None].astype(np.float64)
    s = np.arange(p)[None, :].astype(np.float64)
    ang = np.pi * k * s / p
    fre = np.cos(ang)
    fim = -np.sin(ang)
    fim[0, :] = np.cos(np.pi * s[0])
    fwd_half = np.concatenate([fre, fim], axis=0)
    tt = (p + np.arange(p))[:, None].astype(np.float64)
    kk = np.arange(p)[None, :].astype(np.float64)
    ang2 = np.pi * kk * tt / p
    are = 2.0 * np.cos(ang2) / n
    are[:, 0] = 1.0 / n
    aim = -2.0 * np.sin(ang2) / n
    aim[:, 0] = np.cos(np.pi * tt[:, 0]) / n
    inv = np.concatenate([are, aim], axis=1)
    return jnp.asarray(fwd_half, _BF16), jnp.asarray(inv, _BF16)


def _filter_spectrum_kernel(g0_ref, ga_ref, gb_ref, f_ref, o_ref, prev_ref):
    spectrum = lambda g_ref: jnp.dot(f_ref[...], g_ref[...].astype(_BF16),
                                     preferred_element_type=_F32)

    @pl.when(pl.program_id(1) == 0)
    def _():
        prev_ref[...] = spectrum(g0_ref)

    aa, ab = spectrum(ga_ref), spectrum(gb_ref)
    odd = (lax.broadcasted_iota(jnp.int32, (aa.shape[0], 1), 0) & 1) == 1
    o_ref[0] = (prev_ref[...] + jnp.where(odd, -aa, aa)).astype(o_ref.dtype)
    o_ref[1] = (aa + jnp.where(odd, -ab, ab)).astype(o_ref.dtype)
    prev_ref[...] = ab


def _filter_spectrum(kt, fwd_half, p, tc=256):
    rows, d = kt.shape
    nblk = rows // p
    blk = lambda imap: pl.BlockSpec((p, tc), imap)
    return pl.pallas_call(
        _filter_spectrum_kernel, grid=(d // tc, nblk // 2),
        in_specs=[blk(lambda c, s: (0, c)), blk(lambda c, s: (2 * s + 1, c)),
                  blk(lambda c, s: (jnp.minimum(2 * s + 2, nblk - 1), c)),
                  _const_spec(fwd_half.shape)],
        out_specs=pl.BlockSpec((2, 2 * p, tc), lambda c, s: (s, 0, c)),
        out_shape=jax.ShapeDtypeStruct((nblk, 2 * p, d), _SPECTRUM_DTYPE),
        scratch_shapes=[pltpu.VMEM((2 * p, tc), _F32)],
        compiler_params=_params("parallel", "arbitrary"), name="hyena_filter_spectrum")(
            kt, kt, kt, fwd_half)


def _long_conv_kernel(gate_ref, z_ref, kf_ref, skip_ref, ff_ref, ai_ref, o_ref, zf_ref, yfa_ref,
                      yfb_ref, *, p, nb, tc, ne):
    ch = _CONV_ROWS
    slots = zf_ref.shape[0]
    skip = skip_ref[...]
    first_row = lax.broadcasted_iota(jnp.int32, (ch, 1), 0) == 0

    def forward(e, i):
        zb = z_ref[e, pl.ds(pl.multiple_of(i * p, p), p), :]
        zf_ref[e % slots, i] = jnp.dot(ff_ref[...], zb,
                                       preferred_element_type=_F32).astype(zf_ref.dtype)

    def spectrum(e, j, yf_ref):
        for rc in range(p // ch):
            re = slice(rc * ch, (rc + 1) * ch)
            im = slice(p + rc * ch, p + (rc + 1) * ch)
            a = bm = cc = None
            for i0 in range(0, nb, _SPECTRUM_GROUP):
                ga = jnp.zeros((ch, tc), zf_ref.dtype)
                gb = jnp.zeros((ch, tc), zf_ref.dtype)
                gc = jnp.zeros((ch, tc), zf_ref.dtype)
                for i in range(i0, min(i0 + _SPECTRUM_GROUP, nb)):
                    q = j - i + (nb - 1)
                    zre, zim = zf_ref[e % slots, i, re, :], zf_ref[e % slots, i, im, :]
                    kre, kim = kf_ref[q, re, :], kf_ref[q, im, :]
                    ga = ga + zre * kre
                    gb = gb + zim * kim
                    gc = gc + (zre * kim + zim * kre)
                if nb > _SPECTRUM_GROUP:
                    ga, gb, gc = ga.astype(_F32), gb.astype(_F32), gc.astype(_F32)
                a = ga if a is None else a + ga
                bm = gb if bm is None else bm + gb
                cc = gc if cc is None else cc + gc
            if rc == 0:
                yf_ref[re, :] = jnp.where(first_row, a, a - bm).astype(_BF16)
                yf_ref[im, :] = jnp.where(first_row, bm, cc).astype(_BF16)
            else:
                yf_ref[re, :] = (a - bm).astype(_BF16)
                yf_ref[im, :] = cc.astype(_BF16)

    def finish(e, j, yf_ref):
        y = jnp.dot(ai_ref[...], yf_ref[...], preferred_element_type=_F32)
        rows = pl.ds(pl.multiple_of(j * p, p), p)
        zt = z_ref[e, rows, :].astype(_F32)
        gt = gate_ref[e, rows, :].astype(_F32)
        o_ref[e, rows, :] = (gt * (y + zt * skip)).astype(o_ref.dtype)

    yf = (yfa_ref, yfb_ref)
    if nb <= _CONV_STATIC_BLOCKS:
        for i in range(nb):
            forward(0, i)
        pending = None
        for e in range(ne):
            for j in range(nb):
                if pending is not None:
                    finish(*pending)
                if e + 1 < ne:
                    forward(e + 1, j)
                spectrum(e, j, yf[j % 2])
                pending = (e, j, yf[j % 2])
        finish(*pending)
    else:
        assert ne == 1 and nb % 2 == 0
        for i in range(nb):
            forward(0, i)
        spectrum(0, 0, yfa_ref)

        def pair_body(k, carry):
            finish(0, 2 * k, yfa_ref)
            spectrum(0, 2 * k + 1, yfb_ref)
            finish(0, 2 * k + 1, yfb_ref)
            spectrum(0, 2 * k + 2, yfa_ref)
            return carry

        lax.fori_loop(0, nb // 2 - 1, pair_body, 0)
        finish(0, nb - 2, yfa_ref)
        spectrum(0, nb - 1, yfb_ref)
        finish(0, nb - 1, yfb_ref)


def _long_conv(gate, z, kf, skip_o, fwd_half, inv, p):
    b, l, d = z.shape
    nb = l // p
    ne = 2 if (nb <= _CONV_STATIC_BLOCKS and b % 2 == 0) else 1
    slots = min(ne, 2)
    spectra = (2 * nb - 1 + slots * nb) * 2 * p * jnp.dtype(_SPECTRUM_DTYPE).itemsize
    seq_bytes = ne * l * 2
    dft_bytes = 2 * (2 * p * p * 2)
    budget = _V7X_VMEM_LIMIT_BYTES * 9 // 10
    tc, io_buffers = 128, 2
    for cand_tc, cand_buffers in ((256, 2), (256, 1)):
        if cand_tc * (spectra + (2 + 2 * cand_buffers) * seq_bytes) + dft_bytes <= budget:
            tc, io_buffers = cand_tc, cand_buffers
            break
    io_mode = {} if io_buffers == 2 else {"pipeline_mode": pl.Buffered(1)}
    seq = pl.BlockSpec((ne, l, tc), lambda c, bi: (bi, 0, c))
    seq_io = pl.BlockSpec((ne, l, tc), lambda c, bi: (bi, 0, c), **io_mode)
    kf_spec = pl.BlockSpec((2 * nb - 1, 2 * p, tc), lambda c, bi: (0, 0, c),
                           pipeline_mode=pl.Buffered(1))
    return pl.pallas_call(
        functools.partial(_long_conv_kernel, p=p, nb=nb, tc=tc, ne=ne), grid=(d // tc, b // ne),
        in_specs=[seq_io, seq, kf_spec, pl.BlockSpec((1, tc), lambda c, bi: (0, c)),
                  _const_spec(fwd_half.shape), _const_spec(inv.shape)],
        out_specs=seq_io, out_shape=jax.ShapeDtypeStruct((b, l, d), _BF16),
        scratch_shapes=[pltpu.VMEM((slots, nb, 2 * p, tc), _SPECTRUM_DTYPE),
                        pltpu.VMEM((2 * p, tc), _BF16),
                        pltpu.VMEM((2 * p, tc), _BF16)],
        compiler_params=_params("parallel", "arbitrary"), name="hyena_long_conv")(
            gate, z, kf, skip_o.astype(_F32).reshape(1, d), fwd_half, inv)


def _hyena_layer(x, norm_g, w_in, conv_w, conv_b, w1, b1, w2, b2, w3, b3, freq, w4, skip, w_out):
    b, l, d = x.shape
    p = min(l, max(_CONV_P, l // 8))
    x1, x2, v = _hyena_in(x, norm_g, w_in.astype(_BF16), conv_w, conv_b)
    k0, k1 = _hyena_two_sided_filters(l, w1, b1, w2, b2, w3, b3, freq, w4)
    fwd_half, inv = _dft_matrices(p)
    z = v
    for o, (gate, kt) in enumerate(((x1, k0), (x2, k1))):
        kf = _filter_spectrum(kt, fwd_half, p)
        z = _long_conv(gate, z, kf, skip[o], fwd_half, inv, p)
    return z.reshape(b * l, d), w_out.astype(_BF16)


def _trunk(x, rel_bias, ffn1_norm, ffn1_w, mixer_norm, attn_w_qkv, attn_w_o, hy, ffn2_norm, ffn2_w,
           final_norm):
    b, s, d = x.shape
    depth = ffn1_norm.shape[0]
    flat = lambda a: a.reshape(b * s, d)
    for i in range(depth):
        x = _ffn(flat(x), ffn1_norm[i], *ffn1_w[i]).reshape(b, s, d)
        j = i // 2
        if i % 2 == 0:
            os_, stats, w_o = _dilated_attention_layer(x, mixer_norm[i], attn_w_qkv[j], attn_w_o[j],
                                                       rel_bias)
            pre = ("merge", os_, stats, s, w_o)
        else:
            pre = ("matmul", *_hyena_layer(x, mixer_norm[i], *[a[j] for a in hy]))
        fin = final_norm if i == depth - 1 else None
        x = _ffn(flat(x), ffn2_norm[i], *ffn2_w[i], pre=pre, final_g=fin).reshape(b, s, d)
    return x


def kernel(x_prompt, x_sample, rel_bias, ffn1_norm, ffn1_w_gate_up, ffn1_w_down, mixer_norm, attn_w_qkv, attn_w_o, hyena_w_in, hyena_conv_w, hyena_conv_b, hyena_filt_w1, hyena_filt_b1, hyena_filt_w2, hyena_filt_b2, hyena_filt_w3, hyena_filt_b3, hyena_filt_freq, hyena_filt_w4, hyena_skip, hyena_w_out, ffn2_norm, ffn2_w_gate_up, ffn2_w_down, final_norm):
    depth = ffn1_norm.shape[0]
    ffn1_w = [_ffn_weights(ffn1_w_gate_up[i], ffn1_w_down[i]) for i in range(depth)]
    ffn2_w = [_ffn_weights(ffn2_w_gate_up[i], ffn2_w_down[i]) for i in range(depth)]
    hy = (hyena_w_in, hyena_conv_w, hyena_conv_b, hyena_filt_w1, hyena_filt_b1, hyena_filt_w2,
          hyena_filt_b2, hyena_filt_w3, hyena_filt_b3, hyena_filt_freq, hyena_filt_w4, hyena_skip,
          hyena_w_out)
    args = (rel_bias, ffn1_norm, ffn1_w, mixer_norm, attn_w_qkv, attn_w_o, hy, ffn2_norm, ffn2_w,
            final_norm)
    return (_trunk(x_prompt, *args), _trunk(x_sample, *args))
```

```python
import functools
import math

import numpy as np
import jax
import jax.numpy as jnp
from jax import lax
from jax.experimental import pallas as pl
from jax.experimental.pallas import tpu as pltpu

D_MODEL = 1024
HEAD_DIM = 64
HEADS_PER_GROUP = D_MODEL // HEAD_DIM
DILATED_GROUPS = ((128, 1), (512, 4), (2048, 16))
N_GROUPS = len(DILATED_GROUPS)
NUM_BUCKETS = 32
MAX_DISTANCE = 1024
HYENA_EMB_DIM = 33
HYENA_FILTER_WIDTH = 64
FAST_DECAY_PCT = 0.3
SLOW_DECAY_PCT = 1.5
DECAY_TARGET = 1e-2
RMS_EPS = 1e-6

_F32 = jnp.float32
_BF16 = jnp.bfloat16
_V7X_VMEM_LIMIT_BYTES = 58 * 1024 * 1024
_LANES = 128
_NEG = -1e30
_LOG2E = math.log2(math.e)

_TQ = 128
_HALF_W = 64
_TK = _TQ + 2 * _HALF_W
_ATTN_MAX_SUBTILES = 8
_MXU_DIM = 256
_CONV_P = 512
_CONV_ROWS = 32
_SPECTRUM_DTYPE = jnp.bfloat16
_SPECTRUM_GROUP = 4
_CONV_STATIC_BLOCKS = 4


def _params(*sem):
    return pltpu.CompilerParams(dimension_semantics=sem, vmem_limit_bytes=_V7X_VMEM_LIMIT_BYTES)


def _const_spec(shape):
    nd = len(shape)
    return pl.BlockSpec(shape, lambda *_: (0,) * nd, pipeline_mode=pl.Buffered(1))


def _layer_spec(stacked, layer):
    nd = stacked.ndim - 1
    return pl.BlockSpec((None,) + stacked.shape[1:], lambda *_: (layer,) + (0,) * nd,
                        pipeline_mode=pl.Buffered(1))


def _rms(x, g):
    y = x * lax.rsqrt(jnp.mean(x * x, axis=-1, keepdims=True) + RMS_EPS)
    return y * g


def _merge_rows(o_refs, s_refs, e_ref, of_ref, lf_ref, h, sub):
    nblk = D_MODEL // _LANES

    def natural(o_ref, s_ref, slot):
        d = o_ref.shape[0]
        n = sub // d
        cls = slice(h * n, (h + 1) * n)
        if d == 1:
            return o_ref[0, cls, :].astype(_F32), s_ref[0, cls, :]
        for r in range(d):
            o_r = o_ref[r, cls, :].astype(_F32)
            for c in range(nblk):
                of_ref[slot, c, pl.ds(r, n, stride=d), :] = o_r[:, c * _LANES:(c + 1) * _LANES]
            lf_ref[slot, pl.ds(r, n, stride=d), :] = s_ref[r, cls, :]
        return jnp.concatenate([of_ref[slot, c] for c in range(nblk)], axis=1), lf_ref[slot]

    outs, stats = [], []
    for g, (o_ref, s_ref) in enumerate(zip(o_refs, s_refs)):
        o, st = natural(o_ref, s_ref, 2 * h + (g % 2))
        outs.append(o)
        stats.append(st)
    head = lax.broadcasted_iota(jnp.int32, (1, _LANES), 1) < HEADS_PER_GROUP
    ms = [jnp.where(head, st, 0.0) for st in stats]
    ls = [jnp.where(head, pltpu.roll(st, _LANES - HEADS_PER_GROUP, 1), 1.0) for st in stats]
    m = functools.reduce(jnp.maximum, ms)
    ws = [jnp.exp2(mg - m) for mg in ms]
    inv = 1.0 / sum(w * l for w, l in zip(ws, ls))
    merged = None
    for w, o in zip(ws, outs):
        al = w * inv
        hi = al.astype(_BF16)
        lo = (al - hi.astype(_F32)).astype(_BF16)
        term = jnp.dot(jnp.concatenate([hi, lo], axis=1), e_ref[...], preferred_element_type=_F32) * o
        merged = term if merged is None else merged + term
    return merged


def _ffn_kernel(*refs, pre, final, sub):
    refs = list(refs)
    x_ref, g_ref, wgu_ref, wd_ref = refs[:4]
    extra = refs[4:]
    if pre == "matmul":
        a_ref, wpre_ref = extra[:2]
        extra = extra[2:]
    elif pre == "merge":
        o_refs, s_refs, (e_ref, wpre_ref) = extra[0:3], extra[3:6], extra[6:8]
        extra = extra[8:]
    gf_ref = extra.pop(0) if final else None
    o_ref = extra.pop(0)
    f = wd_ref.shape[0]
    n_tiles = f // _MXU_DIM
    bounds = [0, (n_tiles + 1) // 2 * _MXU_DIM, f]
    for h in range(x_ref.shape[0] // sub):
        rows = slice(h * sub, (h + 1) * sub)
        x = x_ref[rows, :]
        if pre == "matmul":
            x = x + jnp.dot(a_ref[rows, :], wpre_ref[...], preferred_element_type=_F32)
        elif pre == "merge":
            merged = _merge_rows(o_refs, s_refs, e_ref, extra[0], extra[1], h, sub)
            x = x + jnp.dot(merged.astype(_BF16), wpre_ref[...], preferred_element_type=_F32)
        xb = _rms(x, g_ref[...]).astype(_BF16)
        acc = jnp.zeros(x.shape, _F32)
        for lo, hi in zip(bounds[:-1], bounds[1:]):
            gate = jnp.dot(xb, wgu_ref[:, lo:hi], preferred_element_type=_F32)
            up = jnp.dot(xb, wgu_ref[:, f + lo:f + hi], preferred_element_type=_F32)
            act = (gate * jax.nn.sigmoid(gate)) * up
            acc = acc + jnp.dot(act.astype(_BF16), wd_ref[lo:hi, :], preferred_element_type=_F32)
        y = x + 0.5 * acc
        if final:
            y = _rms(y, gf_ref[...])
        o_ref[rows, :] = y


def _ffn(x2d, norm_g, wgu, wd, layer, pre=None, final_g=None, tm=1024, sub=512):
    t, d = x2d.shape
    kind = None if pre is None else pre[0]
    if kind == "merge":
        tm = sub
    row = lambda w: pl.BlockSpec((tm, w), lambda i: (i, 0))
    in_specs = [row(d), _const_spec((1, d)), _layer_spec(wgu, layer), _layer_spec(wd, layer)]
    args = [x2d, norm_g.reshape(1, d), wgu, wd]
    scratch = []
    if kind == "matmul":
        _, a2d, w_pre = pre
        in_specs += [row(a2d.shape[1]), _const_spec(w_pre.shape)]
        args += [a2d, w_pre]
    elif kind == "merge":
        _, os_, stats, seq_len, w_pre = pre
        nt = seq_len // tm
        e = _head_expand_matrix()

        def cls(a):
            dil, w = a.shape[1], a.shape[3]
            return pl.BlockSpec((None, dil, tm // dil, w), lambda i: (i // nt, 0, i % nt, 0))

        in_specs += [cls(a) for a in os_] + [cls(a) for a in stats]
        in_specs += [_const_spec(e.shape), _const_spec(w_pre.shape)]
        args += [*os_, *stats, e, w_pre]
        n_slots = 2 * (tm // sub)
        scratch = [pltpu.VMEM((n_slots, d // _LANES, sub, _LANES), _F32),
                   pltpu.VMEM((n_slots, sub, _LANES), _F32)]
    if final_g is not None:
        in_specs.append(_const_spec((1, d)))
        args.append(final_g.reshape(1, d))
    return pl.pallas_call(
        functools.partial(_ffn_kernel, pre=kind, final=final_g is not None, sub=sub),
        grid=(t // tm,), in_specs=in_specs, out_specs=row(d),
        out_shape=jax.ShapeDtypeStruct((t, d), _F32), scratch_shapes=scratch,
        compiler_params=_params("parallel"), name="ffn")(*args)


def _t5_bucket(rel):
    nb = NUM_BUCKETS // 2
    max_exact = nb // 2
    ret = (rel > 0).astype(np.int32) * nb
    n = np.abs(rel)
    large = max_exact + (np.log(np.maximum(n, 1) / max_exact)
                         / math.log(MAX_DISTANCE / max_exact) * (nb - max_exact)).astype(np.int32)
    large = np.minimum(large, nb - 1)
    return ret + np.where(n < max_exact, n, large)


def _attn_bucket_table(dilation):
    delta = np.arange(_TK)[None, :] - _HALF_W - np.arange(_TQ)[:, None]
    bucket = _t5_bucket(delta * dilation)
    return jnp.asarray(np.where(np.abs(delta) <= _HALF_W, bucket, -1), jnp.int32)


def _bias_tiles_kernel(rb_ref, bucket_ref, o_ref):
    hp = pl.program_id(0)
    bucket = bucket_ref[...]
    kk = lax.broadcasted_iota(jnp.int32, (1, _TK), 1)
    after_start = kk >= _HALF_W
    before_end = kk < _TQ + _HALF_W
    for hh in range(2):
        acc = jnp.full((_TQ, _TK), _NEG, _F32)
        for bkt in range(NUM_BUCKETS):
            acc = jnp.where(bucket == bkt, rb_ref[2 * hp + hh, bkt] * _LOG2E, acc)
        rows = slice(hh * _TQ, (hh + 1) * _TQ)
        o_ref[0, rows, :] = acc
        o_ref[1, rows, :] = jnp.where(after_start, acc, _NEG)
        o_ref[2, rows, :] = jnp.where(before_end, acc, _NEG)
        o_ref[3, rows, :] = jnp.where(after_start & before_end, acc, _NEG)


def _attn_bias_tiles(rel_bias_g, dilation):
    npair = HEADS_PER_GROUP // 2
    return pl.pallas_call(
        _bias_tiles_kernel, grid=(npair,),
        in_specs=[pl.BlockSpec(memory_space=pltpu.SMEM), _const_spec((_TQ, _TK))],
        out_specs=pl.BlockSpec((4, None, 2 * _TQ, _TK), lambda hp: (0, hp, 0, 0)),
        out_shape=jax.ShapeDtypeStruct((4, npair, 2 * _TQ, _TK), _F32),
        compiler_params=_params("parallel"), name="attn_bias_tiles")(
            rel_bias_g.astype(_F32).T, _attn_bucket_table(dilation))


def _attn_kernel(q_ref, kp_ref, kc_ref, kn_ref, vp_ref, vc_ref, vn_ref, bias_ref,
                 o_ref, stat_ref, *, nr, nsub):
    i = pl.program_id(2)
    is_first = jnp.where(i == 0, 1, 0)
    is_last = jnp.where(i == pl.num_programs(2) - 1, 2, 0)
    lane = lax.broadcasted_iota(jnp.int32, (1, _LANES), 1)
    low = lane < HEAD_DIM
    zero = jnp.zeros((), _BF16)
    cur_rows = nsub * _TQ

    def window(prev_ref, cur_ref, next_ref, ri, sub, sl):
        start, end = sub * _TQ - _HALF_W, (sub + 1) * _TQ + _HALF_W
        pieces = []
        if start < 0:
            pieces.append(prev_ref[ri, _TQ - _HALF_W:, sl])
        pieces.append(cur_ref[ri, max(start, 0):min(end, cur_rows), sl])
        if end > cur_rows:
            pieces.append(next_ref[ri, :_HALF_W, sl])
        return jnp.concatenate(pieces, axis=0)

    for ri in range(nr):
        for sub in range(nsub):
            rows = slice(sub * _TQ, (sub + 1) * _TQ)
            variant = (is_first if sub == 0 else 0) + (is_last if sub == nsub - 1 else 0)
            stat = jnp.zeros((_TQ, _LANES), _F32)
            for hp in range(HEADS_PER_GROUP // 2):
                sl = slice(hp * _LANES, (hp + 1) * _LANES)
                q2 = q_ref[ri, rows, sl]
                qq = jnp.concatenate([jnp.where(low, q2, zero), jnp.where(low, zero, q2)], axis=0)
                k2 = window(kp_ref, kc_ref, kn_ref, ri, sub, sl)
                v2 = window(vp_ref, vc_ref, vn_ref, ri, sub, sl)
                s = lax.dot_general(qq, k2, (((1,), (1,)), ((), ())),
                                    preferred_element_type=_F32)
                s = s + bias_ref[variant, hp]
                m = jnp.max(s, axis=-1, keepdims=True)
                p = jnp.exp2(s - m)
                l = jnp.sum(p, axis=-1, keepdims=True)
                pv = jnp.dot(p.astype(_BF16), v2, preferred_element_type=_F32)
                o_ref[ri, rows, sl] = jnp.where(low, pv[:_TQ], pv[_TQ:]).astype(o_ref.dtype)
                h0, h1 = 2 * hp, 2 * hp + 1
                stat = jnp.where(lane == h0, m[:_TQ], jnp.where(lane == h1, m[_TQ:], stat))
                stat = jnp.where(lane == HEADS_PER_GROUP + h0, l[:_TQ],
                                 jnp.where(lane == HEADS_PER_GROUP + h1, l[_TQ:], stat))
            stat_ref[ri, rows, :] = stat


def _qkv_kernel(x_ref, g_ref, w_ref, *rest, dilations, tm):
    o_refs = rest[:len(dilations)]
    xn_ref, xp_ref = rest[len(dilations):]
    nblk = D_MODEL // _LANES
    xn = _rms(x_ref[...], g_ref[...])
    for c in range(nblk):
        xn_ref[c] = xn[:, c * _LANES:(c + 1) * _LANES]
    for g, (d, o_ref) in enumerate(zip(dilations, o_refs)):
        rows = tm // d
        if d == 1:
            xp = xn.astype(_BF16)
        else:
            for r in range(d):
                for c in range(nblk):
                    xp_ref[g % 2, r * rows:(r + 1) * rows, c * _LANES:(c + 1) * _LANES] = (
                        xn_ref[c, pl.ds(r, rows, stride=d), :].astype(_BF16))
            xp = xp_ref[g % 2]
        for j in range(3):
            cs = slice(j * D_MODEL, (j + 1) * D_MODEL)
            wcols = slice(g * 3 * D_MODEL + j * D_MODEL, g * 3 * D_MODEL + (j + 1) * D_MODEL)
            y = jnp.dot(xp, w_ref[:, wcols], preferred_element_type=_F32).astype(o_ref.dtype)
            for r in range(d):
                o_ref[r, :, cs] = y[r * rows:(r + 1) * rows]


def _qkv(x, norm_g, w_bf16, dilations, tm=512):
    b, s, dm = x.shape
    n3 = w_bf16.shape[1] // len(dilations)
    return pl.pallas_call(
        functools.partial(_qkv_kernel, dilations=dilations, tm=tm), grid=(b, s // tm),
        in_specs=[pl.BlockSpec((None, tm, dm), lambda bi, i: (bi, i, 0)), _const_spec((1, dm)),
                  _const_spec(w_bf16.shape)],
        out_specs=[pl.BlockSpec((None, d, tm // d, n3), lambda bi, i: (bi, 0, i, 0))
                   for d in dilations],
        out_shape=[jax.ShapeDtypeStruct((b, d, s // d, n3), _BF16) for d in dilations],
        scratch_shapes=[pltpu.VMEM((dm // _LANES, tm, _LANES), _F32),
                        pltpu.VMEM((2, tm, dm), _BF16)],
        compiler_params=_params("parallel", "parallel"), name="qkv")(
            x, norm_g.reshape(1, dm), w_bf16)


def _group_attention(qkv, bias_tiles):
    b, d, lc, _ = qkv.shape
    n_tq = lc // _TQ
    nsub = min(_ATTN_MAX_SUBTILES, n_tq)
    nr = min(d, _ATTN_MAX_SUBTILES // nsub)
    rows = nsub * _TQ

    def cur(which):
        return pl.BlockSpec((None, nr, rows, D_MODEL), lambda bi, r, i: (bi, r, i, which))

    def prev(which):
        return pl.BlockSpec((None, nr, _TQ, D_MODEL),
                            lambda bi, r, i: (bi, r, jnp.maximum(i * nsub - 1, 0), which))

    def nxt(which):
        return pl.BlockSpec((None, nr, _TQ, D_MODEL),
                            lambda bi, r, i: (bi, r, jnp.minimum((i + 1) * nsub, n_tq - 1), which))

    return pl.pallas_call(
        functools.partial(_attn_kernel, nr=nr, nsub=nsub), grid=(b, d // nr, n_tq // nsub),
        in_specs=[cur(0), prev(1), cur(1), nxt(1), prev(2), cur(2), nxt(2),
                  _const_spec(bias_tiles.shape)],
        out_specs=[pl.BlockSpec((None, nr, rows, D_MODEL), lambda bi, r, i: (bi, r, i, 0)),
                   pl.BlockSpec((None, nr, rows, _LANES), lambda bi, r, i: (bi, r, i, 0))],
        out_shape=[jax.ShapeDtypeStruct((b, d, lc, D_MODEL), _BF16),
                   jax.ShapeDtypeStruct((b, d, lc, _LANES), _F32)],
        compiler_params=_params("parallel", "parallel", "arbitrary"), name=f"attn_d{d}")(
            qkv, qkv, qkv, qkv, qkv, qkv, qkv, bias_tiles)


def _head_expand_matrix():
    e = np.zeros((2 * _LANES, D_MODEL), np.float32)
    for h in range(HEADS_PER_GROUP):
        e[h, h * HEAD_DIM:(h + 1) * HEAD_DIM] = 1.0
        e[_LANES + h, h * HEAD_DIM:(h + 1) * HEAD_DIM] = 1.0
    return jnp.asarray(e, _BF16)


def _dilated_attention_layer(x, norm_g, w_qkv, w_o, rel_bias):
    n3 = 3 * D_MODEL
    dilations = tuple(dil for _, dil in DILATED_GROUPS)
    assert all((window // 2) // dil == _HALF_W for window, dil in DILATED_GROUPS)
    is_q = (jnp.arange(N_GROUPS * n3) % n3) < D_MODEL
    col_scale = jnp.where(is_q, _LOG2E * HEAD_DIM ** -0.5, 1.0).astype(_F32)
    qkvs = _qkv(x, norm_g, (w_qkv.astype(_F32) * col_scale).astype(_BF16), dilations)
    os_, stats = [], []
    for g, (dil, qkv) in enumerate(zip(dilations, qkvs)):
        bias = _attn_bias_tiles(rel_bias[:, g * HEADS_PER_GROUP:(g + 1) * HEADS_PER_GROUP], dil)
        o, stat = _group_attention(qkv, bias)
        os_.append(o)
        stats.append(stat)
    return os_, stats, w_o.astype(_BF16)


def _hyena_in_kernel(xp_ref, x_ref, xn_ref, g_ref, w_ref, cw_ref, cb_ref,
                     x1_ref, x2_ref, v_ref, *, tm, sub):
    i = pl.program_id(1)
    last = pl.num_programs(1) - 1
    keep_prev = jnp.where(i > 0, 1.0, 0.0)
    keep_next = jnp.where(i < last, 1.0, 0.0)
    xe = jnp.concatenate([xp_ref[...] * keep_prev, x_ref[...], xn_ref[...] * keep_next], axis=0)
    n = sub + 16
    for h in range(tm // sub):
        xb = _rms(xe[h * sub:h * sub + n], g_ref[...]).astype(_BF16)
        rows = slice(h * sub, (h + 1) * sub)
        for j, o_ref in enumerate((x1_ref, x2_ref, v_ref)):
            cs = slice(j * D_MODEL, (j + 1) * D_MODEL)
            u = jnp.dot(xb, w_ref[:, cs], preferred_element_type=_F32)
            u_prev = pltpu.roll(u, 1, 0)[8:sub + 8]
            u_next = pltpu.roll(u, n - 1, 0)[8:sub + 8]
            cw = cw_ref[:, cs]
            y = (u_prev * cw[0:1] + u[8:sub + 8] * cw[1:2] + u_next * cw[2:3]) + cb_ref[:, cs]
            o_ref[rows, :] = y.astype(o_ref.dtype)


def _hyena_in(x, norm_g, w_in_bf16, conv_w, conv_b, tm=1024, sub=512):
    b, s, d = x.shape
    n3 = w_in_bf16.shape[1]
    nblk8 = s // 8
    r8 = tm // 8
    main = pl.BlockSpec((None, tm, d), lambda bi, i: (bi, i, 0))
    prev = pl.BlockSpec((None, 8, d), lambda bi, i: (bi, jnp.maximum(i * r8 - 1, 0), 0))
    nxt = pl.BlockSpec((None, 8, d), lambda bi, i: (bi, jnp.minimum((i + 1) * r8, nblk8 - 1), 0))
    out = jax.ShapeDtypeStruct((b, s, d), _BF16)
    return pl.pallas_call(
        functools.partial(_hyena_in_kernel, tm=tm, sub=sub), grid=(b, s // tm),
        in_specs=[prev, main, nxt, _const_spec((1, d)), _const_spec(w_in_bf16.shape),
                  _const_spec((3, n3)), _const_spec((1, n3))],
        out_specs=[main, main, main], out_shape=[out, out, out],
        compiler_params=_params("parallel", "arbitrary"), name="hyena_in")(
            x, x, x, norm_g.reshape(1, d), w_in_bf16, conv_w.astype(_F32),
            conv_b.astype(_F32).reshape(1, n3))


def _filter_kernel(fr_ref, ph_ref, w1a_ref, w1b_ref, b1_ref, w2_ref, b2_ref, w3_ref, b3_ref, fq_ref,
                   w4h_ref, w4l_ref, dl_ref, k0_ref, k1_ref, *, seq_len, tr):
    hi = lax.Precision.HIGHEST
    half = tr // 2
    r = pl.program_id(0) * tr + lax.broadcasted_iota(jnp.int32, (tr, 1), 0)
    pos = jnp.abs(r - seq_len).astype(_F32)
    t = pos / (seq_len - 1.0)
    wpos = (2.0 * math.pi) * pos / seq_len
    lane = lax.broadcasted_iota(jnp.int32, (1, _LANES), 1)
    z = jnp.where(lane == 0, t,
                  jnp.where(lane < HYENA_EMB_DIM, jnp.cos(fr_ref[...] * wpos + ph_ref[...]), 0.0))
    dot = functools.partial(jnp.dot, precision=hi, preferred_element_type=_F32)
    fq = fq_ref[...]
    h = jnp.sin(fq * (dot(z[:half], w1a_ref[...]) + dot(z[half:], w1b_ref[...]) + b1_ref[...]))
    h = jnp.sin(fq * (dot(h, w2_ref[...]) + b2_ref[...]))
    h = jnp.sin(fq * (dot(h, w3_ref[...]) + b3_ref[...]))
    decay = jnp.exp(-t * jnp.abs(dl_ref[...]))
    live = r > 0
    h_hi = h.astype(_BF16)
    h_lo = (h - h_hi.astype(_F32)).astype(_BF16)
    for o, k_ref in enumerate((k0_ref, k1_ref)):
        cs = slice(o * D_MODEL, (o + 1) * D_MODEL)
        for part in range(2):
            w_hi = w4h_ref[part, :, cs]
            y = jnp.dot(h_hi, w_hi, preferred_element_type=_F32)
            y = y + jnp.dot(h_hi, w4l_ref[part, :, cs], preferred_element_type=_F32)
            y = y + jnp.dot(h_lo, w_hi, preferred_element_type=_F32)
            rows = slice(part * half, (part + 1) * half)
            k_ref[rows, :] = jnp.where(live[rows], y * decay[rows], 0.0)


def _hyena_two_sided_filters(seq_len, w1, b1, w2, b2, w3, b3, freq, w4, tr=512):
    fw = HYENA_FILTER_WIDTH
    assert 2 * fw == _LANES
    n_tiles = 2 * seq_len // tr
    f32 = lambda a: a.astype(_F32)
    zeros = jnp.zeros((fw, 2 * D_MODEL), _F32)
    w4d = f32(w4).reshape(fw, 2, 2, D_MODEL).transpose(2, 0, 1, 3).reshape(2, fw, 2 * D_MODEL)
    w4p = jnp.stack([jnp.stack([jnp.concatenate([w4d[dr], zeros]), jnp.concatenate([zeros, w4d[dr]])])
                     for dr in range(2)])
    w4_hi = w4p.astype(_BF16)
    w4_lo = (w4p - w4_hi.astype(_F32)).astype(_BF16)
    w4_spec = pl.BlockSpec((None, 2, _LANES, 2 * D_MODEL),
                           lambda i: (jnp.where(i >= n_tiles // 2, 0, 1), 0, 0, 0))
    bands = (HYENA_EMB_DIM - 1) // 2
    fr = jnp.linspace(1e-4, bands - 1, bands, dtype=_F32)
    fr_lanes = jnp.zeros((1, _LANES), _F32).at[0, 1:1 + bands].set(fr).at[0, 1 + bands:1 + 2 * bands].set(fr)
    phase = jnp.zeros((1, _LANES), _F32).at[0, 1 + bands:1 + 2 * bands].set(0.5 * math.pi)
    w1p = jnp.zeros((_LANES, fw), _F32).at[:HYENA_EMB_DIM].set(f32(w1))
    z64 = jnp.zeros((_LANES, fw), _F32)
    w1a = jnp.concatenate([w1p, z64], axis=1)
    w1b = jnp.concatenate([z64, w1p], axis=1)
    zz = jnp.zeros((fw, fw), _F32)
    bdiag = lambda w: jnp.block([[f32(w), zz], [zz, f32(w)]])
    twice = lambda a: jnp.tile(f32(a).reshape(1, fw), (1, 2))
    max_decay = math.log(DECAY_TARGET) / FAST_DECAY_PCT
    min_decay = math.log(DECAY_TARGET) / SLOW_DECAY_PCT
    deltas = jnp.linspace(min_decay, max_decay, D_MODEL, dtype=_F32).reshape(1, D_MODEL)
    out = jax.ShapeDtypeStruct((2 * seq_len, D_MODEL), _F32)
    cs = _const_spec
    sq, vec = cs((_LANES, _LANES)), cs((1, _LANES))
    return pl.pallas_call(
        functools.partial(_filter_kernel, seq_len=seq_len, tr=tr), grid=(n_tiles,),
        in_specs=[vec, vec, sq, sq, vec, sq, vec, sq, vec, vec, w4_spec, w4_spec, cs((1, D_MODEL))],
        out_specs=[pl.BlockSpec((tr, D_MODEL), lambda i: (i, 0))] * 2, out_shape=[out, out],
        compiler_params=_params("parallel"), name="hyena_filter")(
            fr_lanes, phase, w1a, w1b, twice(b1), bdiag(w2), twice(b2), bdiag(w3), twice(b3),
            twice(freq), w4_hi, w4_lo, deltas)


def _dft_matrices(p):
    n = 2 * p
    k = np.arange(p)[:, None].astype(np.float64)
    s = np.arange(p)[None, :].astype(np.float64)
    ang = np.pi * k * s / p
    fre = np.cos(ang)
    fim = -np.sin(ang)
    fim[0, :] = np.cos(np.pi * s[0])
    fwd_half = np.concatenate([fre, fim], axis=0)
    tt = (p + np.arange(p))[:, None].astype(np.float64)
    kk = np.arange(p)[None, :].astype(np.float64)
    ang2 = np.pi * kk * tt / p
    are = 2.0 * np.cos(ang2) / n
    are[:, 0] = 1.0 / n
    aim = -2.0 * np.sin(ang2) / n
    aim[:, 0] = np.cos(np.pi * tt[:, 0]) / n
    inv = np.concatenate([are, aim], axis=1)
    return jnp.asarray(fwd_half, _BF16), jnp.asarray(inv, _BF16)


def _filter_spectrum_kernel(g0_ref, ga_ref, gb_ref, f_ref, o_ref, prev_ref):
    spectrum = lambda g_ref: jnp.dot(f_ref[...], g_ref[...].astype(_BF16),
                                     preferred_element_type=_F32)

    @pl.when(pl.program_id(1) == 0)
    def _():
        prev_ref[...] = spectrum(g0_ref)

    aa, ab = spectrum(ga_ref), spectrum(gb_ref)
    odd = (lax.broadcasted_iota(jnp.int32, (aa.shape[0], 1), 0) & 1) == 1
    o_ref[0] = (prev_ref[...] + jnp.where(odd, -aa, aa)).astype(o_ref.dtype)
    o_ref[1] = (aa + jnp.where(odd, -ab, ab)).astype(o_ref.dtype)
    prev_ref[...] = ab


def _filter_spectrum(kt, fwd_half, p, tc=256):
    rows, d = kt.shape
    nblk = rows // p
    blk = lambda imap: pl.BlockSpec((p, tc), imap)
    return pl.pallas_call(
        _filter_spectrum_kernel, grid=(d // tc, nblk // 2),
        in_specs=[blk(lambda c, s: (0, c)), blk(lambda c, s: (2 * s + 1, c)),
                  blk(lambda c, s: (jnp.minimum(2 * s + 2, nblk - 1), c)),
                  _const_spec(fwd_half.shape)],
        out_specs=pl.BlockSpec((2, 2 * p, tc), lambda c, s: (s, 0, c)),
        out_shape=jax.ShapeDtypeStruct((nblk, 2 * p, d), _SPECTRUM_DTYPE),
        scratch_shapes=[pltpu.VMEM((2 * p, tc), _F32)],
        compiler_params=_params("parallel", "arbitrary"), name="hyena_filter_spectrum")(
            kt, kt, kt, fwd_half)


def _long_conv_kernel(gate_ref, z_ref, kf_ref, skip_ref, ff_ref, ai_ref, o_ref, zf_ref, yfa_ref,
                      yfb_ref, *, p, nb, tc, ne):
    ch = _CONV_ROWS
    slots = zf_ref.shape[0]
    skip = skip_ref[...]
    first_row = lax.broadcasted_iota(jnp.int32, (ch, 1), 0) == 0

    def forward(e, i):
        zb = z_ref[e, pl.ds(pl.multiple_of(i * p, p), p), :]
        zf_ref[e % slots, i] = jnp.dot(ff_ref[...], zb,
                                       preferred_element_type=_F32).astype(zf_ref.dtype)

    def spectrum(e, j, yf_ref):
        for rc in range(p // ch):
            re = slice(rc * ch, (rc + 1) * ch)
            im = slice(p + rc * ch, p + (rc + 1) * ch)
            a = bm = cc = None
            for i0 in range(0, nb, _SPECTRUM_GROUP):
                ga = jnp.zeros((ch, tc), zf_ref.dtype)
                gb = jnp.zeros((ch, tc), zf_ref.dtype)
                gc = jnp.zeros((ch, tc), zf_ref.dtype)
                for i in range(i0, min(i0 + _SPECTRUM_GROUP, nb)):
                    q = j - i + (nb - 1)
                    zre, zim = zf_ref[e % slots, i, re, :], zf_ref[e % slots, i, im, :]
                    kre, kim = kf_ref[q, re, :], kf_ref[q, im, :]
                    ga = ga + zre * kre
                    gb = gb + zim * kim
                    gc = gc + (zre * kim + zim * kre)
                if nb > _SPECTRUM_GROUP:
                    ga, gb, gc = ga.astype(_F32), gb.astype(_F32), gc.astype(_F32)
                a = ga if a is None else a + ga
                bm = gb if bm is None else bm + gb
                cc = gc if cc is None else cc + gc
            if rc == 0:
                yf_ref[re, :] = jnp.where(first_row, a, a - bm).astype(_BF16)
                yf_ref[im, :] = jnp.where(first_row, bm, cc).astype(_BF16)
            else:
                yf_ref[re, :] = (a - bm).astype(_BF16)
                yf_ref[im, :] = cc.astype(_BF16)

    def finish(e, j, yf_ref):
        y = jnp.dot(ai_ref[...], yf_ref[...], preferred_element_type=_F32)
        rows = pl.ds(pl.multiple_of(j * p, p), p)
        zt = z_ref[e, rows, :].astype(_F32)
        gt = gate_ref[e, rows, :].astype(_F32)
        o_ref[e, rows, :] = (gt * (y + zt * skip)).astype(o_ref.dtype)

    yf = (yfa_ref, yfb_ref)
    if nb <= _CONV_STATIC_BLOCKS:
        for i in range(nb):
            forward(0, i)
        pending = None
        for e in range(ne):
            for j in range(nb):
                if pending is not None:
                    finish(*pending)
                if e + 1 < ne:
                    forward(e + 1, j)
                spectrum(e, j, yf[j % 2])
                pending = (e, j, yf[j % 2])
        finish(*pending)
    else:
        assert ne == 1 and nb % 2 == 0
        for i in range(nb):
            forward(0, i)
        spectrum(0, 0, yfa_ref)

        def pair_body(k, carry):
            finish(0, 2 * k, yfa_ref)
            spectrum(0, 2 * k + 1, yfb_ref)
            finish(0, 2 * k + 1, yfb_ref)
            spectrum(0, 2 * k + 2, yfa_ref)
            return carry

        lax.fori_loop(0, nb // 2 - 1, pair_body, 0)
        finish(0, nb - 2, yfa_ref)
        spectrum(0, nb - 1, yfb_ref)
        finish(0, nb - 1, yfb_ref)


def _long_conv(gate, z, kf, skip_o, fwd_half, inv, p):
    b, l, d = z.shape
    nb = l // p
    ne = 2 if (nb <= _CONV_STATIC_BLOCKS and b % 2 == 0) else 1
    slots = min(ne, 2)
    spectra = (2 * nb - 1 + slots * nb) * 2 * p * jnp.dtype(_SPECTRUM_DTYPE).itemsize
    seq_bytes = ne * l * 2
    dft_bytes = 2 * (2 * p * p * 2)
    budget = _V7X_VMEM_LIMIT_BYTES * 9 // 10
    tc, io_buffers = 128, 2
    for cand_tc, cand_buffers in ((256, 2), (256, 1)):
        if cand_tc * (spectra + (2 + 2 * cand_buffers) * seq_bytes) + dft_bytes <= budget:
            tc, io_buffers = cand_tc, cand_buffers
            break
    io_mode = {} if io_buffers == 2 else {"pipeline_mode": pl.Buffered(1)}
    seq = pl.BlockSpec((ne, l, tc), lambda c, bi: (bi, 0, c))
    seq_io = pl.BlockSpec((ne, l, tc), lambda c, bi: (bi, 0, c), **io_mode)
    kf_spec = pl.BlockSpec((2 * nb - 1, 2 * p, tc), lambda c, bi: (0, 0, c),
                           pipeline_mode=pl.Buffered(1))
    return pl.pallas_call(
        functools.partial(_long_conv_kernel, p=p, nb=nb, tc=tc, ne=ne), grid=(d // tc, b // ne),
        in_specs=[seq_io, seq, kf_spec, pl.BlockSpec((1, tc), lambda c, bi: (0, c)),
                  _const_spec(fwd_half.shape), _const_spec(inv.shape)],
        out_specs=seq_io, out_shape=jax.ShapeDtypeStruct((b, l, d), _BF16),
        scratch_shapes=[pltpu.VMEM((slots, nb, 2 * p, tc), _SPECTRUM_DTYPE),
                        pltpu.VMEM((2 * p, tc), _BF16),
                        pltpu.VMEM((2 * p, tc), _BF16)],
        compiler_params=_params("parallel", "arbitrary"), name="hyena_long_conv")(
            gate, z, kf, skip_o.astype(_F32).reshape(1, d), fwd_half, inv)


def _hyena_layer(x, norm_g, w_in, conv_w, conv_b, w1, b1, w2, b2, w3, b3, freq, w4, skip, w_out):
    b, l, d = x.shape
    p = min(l, max(_CONV_P, l // 8))
    x1, x2, v = _hyena_in(x, norm_g, w_in.astype(_BF16), conv_w, conv_b)
    k0, k1 = _hyena_two_sided_filters(l, w1, b1, w2, b2, w3, b3, freq, w4)
    fwd_half, inv = _dft_matrices(p)
    z = v
    for o, (gate, kt) in enumerate(((x1, k0), (x2, k1))):
        kf = _filter_spectrum(kt, fwd_half, p)
        z = _long_conv(gate, z, kf, skip[o], fwd_half, inv, p)
    return z.reshape(b * l, d), w_out.astype(_BF16)


def _trunk(x, rel_bias, ffn1_norm, ffn1_w, mixer_norm, attn_w_qkv, attn_w_o, hy, ffn2_norm, ffn2_w,
           final_norm):
    b, s, d = x.shape
    depth = ffn1_norm.shape[0]
    flat = lambda a: a.reshape(b * s, d)
    for i in range(depth):
        x = _ffn(flat(x), ffn1_norm[i], *ffn1_w, i).reshape(b, s, d)
        j = i // 2
        if i % 2 == 0:
            os_, stats, w_o = _dilated_attention_layer(x, mixer_norm[i], attn_w_qkv[j], attn_w_o[j],
                                                       rel_bias)
            pre = ("merge", os_, stats, s, w_o)
        else:
            pre = ("matmul", *_hyena_layer(x, mixer_norm[i], *[a[j] for a in hy]))
        fin = final_norm if i == depth - 1 else None
        x = _ffn(flat(x), ffn2_norm[i], *ffn2_w, i, pre=pre, final_g=fin).reshape(b, s, d)
    return x


def kernel(x_prompt, x_sample, rel_bias, ffn1_norm, ffn1_w_gate_up, ffn1_w_down, mixer_norm, attn_w_qkv, attn_w_o, hyena_w_in, hyena_conv_w, hyena_conv_b, hyena_filt_w1, hyena_filt_b1, hyena_filt_w2, hyena_filt_b2, hyena_filt_w3, hyena_filt_b3, hyena_filt_freq, hyena_filt_w4, hyena_skip, hyena_w_out, ffn2_norm, ffn2_w_gate_up, ffn2_w_down, final_norm):
    depth = ffn1_norm.shape[0]
    ffn1_w = (ffn1_w_gate_up.astype(_BF16), ffn1_w_down.astype(_BF16))
    ffn2_w = (ffn2_w_gate_up.astype(_BF16), ffn2_w_down.astype(_BF16))
    hy = (hyena_w_in, hyena_conv_w, hyena_conv_b, hyena_filt_w1, hyena_filt_b1, hyena_filt_w2,
          hyena_filt_b2, hyena_filt_w3, hyena_filt_b3, hyena_filt_freq, hyena_filt_w4, hyena_skip,
          hyena_w_out)
    args = (rel_bias, ffn1_norm, ffn1_w, mixer_norm, attn_w_qkv, attn_w_o, hy, ffn2_norm, ffn2_w,
            final_norm)
    return (_trunk(x_prompt, *args), _trunk(x_sample, *args))
```

```python
import functools
import math

import numpy as np
import jax
import jax.numpy as jnp
from jax import lax
from jax.experimental import pallas as pl
from jax.experimental.pallas import tpu as pltpu

D_MODEL = 1024
HEAD_DIM = 64
HEADS_PER_GROUP = D_MODEL // HEAD_DIM
DILATED_GROUPS = ((128, 1), (512, 4), (2048, 16))
N_GROUPS = len(DILATED_GROUPS)
NUM_BUCKETS = 32
MAX_DISTANCE = 1024
HYENA_EMB_DIM = 33
HYENA_FILTER_WIDTH = 64
FAST_DECAY_PCT = 0.3
SLOW_DECAY_PCT = 1.5
DECAY_TARGET = 1e-2
RMS_EPS = 1e-6

_F32 = jnp.float32
_BF16 = jnp.bfloat16
_V7X_VMEM_LIMIT_BYTES = 58 * 1024 * 1024
_LANES = 128
_NEG = -1e30
_LOG2E = math.log2(math.e)

_TQ = 128
_HALF_W = 64
_TK = _TQ + 2 * _HALF_W
_ATTN_MAX_SUBTILES = 8
_MXU_DIM = 256
_CONV_P = 512
_CONV_ROWS = 32
_SPECTRUM_DTYPE = jnp.bfloat16
_SPECTRUM_GROUP = 4
_CONV_STATIC_BLOCKS = 4


def _params(*sem):
    return pltpu.CompilerParams(dimension_semantics=sem, vmem_limit_bytes=_V7X_VMEM_LIMIT_BYTES)


def _const_spec(shape):
    nd = len(shape)
    return pl.BlockSpec(shape, lambda *_: (0,) * nd, pipeline_mode=pl.Buffered(1))


def _layer_spec(stacked, layer):
    nd = stacked.ndim - 1
    return pl.BlockSpec((None,) + stacked.shape[1:], lambda *_: (layer,) + (0,) * nd,
                        pipeline_mode=pl.Buffered(1))


def _rms(x, g):
    y = x * lax.rsqrt(jnp.mean(x * x, axis=-1, keepdims=True) + RMS_EPS)
    return y * g


def _merge_rows(o_refs, s_refs, e_ref, of_ref, lf_ref, h, sub):
    nblk = D_MODEL // _LANES

    def natural(o_ref, s_ref, slot):
        d = o_ref.shape[0]
        n = sub // d
        cls = slice(h * n, (h + 1) * n)
        if d == 1:
            return o_ref[0, cls, :].astype(_F32), s_ref[0, cls, :]
        for r in range(d):
            o_r = o_ref[r, cls, :].astype(_F32)
            for c in range(nblk):
                of_ref[slot, c, pl.ds(r, n, stride=d), :] = o_r[:, c * _LANES:(c + 1) * _LANES]
            lf_ref[slot, pl.ds(r, n, stride=d), :] = s_ref[r, cls, :]
        return jnp.concatenate([of_ref[slot, c] for c in range(nblk)], axis=1), lf_ref[slot]

    outs, stats = [], []
    for g, (o_ref, s_ref) in enumerate(zip(o_refs, s_refs)):
        o, st = natural(o_ref, s_ref, 2 * h + (g % 2))
        outs.append(o)
        stats.append(st)
    head = lax.broadcasted_iota(jnp.int32, (1, _LANES), 1) < HEADS_PER_GROUP
    ms = [jnp.where(head, st, 0.0) for st in stats]
    ls = [jnp.where(head, pltpu.roll(st, _LANES - HEADS_PER_GROUP, 1), 1.0) for st in stats]
    m = functools.reduce(jnp.maximum, ms)
    ws = [jnp.exp2(mg - m) for mg in ms]
    inv = 1.0 / sum(w * l for w, l in zip(ws, ls))
    merged = None
    for w, o in zip(ws, outs):
        al = w * inv
        hi = al.astype(_BF16)
        lo = (al - hi.astype(_F32)).astype(_BF16)
        term = jnp.dot(jnp.concatenate([hi, lo], axis=1), e_ref[...], preferred_element_type=_F32) * o
        merged = term if merged is None else merged + term
    return merged


def _ffn_kernel(*refs, pre, final, sub):
    refs = list(refs)
    x_ref, g_ref, wgu_ref, wd_ref = refs[:4]
    extra = refs[4:]
    if pre == "matmul":
        a_ref, wpre_ref = extra[:2]
        extra = extra[2:]
    elif pre == "merge":
        o_refs, s_refs, (e_ref, wpre_ref) = extra[0:3], extra[3:6], extra[6:8]
        extra = extra[8:]
    gf_ref = extra.pop(0) if final else None
    o_ref = extra.pop(0)
    f = wd_ref.shape[0]
    n_tiles = f // _MXU_DIM
    bounds = [0, (n_tiles + 1) // 2 * _MXU_DIM, f]
    for h in range(x_ref.shape[0] // sub):
        rows = slice(h * sub, (h + 1) * sub)
        x = x_ref[rows, :]
        if pre == "matmul":
            x = x + jnp.dot(a_ref[rows, :], wpre_ref[...], preferred_element_type=_F32)
        elif pre == "merge":
            merged = _merge_rows(o_refs, s_refs, e_ref, extra[0], extra[1], h, sub)
            x = x + jnp.dot(merged.astype(_BF16), wpre_ref[...], preferred_element_type=_F32)
        xb = _rms(x, g_ref[...]).astype(_BF16)
        acc = jnp.zeros(x.shape, _F32)
        for lo, hi in zip(bounds[:-1], bounds[1:]):
            gate = jnp.dot(xb, wgu_ref[:, lo:hi], preferred_element_type=_F32)
            up = jnp.dot(xb, wgu_ref[:, f + lo:f + hi], preferred_element_type=_F32)
            act = (gate * jax.nn.sigmoid(gate)) * up
            acc = acc + jnp.dot(act.astype(_BF16), wd_ref[lo:hi, :], preferred_element_type=_F32)
        y = x + 0.5 * acc
        if final:
            y = _rms(y, gf_ref[...])
        o_ref[rows, :] = y


def _ffn(x2d, norm_g, wgu, wd, layer, pre=None, final_g=None, tm=1024, sub=512):
    t, d = x2d.shape
    kind = None if pre is None else pre[0]
    if kind == "merge":
        tm = sub
    row = lambda w: pl.BlockSpec((tm, w), lambda i: (i, 0))
    in_specs = [row(d), _const_spec((1, d)), _layer_spec(wgu, layer), _layer_spec(wd, layer)]
    args = [x2d, norm_g.reshape(1, d), wgu, wd]
    scratch = []
    if kind == "matmul":
        _, a2d, w_pre = pre
        in_specs += [row(a2d.shape[1]), _const_spec(w_pre.shape)]
        args += [a2d, w_pre]
    elif kind == "merge":
        _, os_, stats, seq_len, w_pre = pre
        nt = seq_len // tm
        e = _head_expand_matrix()

        def cls(a):
            dil, w = a.shape[1], a.shape[3]
            return pl.BlockSpec((None, dil, tm // dil, w), lambda i: (i // nt, 0, i % nt, 0))

        in_specs += [cls(a) for a in os_] + [cls(a) for a in stats]
        in_specs += [_const_spec(e.shape), _const_spec(w_pre.shape)]
        args += [*os_, *stats, e, w_pre]
        n_slots = 2 * (tm // sub)
        scratch = [pltpu.VMEM((n_slots, d // _LANES, sub, _LANES), _F32),
                   pltpu.VMEM((n_slots, sub, _LANES), _F32)]
    if final_g is not None:
        in_specs.append(_const_spec((1, d)))
        args.append(final_g.reshape(1, d))
    return pl.pallas_call(
        functools.partial(_ffn_kernel, pre=kind, final=final_g is not None, sub=sub),
        grid=(t // tm,), in_specs=in_specs, out_specs=row(d),
        out_shape=jax.ShapeDtypeStruct((t, d), _F32), scratch_shapes=scratch,
        compiler_params=_params("parallel"), name="ffn")(*args)


def _t5_bucket(rel):
    nb = NUM_BUCKETS // 2
    max_exact = nb // 2
    ret = (rel > 0).astype(np.int32) * nb
    n = np.abs(rel)
    large = max_exact + (np.log(np.maximum(n, 1) / max_exact)
                         / math.log(MAX_DISTANCE / max_exact) * (nb - max_exact)).astype(np.int32)
    large = np.minimum(large, nb - 1)
    return ret + np.where(n < max_exact, n, large)


def _attn_bucket_table(dilation):
    delta = np.arange(_TK)[None, :] - _HALF_W - np.arange(_TQ)[:, None]
    bucket = _t5_bucket(delta * dilation)
    return jnp.asarray(np.where(np.abs(delta) <= _HALF_W, bucket, -1), jnp.int32)


def _bias_tiles_kernel(rb_ref, bucket_ref, o_ref):
    hp = pl.program_id(0)
    bucket = bucket_ref[...]
    kk = lax.broadcasted_iota(jnp.int32, (1, _TK), 1)
    after_start = kk >= _HALF_W
    before_end = kk < _TQ + _HALF_W
    for hh in range(2):
        acc = jnp.full((_TQ, _TK), _NEG, _F32)
        for bkt in range(NUM_BUCKETS):
            acc = jnp.where(bucket == bkt, rb_ref[2 * hp + hh, bkt] * _LOG2E, acc)
        rows = slice(hh * _TQ, (hh + 1) * _TQ)
        o_ref[0, rows, :] = acc
        o_ref[1, rows, :] = jnp.where(after_start, acc, _NEG)
        o_ref[2, rows, :] = jnp.where(before_end, acc, _NEG)
        o_ref[3, rows, :] = jnp.where(after_start & before_end, acc, _NEG)


def _attn_bias_tiles(rel_bias_g, dilation):
    npair = HEADS_PER_GROUP // 2
    return pl.pallas_call(
        _bias_tiles_kernel, grid=(npair,),
        in_specs=[pl.BlockSpec(memory_space=pltpu.SMEM), _const_spec((_TQ, _TK))],
        out_specs=pl.BlockSpec((4, None, 2 * _TQ, _TK), lambda hp: (0, hp, 0, 0)),
        out_shape=jax.ShapeDtypeStruct((4, npair, 2 * _TQ, _TK), _F32),
        compiler_params=_params("parallel"), name="attn_bias_tiles")(
            rel_bias_g.astype(_F32).T, _attn_bucket_table(dilation))


def _attn_kernel(q_ref, kp_ref, kc_ref, kn_ref, vp_ref, vc_ref, vn_ref, bias_ref,
                 o_ref, stat_ref, *, nr, nsub):
    i = pl.program_id(2)
    is_first = jnp.where(i == 0, 1, 0)
    is_last = jnp.where(i == pl.num_programs(2) - 1, 2, 0)
    lane = lax.broadcasted_iota(jnp.int32, (1, _LANES), 1)
    low = lane < HEAD_DIM
    zero = jnp.zeros((), _BF16)
    cur_rows = nsub * _TQ

    def window(prev_ref, cur_ref, next_ref, ri, sub, sl):
        start, end = sub * _TQ - _HALF_W, (sub + 1) * _TQ + _HALF_W
        pieces = []
        if start < 0:
            pieces.append(prev_ref[ri, _TQ - _HALF_W:, sl])
        pieces.append(cur_ref[ri, max(start, 0):min(end, cur_rows), sl])
        if end > cur_rows:
            pieces.append(next_ref[ri, :_HALF_W, sl])
        return jnp.concatenate(pieces, axis=0)

    for ri in range(nr):
        for sub in range(nsub):
            rows = slice(sub * _TQ, (sub + 1) * _TQ)
            variant = (is_first if sub == 0 else 0) + (is_last if sub == nsub - 1 else 0)
            stat = jnp.zeros((_TQ, _LANES), _F32)
            for hp in range(HEADS_PER_GROUP // 2):
                sl = slice(hp * _LANES, (hp + 1) * _LANES)
                q2 = q_ref[ri, rows, sl]
                qq = jnp.concatenate([jnp.where(low, q2, zero), jnp.where(low, zero, q2)], axis=0)
                k2 = window(kp_ref, kc_ref, kn_ref, ri, sub, sl)
                v2 = window(vp_ref, vc_ref, vn_ref, ri, sub, sl)
                s = lax.dot_general(qq, k2, (((1,), (1,)), ((), ())),
                                    preferred_element_type=_F32)
                s = s + bias_ref[variant, hp]
                m = jnp.max(s, axis=-1, keepdims=True)
                p = jnp.exp2(s - m)
                l = jnp.sum(p, axis=-1, keepdims=True)
                pv = jnp.dot(p.astype(_BF16), v2, preferred_element_type=_F32)
                o_ref[ri, rows, sl] = jnp.where(low, pv[:_TQ], pv[_TQ:]).astype(o_ref.dtype)
                h0, h1 = 2 * hp, 2 * hp + 1
                stat = jnp.where(lane == h0, m[:_TQ], jnp.where(lane == h1, m[_TQ:], stat))
                stat = jnp.where(lane == HEADS_PER_GROUP + h0, l[:_TQ],
                                 jnp.where(lane == HEADS_PER_GROUP + h1, l[_TQ:], stat))
            stat_ref[ri, rows, :] = stat


def _qkv_kernel(x_ref, g_ref, w_ref, *rest, dilations, tm):
    o_refs = rest[:len(dilations)]
    xn_ref, xp_ref = rest[len(dilations):]
    nblk = D_MODEL // _LANES
    xn = _rms(x_ref[...], g_ref[...])
    for c in range(nblk):
        xn_ref[c] = xn[:, c * _LANES:(c + 1) * _LANES]
    for g, (d, o_ref) in enumerate(zip(dilations, o_refs)):
        rows = tm // d
        if d == 1:
            xp = xn.astype(_BF16)
        else:
            for r in range(d):
                for c in range(nblk):
                    xp_ref[g % 2, r * rows:(r + 1) * rows, c * _LANES:(c + 1) * _LANES] = (
                        xn_ref[c, pl.ds(r, rows, stride=d), :].astype(_BF16))
            xp = xp_ref[g % 2]
        for j in range(3):
            cs = slice(j * D_MODEL, (j + 1) * D_MODEL)
            wcols = slice(g * 3 * D_MODEL + j * D_MODEL, g * 3 * D_MODEL + (j + 1) * D_MODEL)
            y = jnp.dot(xp, w_ref[:, wcols], preferred_element_type=_F32).astype(o_ref.dtype)
            for r in range(d):
                o_ref[r, :, cs] = y[r * rows:(r + 1) * rows]


def _qkv(x, norm_g, w_bf16, dilations, tm=512):
    b, s, dm = x.shape
    n3 = w_bf16.shape[1] // len(dilations)
    return pl.pallas_call(
        functools.partial(_qkv_kernel, dilations=dilations, tm=tm), grid=(b, s // tm),
        in_specs=[pl.BlockSpec((None, tm, dm), lambda bi, i: (bi, i, 0)), _const_spec((1, dm)),
                  _const_spec(w_bf16.shape)],
        out_specs=[pl.BlockSpec((None, d, tm // d, n3), lambda bi, i: (bi, 0, i, 0))
                   for d in dilations],
        out_shape=[jax.ShapeDtypeStruct((b, d, s // d, n3), _BF16) for d in dilations],
        scratch_shapes=[pltpu.VMEM((dm // _LANES, tm, _LANES), _F32),
                        pltpu.VMEM((2, tm, dm), _BF16)],
        compiler_params=_params("parallel", "parallel"), name="qkv")(
            x, norm_g.reshape(1, dm), w_bf16)


def _group_attention(qkv, bias_tiles):
    b, d, lc, _ = qkv.shape
    n_tq = lc // _TQ
    nsub = min(_ATTN_MAX_SUBTILES, n_tq)
    nr = min(d, _ATTN_MAX_SUBTILES // nsub)
    rows = nsub * _TQ

    def cur(which):
        return pl.BlockSpec((None, nr, rows, D_MODEL), lambda bi, r, i: (bi, r, i, which))

    def prev(which):
        if n_tq == nsub:
            return pl.BlockSpec((None, nr, _TQ, D_MODEL), lambda bi, r, i: (0, 0, 0, which),
                                pipeline_mode=pl.Buffered(1))
        return pl.BlockSpec((None, nr, _TQ, D_MODEL),
                            lambda bi, r, i: (bi, r, jnp.maximum(i * nsub - 1, 0), which))

    def nxt(which):
        if n_tq == nsub:
            return prev(which)
        return pl.BlockSpec((None, nr, _TQ, D_MODEL),
                            lambda bi, r, i: (bi, r, jnp.minimum((i + 1) * nsub, n_tq - 1), which))

    return pl.pallas_call(
        functools.partial(_attn_kernel, nr=nr, nsub=nsub), grid=(b, d // nr, n_tq // nsub),
        in_specs=[cur(0), prev(1), cur(1), nxt(1), prev(2), cur(2), nxt(2),
                  _const_spec(bias_tiles.shape)],
        out_specs=[pl.BlockSpec((None, nr, rows, D_MODEL), lambda bi, r, i: (bi, r, i, 0)),
                   pl.BlockSpec((None, nr, rows, _LANES), lambda bi, r, i: (bi, r, i, 0))],
        out_shape=[jax.ShapeDtypeStruct((b, d, lc, D_MODEL), _BF16),
                   jax.ShapeDtypeStruct((b, d, lc, _LANES), _F32)],
        compiler_params=_params("parallel", "parallel", "arbitrary"), name=f"attn_d{d}")(
            qkv, qkv, qkv, qkv, qkv, qkv, qkv, bias_tiles)


def _head_expand_matrix():
    e = np.zeros((2 * _LANES, D_MODEL), np.float32)
    for h in range(HEADS_PER_GROUP):
        e[h, h * HEAD_DIM:(h + 1) * HEAD_DIM] = 1.0
        e[_LANES + h, h * HEAD_DIM:(h + 1) * HEAD_DIM] = 1.0
    return jnp.asarray(e, _BF16)


def _dilated_attention_layer(x, norm_g, w_qkv, w_o, rel_bias):
    n3 = 3 * D_MODEL
    dilations = tuple(dil for _, dil in DILATED_GROUPS)
    assert all((window // 2) // dil == _HALF_W for window, dil in DILATED_GROUPS)
    is_q = (jnp.arange(N_GROUPS * n3) % n3) < D_MODEL
    col_scale = jnp.where(is_q, _LOG2E * HEAD_DIM ** -0.5, 1.0).astype(_F32)
    qkvs = _qkv(x, norm_g, (w_qkv.astype(_F32) * col_scale).astype(_BF16), dilations)
    os_, stats = [], []
    for g, (dil, qkv) in enumerate(zip(dilations, qkvs)):
        bias = _attn_bias_tiles(rel_bias[:, g * HEADS_PER_GROUP:(g + 1) * HEADS_PER_GROUP], dil)
        o, stat = _group_attention(qkv, bias)
        os_.append(o)
        stats.append(stat)
    return os_, stats, w_o.astype(_BF16)


def _hyena_in_kernel(xp_ref, x_ref, xn_ref, g_ref, w_ref, cw_ref, cb_ref,
                     x1_ref, x2_ref, v_ref, *, tm, sub):
    i = pl.program_id(1)
    last = pl.num_programs(1) - 1
    keep_prev = jnp.where(i > 0, 1.0, 0.0)
    keep_next = jnp.where(i < last, 1.0, 0.0)
    xe = jnp.concatenate([xp_ref[...] * keep_prev, x_ref[...], xn_ref[...] * keep_next], axis=0)
    n = sub + 16
    for h in range(tm // sub):
        xb = _rms(xe[h * sub:h * sub + n], g_ref[...]).astype(_BF16)
        rows = slice(h * sub, (h + 1) * sub)
        for j, o_ref in enumerate((x1_ref, x2_ref, v_ref)):
            cs = slice(j * D_MODEL, (j + 1) * D_MODEL)
            u = jnp.dot(xb, w_ref[:, cs], preferred_element_type=_F32)
            u_prev = pltpu.roll(u, 1, 0)[8:sub + 8]
            u_next = pltpu.roll(u, n - 1, 0)[8:sub + 8]
            cw = cw_ref[:, cs]
            y = (u_prev * cw[0:1] + u[8:sub + 8] * cw[1:2] + u_next * cw[2:3]) + cb_ref[:, cs]
            o_ref[rows, :] = y.astype(o_ref.dtype)


def _hyena_in(x, norm_g, w_in_bf16, conv_w, conv_b, tm=1024, sub=512):
    b, s, d = x.shape
    n3 = w_in_bf16.shape[1]
    nblk8 = s // 8
    r8 = tm // 8
    main = pl.BlockSpec((None, tm, d), lambda bi, i: (bi, i, 0))
    prev = pl.BlockSpec((None, 8, d), lambda bi, i: (bi, jnp.maximum(i * r8 - 1, 0), 0))
    nxt = pl.BlockSpec((None, 8, d), lambda bi, i: (bi, jnp.minimum((i + 1) * r8, nblk8 - 1), 0))
    out = jax.ShapeDtypeStruct((b, s, d), _BF16)
    return pl.pallas_call(
        functools.partial(_hyena_in_kernel, tm=tm, sub=sub), grid=(b, s // tm),
        in_specs=[prev, main, nxt, _const_spec((1, d)), _const_spec(w_in_bf16.shape),
                  _const_spec((3, n3)), _const_spec((1, n3))],
        out_specs=[main, main, main], out_shape=[out, out, out],
        compiler_params=_params("parallel", "arbitrary"), name="hyena_in")(
            x, x, x, norm_g.reshape(1, d), w_in_bf16, conv_w.astype(_F32),
            conv_b.astype(_F32).reshape(1, n3))


def _filter_kernel(fr_ref, ph_ref, w1a_ref, w1b_ref, b1_ref, w2_ref, b2_ref, w3_ref, b3_ref, fq_ref,
                   w4h_ref, w4l_ref, dl_ref, k0_ref, k1_ref, *, seq_len, tr):
    hi = lax.Precision.HIGHEST
    half = tr // 2
    r = pl.program_id(0) * tr + lax.broadcasted_iota(jnp.int32, (tr, 1), 0)
    pos = jnp.abs(r - seq_len).astype(_F32)
    t = pos / (seq_len - 1.0)
    wpos = (2.0 * math.pi) * pos / seq_len
    lane = lax.broadcasted_iota(jnp.int32, (1, _LANES), 1)
    z = jnp.where(lane == 0, t,
                  jnp.where(lane < HYENA_EMB_DIM, jnp.cos(fr_ref[...] * wpos + ph_ref[...]), 0.0))
    dot = functools.partial(jnp.dot, precision=hi, preferred_element_type=_F32)
    fq = fq_ref[...]
    h = jnp.sin(fq * (dot(z[:half], w1a_ref[...]) + dot(z[half:], w1b_ref[...]) + b1_ref[...]))
    h = jnp.sin(fq * (dot(h, w2_ref[...]) + b2_ref[...]))
    h = jnp.sin(fq * (dot(h, w3_ref[...]) + b3_ref[...]))
    decay = jnp.exp(-t * jnp.abs(dl_ref[...]))
    live = r > 0
    h_hi = h.astype(_BF16)
    h_lo = (h - h_hi.astype(_F32)).astype(_BF16)
    for o, k_ref in enumerate((k0_ref, k1_ref)):
        cs = slice(o * D_MODEL, (o + 1) * D_MODEL)
        for part in range(2):
            w_hi = w4h_ref[part, :, cs]
            y = jnp.dot(h_hi, w_hi, preferred_element_type=_F32)
            y = y + jnp.dot(h_hi, w4l_ref[part, :, cs], preferred_element_type=_F32)
            y = y + jnp.dot(h_lo, w_hi, preferred_element_type=_F32)
            rows = slice(part * half, (part + 1) * half)
            k_ref[rows, :] = jnp.where(live[rows], y * decay[rows], 0.0)


def _hyena_two_sided_filters(seq_len, w1, b1, w2, b2, w3, b3, freq, w4, tr=512):
    fw = HYENA_FILTER_WIDTH
    assert 2 * fw == _LANES
    n_tiles = 2 * seq_len // tr
    f32 = lambda a: a.astype(_F32)
    zeros = jnp.zeros((fw, 2 * D_MODEL), _F32)
    w4d = f32(w4).reshape(fw, 2, 2, D_MODEL).transpose(2, 0, 1, 3).reshape(2, fw, 2 * D_MODEL)
    w4p = jnp.stack([jnp.stack([jnp.concatenate([w4d[dr], zeros]), jnp.concatenate([zeros, w4d[dr]])])
                     for dr in range(2)])
    w4_hi = w4p.astype(_BF16)
    w4_lo = (w4p - w4_hi.astype(_F32)).astype(_BF16)
    w4_spec = pl.BlockSpec((None, 2, _LANES, 2 * D_MODEL),
                           lambda i: (jnp.where(i >= n_tiles // 2, 0, 1), 0, 0, 0))
    bands = (HYENA_EMB_DIM - 1) // 2
    fr = jnp.linspace(1e-4, bands - 1, bands, dtype=_F32)
    fr_lanes = jnp.zeros((1, _LANES), _F32).at[0, 1:1 + bands].set(fr).at[0, 1 + bands:1 + 2 * bands].set(fr)
    phase = jnp.zeros((1, _LANES), _F32).at[0, 1 + bands:1 + 2 * bands].set(0.5 * math.pi)
    w1p = jnp.zeros((_LANES, fw), _F32).at[:HYENA_EMB_DIM].set(f32(w1))
    z64 = jnp.zeros((_LANES, fw), _F32)
    w1a = jnp.concatenate([w1p, z64], axis=1)
    w1b = jnp.concatenate([z64, w1p], axis=1)
    zz = jnp.zeros((fw, fw), _F32)
    bdiag = lambda w: jnp.block([[f32(w), zz], [zz, f32(w)]])
    twice = lambda a: jnp.tile(f32(a).reshape(1, fw), (1, 2))
    max_decay = math.log(DECAY_TARGET) / FAST_DECAY_PCT
    min_decay = math.log(DECAY_TARGET) / SLOW_DECAY_PCT
    deltas = jnp.linspace(min_decay, max_decay, D_MODEL, dtype=_F32).reshape(1, D_MODEL)
    out = jax.ShapeDtypeStruct((2 * seq_len, D_MODEL), _F32)
    cs = _const_spec
    sq, vec = cs((_LANES, _LANES)), cs((1, _LANES))
    return pl.pallas_call(
        functools.partial(_filter_kernel, seq_len=seq_len, tr=tr), grid=(n_tiles,),
        in_specs=[vec, vec, sq, sq, vec, sq, vec, sq, vec, vec, w4_spec, w4_spec, cs((1, D_MODEL))],
        out_specs=[pl.BlockSpec((tr, D_MODEL), lambda i: (i, 0))] * 2, out_shape=[out, out],
        compiler_params=_params("parallel"), name="hyena_filter")(
            fr_lanes, phase, w1a, w1b, twice(b1), bdiag(w2), twice(b2), bdiag(w3), twice(b3),
            twice(freq), w4_hi, w4_lo, deltas)


def _dft_matrices(p):
    n = 2 * p
    k = np.arange(p)[:, None].astype(np.float64)
    s = np.arange(p)[None, :].astype(np.float64)
    ang = np.pi * k * s / p
    fre = np.cos(ang)
    fim = -np.sin(ang)
    fim[0, :] = np.cos(np.pi * s[0])
    fwd_half = np.concatenate([fre, fim], axis=0)
    tt = (p + np.arange(p))[:, None].astype(np.float64)
    kk = np.arange(p)[None, :].astype(np.float64)
    ang2 = np.pi * kk * tt / p
    are = 2.0 * np.cos(ang2) / n
    are[:, 0] = 1.0 / n
    aim = -2.0 * np.sin(ang2) / n
    aim[:, 0] = np.cos(np.pi * tt[:, 0]) / n
    inv = np.concatenate([are, aim], axis=1)
    return jnp.asarray(fwd_half, _BF16), jnp.asarray(inv, _BF16)


def _filter_spectrum_kernel(g0_ref, ga_ref, gb_ref, f_ref, o_ref, prev_ref):
    spectrum = lambda g_ref: jnp.dot(f_ref[...], g_ref[...].astype(_BF16),
                                     preferred_element_type=_F32)

    @pl.when(pl.program_id(1) == 0)
    def _():
        prev_ref[...] = spectrum(g0_ref)

    aa, ab = spectrum(ga_ref), spectrum(gb_ref)
    odd = (lax.broadcasted_iota(jnp.int32, (aa.shape[0], 1), 0) & 1) == 1
    o_ref[0] = (prev_ref[...] + jnp.where(odd, -aa, aa)).astype(o_ref.dtype)
    o_ref[1] = (aa + jnp.where(odd, -ab, ab)).astype(o_ref.dtype)
    prev_ref[...] = ab


def _filter_spectrum(kt, fwd_half, p, tc=256):
    rows, d = kt.shape
    nblk = rows // p
    blk = lambda imap: pl.BlockSpec((p, tc), imap)
    return pl.pallas_call(
        _filter_spectrum_kernel, grid=(d // tc, nblk // 2),
        in_specs=[blk(lambda c, s: (0, c)), blk(lambda c, s: (2 * s + 1, c)),
                  blk(lambda c, s: (jnp.minimum(2 * s + 2, nblk - 1), c)),
                  _const_spec(fwd_half.shape)],
        out_specs=pl.BlockSpec((2, 2 * p, tc), lambda c, s: (s, 0, c)),
        out_shape=jax.ShapeDtypeStruct((nblk, 2 * p, d), _SPECTRUM_DTYPE),
        scratch_shapes=[pltpu.VMEM((2 * p, tc), _F32)],
        compiler_params=_params("parallel", "arbitrary"), name="hyena_filter_spectrum")(
            kt, kt, kt, fwd_half)


def _long_conv_kernel(gate_ref, z_ref, kf_ref, skip_ref, ff_ref, ai_ref, o_ref, zf_ref, yfa_ref,
                      yfb_ref, *, p, nb, tc, ne):
    ch = _CONV_ROWS
    slots = zf_ref.shape[0]
    skip = skip_ref[...]
    first_row = lax.broadcasted_iota(jnp.int32, (ch, 1), 0) == 0

    def forward(e, i):
        zb = z_ref[e, pl.ds(pl.multiple_of(i * p, p), p), :]
        zf_ref[e % slots, i] = jnp.dot(ff_ref[...], zb,
                                       preferred_element_type=_F32).astype(zf_ref.dtype)

    def spectrum(e, j, yf_ref):
        for rc in range(p // ch):
            re = slice(rc * ch, (rc + 1) * ch)
            im = slice(p + rc * ch, p + (rc + 1) * ch)
            a = bm = cc = None
            for i0 in range(0, nb, _SPECTRUM_GROUP):
                ga = jnp.zeros((ch, tc), zf_ref.dtype)
                gb = jnp.zeros((ch, tc), zf_ref.dtype)
                gc = jnp.zeros((ch, tc), zf_ref.dtype)
                for i in range(i0, min(i0 + _SPECTRUM_GROUP, nb)):
                    q = j - i + (nb - 1)
                    zre, zim = zf_ref[e % slots, i, re, :], zf_ref[e % slots, i, im, :]
                    kre, kim = kf_ref[q, re, :], kf_ref[q, im, :]
                    ga = ga + zre * kre
                    gb = gb + zim * kim
                    gc = gc + (zre * kim + zim * kre)
                if nb > _SPECTRUM_GROUP:
                    ga, gb, gc = ga.astype(_F32), gb.astype(_F32), gc.astype(_F32)
                a = ga if a is None else a + ga
                bm = gb if bm is None else bm + gb
                cc = gc if cc is None else cc + gc
            if rc == 0:
                yf_ref[re, :] = jnp.where(first_row, a, a - bm).astype(_BF16)
                yf_ref[im, :] = jnp.where(first_row, bm, cc).astype(_BF16)
            else:
                yf_ref[re, :] = (a - bm).astype(_BF16)
                yf_ref[im, :] = cc.astype(_BF16)

    def finish(e, j, yf_ref):
        y = jnp.dot(ai_ref[...], yf_ref[...], preferred_element_type=_F32)
        rows = pl.ds(pl.multiple_of(j * p, p), p)
        zt = z_ref[e, rows, :].astype(_F32)
        gt = gate_ref[e, rows, :].astype(_F32)
        o_ref[e, rows, :] = (gt * (y + zt * skip)).astype(o_ref.dtype)

    yf = (yfa_ref, yfb_ref)
    if nb <= _CONV_STATIC_BLOCKS:
        for i in range(nb):
            forward(0, i)
        pending = None
        for e in range(ne):
            for j in range(nb):
                if pending is not None:
                    finish(*pending)
                if e + 1 < ne:
                    forward(e + 1, j)
                spectrum(e, j, yf[j % 2])
                pending = (e, j, yf[j % 2])
        finish(*pending)
    else:
        assert ne == 1 and nb % 2 == 0
        for i in range(nb):
            forward(0, i)
        spectrum(0, 0, yfa_ref)

        def pair_body(k, carry):
            finish(0, 2 * k, yfa_ref)
            spectrum(0, 2 * k + 1, yfb_ref)
            finish(0, 2 * k + 1, yfb_ref)
            spectrum(0, 2 * k + 2, yfa_ref)
            return carry

        lax.fori_loop(0, nb // 2 - 1, pair_body, 0)
        finish(0, nb - 2, yfa_ref)
        spectrum(0, nb - 1, yfb_ref)
        finish(0, nb - 1, yfb_ref)


def _long_conv(gate, z, kf, skip_o, fwd_half, inv, p):
    b, l, d = z.shape
    nb = l // p
    ne = 2 if (nb <= _CONV_STATIC_BLOCKS and b % 2 == 0) else 1
    slots = min(ne, 2)
    spectra = (2 * nb - 1 + slots * nb) * 2 * p * jnp.dtype(_SPECTRUM_DTYPE).itemsize
    seq_bytes = ne * l * 2
    dft_bytes = 2 * (2 * p * p * 2)
    budget = _V7X_VMEM_LIMIT_BYTES * 9 // 10
    tc, io_buffers = 128, 2
    for cand_tc, cand_buffers in ((256, 2), (256, 1)):
        if cand_tc * (spectra + (2 + 2 * cand_buffers) * seq_bytes) + dft_bytes <= budget:
            tc, io_buffers = cand_tc, cand_buffers
            break
    io_mode = {} if io_buffers == 2 else {"pipeline_mode": pl.Buffered(1)}
    seq = pl.BlockSpec((ne, l, tc), lambda c, bi: (bi, 0, c))
    seq_io = pl.BlockSpec((ne, l, tc), lambda c, bi: (bi, 0, c), **io_mode)
    kf_spec = pl.BlockSpec((2 * nb - 1, 2 * p, tc), lambda c, bi: (0, 0, c),
                           pipeline_mode=pl.Buffered(1))
    return pl.pallas_call(
        functools.partial(_long_conv_kernel, p=p, nb=nb, tc=tc, ne=ne), grid=(d // tc, b // ne),
        in_specs=[seq_io, seq, kf_spec, pl.BlockSpec((1, tc), lambda c, bi: (0, c)),
                  _const_spec(fwd_half.shape), _const_spec(inv.shape)],
        out_specs=seq_io, out_shape=jax.ShapeDtypeStruct((b, l, d), _BF16),
        scratch_shapes=[pltpu.VMEM((slots, nb, 2 * p, tc), _SPECTRUM_DTYPE),
                        pltpu.VMEM((2 * p, tc), _BF16),
                        pltpu.VMEM((2 * p, tc), _BF16)],
        compiler_params=_params("parallel", "arbitrary"), name="hyena_long_conv")(
            gate, z, kf, skip_o.astype(_F32).reshape(1, d), fwd_half, inv)


def _hyena_layer(x, norm_g, w_in, conv_w, conv_b, w1, b1, w2, b2, w3, b3, freq, w4, skip, w_out):
    b, l, d = x.shape
    p = min(l, max(_CONV_P, l // 8))
    x1, x2, v = _hyena_in(x, norm_g, w_in.astype(_BF16), conv_w, conv_b)
    k0, k1 = _hyena_two_sided_filters(l, w1, b1, w2, b2, w3, b3, freq, w4)
    fwd_half, inv = _dft_matrices(p)
    z = v
    for o, (gate, kt) in enumerate(((x1, k0), (x2, k1))):
        kf = _filter_spectrum(kt, fwd_half, p)
        z = _long_conv(gate, z, kf, skip[o], fwd_half, inv, p)
    return z.reshape(b * l, d), w_out.astype(_BF16)


def _trunk(x, rel_bias, ffn1_norm, ffn1_w, mixer_norm, attn_w_qkv, attn_w_o, hy, ffn2_norm, ffn2_w,
           final_norm):
    b, s, d = x.shape
    depth = ffn1_norm.shape[0]
    flat = lambda a: a.reshape(b * s, d)
    for i in range(depth):
        x = _ffn(flat(x), ffn1_norm[i], *ffn1_w, i).reshape(b, s, d)
        j = i // 2
        if i % 2 == 0:
            os_, stats, w_o = _dilated_attention_layer(x, mixer_norm[i], attn_w_qkv[j], attn_w_o[j],
                                                       rel_bias)
            pre = ("merge", os_, stats, s, w_o)
        else:
            pre = ("matmul", *_hyena_layer(x, mixer_norm[i], *[a[j] for a in hy]))
        fin = final_norm if i == depth - 1 else None
        x = _ffn(flat(x), ffn2_norm[i], *ffn2_w, i, pre=pre, final_g=fin).reshape(b, s, d)
    return x


def kernel(x_prompt, x_sample, rel_bias, ffn1_norm, ffn1_w_gate_up, ffn1_w_down, mixer_norm, attn_w_qkv, attn_w_o, hyena_w_in, hyena_conv_w, hyena_conv_b, hyena_filt_w1, hyena_filt_b1, hyena_filt_w2, hyena_filt_b2, hyena_filt_w3, hyena_filt_b3, hyena_filt_freq, hyena_filt_w4, hyena_skip, hyena_w_out, ffn2_norm, ffn2_w_gate_up, ffn2_w_down, final_norm):
    depth = ffn1_norm.shape[0]
    ffn1_w = (ffn1_w_gate_up.astype(_BF16), ffn1_w_down.astype(_BF16))
    ffn2_w = (ffn2_w_gate_up.astype(_BF16), ffn2_w_down.astype(_BF16))
    hy = (hyena_w_in, hyena_conv_w, hyena_conv_b, hyena_filt_w1, hyena_filt_b1, hyena_filt_w2,
          hyena_filt_b2, hyena_filt_w3, hyena_filt_b3, hyena_filt_freq, hyena_filt_w4, hyena_skip,
          hyena_w_out)
    args = (rel_bias, ffn1_norm, ffn1_w, mixer_norm, attn_w_qkv, attn_w_o, hy, ffn2_norm, ffn2_w,
            final_norm)
    return (_trunk(x_prompt, *args), _trunk(x_sample, *args))
```
